```python
import math
import jax, jax.numpy as jnp
from jax import lax
import numpy as np

D_MODEL = 1024
BATCH = 2
SEQ = 8192
DEPTH = 2
DEC_BATCH = 32
DEC_SEQ = 8
PAST_LEN = 8192
PAGE_SIZE = 128

HEAD_DIM = 64
ROPE_THETA = 500000.0
N_HEADS_A = 8
N_IDX_HEADS = 8
IDX_DIM = 64
TOPK_MAX = 256
N_HEADS_B = 4
N_HEADS_C = 16
BLOCK_Q = 128
FORGET_BIAS = 2.0
EPS = 1e-6
WIDTH_A = N_HEADS_A * HEAD_DIM
WIDTH_B = N_HEADS_B * 2 * HEAD_DIM
WIDTH_C = N_HEADS_C * HEAD_DIM
EVEN_SPLITS = (WIDTH_A, WIDTH_A, WIDTH_A, WIDTH_A, N_IDX_HEADS * IDX_DIM, IDX_DIM, N_IDX_HEADS, WIDTH_B, WIDTH_B, WIDTH_B, WIDTH_B)
ODD_SPLITS = (WIDTH_C, WIDTH_C, WIDTH_C, WIDTH_C, N_HEADS_C)
F32 = jnp.float32

kernel_name = 'hybrid_dsa_diff_fox_decode_step'


def _rms(x, g):
    xf = x.astype(F32)
    y = xf * lax.rsqrt(jnp.mean(xf * xf, axis=-1, keepdims=True) + EPS)
    return (y * g.astype(F32)).astype(x.dtype)


def _rope(x, pos):
    rot = x.shape[-1] // 4
    half = rot // 2
    inv = jnp.power(F32(ROPE_THETA), -jnp.arange(half, dtype=F32) * (2.0 / rot))
    ang = pos.astype(F32)[:, None] * inv[None, :]
    shape = (pos.shape[0],) + (1,) * (x.ndim - 3) + (half,)
    cos = jnp.cos(ang).reshape(shape)
    sin = jnp.sin(ang).reshape(shape)
    xf = x.astype(F32)
    x1 = xf[..., :half]
    x2 = xf[..., half:rot]
    out = jnp.concatenate([x1 * cos - x2 * sin, x2 * cos + x1 * sin, xf[..., rot:]], axis=-1)
    return out.astype(x.dtype)


def _split(t, sizes):
    return jnp.split(t, np.cumsum(sizes)[:-1].tolist(), axis=-1)


def _to_blocks(t):
    b, L = t.shape[0], t.shape[1]
    return jnp.swapaxes(t.reshape((b, L // BLOCK_Q, BLOCK_Q) + t.shape[2:]), 0, 1)


def _from_blocks(t):
    nb, b = t.shape[0], t.shape[1]
    return jnp.swapaxes(t, 0, 1).reshape((b, nb * t.shape[2]) + t.shape[3:])


def _gather_pages(pool, page_table):
    g = pool[page_table]
    return g.reshape((page_table.shape[0], page_table.shape[1] * pool.shape[1]) + pool.shape[2:])


def _take_rows(k, v, idx):
    return jax.vmap(lambda kb, vb, ib: (kb[ib], vb[ib]))(k, v, idx)


def _masked_softmax(logits, mask):
    return jax.nn.softmax(jnp.where(mask, logits, -jnp.inf), axis=-1)


def _apply_parts(p, v_parts, eq):
    outs = []
    start = 0
    for v in v_parts:
        n = v.shape[1]
        outs.append(jnp.einsum(eq, p[..., start:start + n].astype(v.dtype), v))
        start += n
    out = outs[0]
    for o in outs[1:]:
        out = out + o
    return out


def _dsa_attend(q, q_i, w_i, k_i, qpos, n_sel, gather):
    s = jnp.einsum('bqhd,bld->bqhl', q_i.astype(F32), k_i.astype(F32))
    score = jnp.einsum('bqhl,bqh->bql', jax.nn.relu(s), w_i)
    causal = jnp.arange(k_i.shape[1])[None, :] <= qpos[:, None]
    score = jnp.where(causal, score, -jnp.inf)
    top, idx = lax.top_k(score, n_sel)
    k_sel, v_sel = gather(idx)
    logits = jnp.einsum('bqhd,bqnhd->bqhn', q, k_sel).astype(F32) * (HEAD_DIM ** -0.5)
    p = _masked_softmax(logits, jnp.isfinite(top)[:, :, None, :])
    return jnp.einsum('bqhn,bqnhd->bqhd', p.astype(v_sel.dtype), v_sel)


def _diff_attend(q, k_parts, v_parts, mask_parts, lam):
    logits = jnp.concatenate([jnp.einsum('bqhcd,blhcd->bhcql', q, k) for k in k_parts], axis=-1).astype(F32) * (HEAD_DIM ** -0.5)
    p = _masked_softmax(logits, jnp.concatenate(mask_parts, axis=-1))
    a = p[:, :, 0] - lam * p[:, :, 1]
    return _apply_parts(a, v_parts, 'bhql,blhe->bqhe')


def _fox_attend(q, k_parts, v_parts, c_q, c_parts, mask_parts):
    logits = jnp.concatenate([jnp.einsum('bqhd,blhd->bhql', q, k) for k in k_parts], axis=-1).astype(F32) * (HEAD_DIM ** -0.5)
    c_k = jnp.concatenate(c_parts, axis=1)
    logits = logits + jnp.swapaxes(c_q, 1, 2)[:, :, :, None] - jnp.swapaxes(c_k, 1, 2)[:, :, None, :]
    p = _masked_softmax(logits, jnp.concatenate(mask_parts, axis=-1))
    return _apply_parts(p, v_parts, 'bhql,blhd->bqhd')


def _diff_lambda(p, lam_init):
    def e(a, c):
        return jnp.exp(jnp.sum(a.astype(F32) * c.astype(F32)))
    return e(p['lam_q1'], p['lam_k1']) - e(p['lam_q2'], p['lam_k2']) + lam_init


def _even_project(x, pos, p):
    b, L, _ = x.shape
    h = _rms(x, p['norm']) @ p['w_in']
    q_a, k_a, v_a, g_a, q_i, k_i, w_i, q_b, k_b, v_b, g_b = _split(h, EVEN_SPLITS)
    sa = (b, L, N_HEADS_A, HEAD_DIM)
    sb = (b, L, N_HEADS_B, 2, HEAD_DIM)
    q_a = _rope(_rms(q_a.reshape(sa), p['qn_a']), pos)
    k_a = _rope(_rms(k_a.reshape(sa), p['kn_a']), pos)
    q_i = _rope(q_i.reshape(b, L, N_IDX_HEADS, IDX_DIM), pos)
    k_i = _rope(_rms(k_i, p['kn_idx']), pos)
    w_i = w_i.astype(F32) * ((N_IDX_HEADS * IDX_DIM) ** -0.5)
    q_b = _rope(_rms(q_b.reshape(sb), p['qn_b']), pos)
    k_b = _rope(_rms(k_b.reshape(sb), p['kn_b']), pos)
    v_b = v_b.reshape(b, L, N_HEADS_B, 2 * HEAD_DIM)
    return q_a, k_a, v_a.reshape(sa), g_a, q_i, k_i, w_i, q_b, k_b, v_b, g_b


def _even_output(x, o_a, g_a, o_b, g_b, lam_init, p):
    b, L, _ = x.shape
    o_b = _rms(o_b, p['subln_b']) * (1.0 - lam_init)
    y = jnp.concatenate([o_a.reshape(b, L, WIDTH_A) * jax.nn.silu(g_a),
                         o_b.reshape(b, L, WIDTH_B) * jax.nn.silu(g_b)], axis=-1)
    return x + y @ p['w_out']


def _even_layer_prompt(x, layer, p):
    pos = jnp.arange(SEQ)
    q_a, k_a, v_a, g_a, q_i, k_i, w_i, q_b, k_b, v_b, g_b = _even_project(x, pos, p)
    lam_init = 0.8 - 0.6 * math.exp(-0.3 * layer)
    lam = _diff_lambda(p, lam_init)
    n_sel = min(TOPK_MAX, SEQ // 4)

    def gather(idx):
        return _take_rows(k_a, v_a, idx)

    def block(args):
        qa, qi, wi, qb, qpos = args
        o_a = _dsa_attend(qa, qi, wi, k_i, qpos, n_sel, gather)
        mask = pos[None, :] <= qpos[:, None]
        o_b = _diff_attend(qb, [k_b], [v_b], [mask], lam)
        return o_a, o_b

    o_a, o_b = lax.map(block, (_to_blocks(q_a), _to_blocks(q_i), _to_blocks(w_i), _to_blocks(q_b), pos.reshape(-1, BLOCK_Q)))
    y = _even_output(x, _from_blocks(o_a), g_a, _from_blocks(o_b), g_b, lam_init, p)
    return y, (k_a, v_a, k_i, k_b, v_b)


def _even_layer_sample(x, layer, caches, page_table, p):
    cache_k_a, cache_v_a, cache_k_i, cache_k_b, cache_v_b = caches
    pos = PAST_LEN + jnp.arange(DEC_SEQ)
    q_a, k_a, v_a, g_a, q_i, k_i, w_i, q_b, k_b, v_b, g_b = _even_project(x, pos, p)
    lam_init = 0.8 - 0.6 * math.exp(-0.3 * layer)
    lam = _diff_lambda(p, lam_init)
    n_sel = min(TOPK_MAX, (PAST_LEN + DEC_SEQ) // 4)
    k_i_all = jnp.concatenate([_gather_pages(cache_k_i, page_table).astype(k_i.dtype), k_i], axis=1)

    def gather(idx):
        b = idx.shape[0]
        is_past = (idx < PAST_LEN)[..., None, None]
        ip = jnp.minimum(idx, PAST_LEN - 1)
        phys = jnp.take_along_axis(page_table, (ip // PAGE_SIZE).reshape(b, -1), axis=1).reshape(idx.shape)
        off = ip % PAGE_SIZE
        kn, vn = _take_rows(k_a, v_a, jnp.clip(idx - PAST_LEN, 0, DEC_SEQ - 1))
        return (jnp.where(is_past, cache_k_a[phys, off].astype(kn.dtype), kn),
                jnp.where(is_past, cache_v_a[phys, off].astype(vn.dtype), vn))

    o_a = _dsa_attend(q_a, q_i, w_i, k_i_all, pos, n_sel, gather)
    masks = [jnp.ones((DEC_SEQ, PAST_LEN), bool), jnp.tril(jnp.ones((DEC_SEQ, DEC_SEQ), bool))]
    o_b = _diff_attend(q_b, [_gather_pages(cache_k_b, page_table).astype(k_b.dtype), k_b],
                       [_gather_pages(cache_v_b, page_table).astype(v_b.dtype), v_b], masks, lam)
    y = _even_output(x, o_a, g_a, o_b, g_b, lam_init, p)
    return y, (k_a, v_a, k_i, k_b, v_b)


def _odd_project(x, p):
    b, L, _ = x.shape
    h = _rms(x, p['norm']) @ p['w_in']
    q, k, v, g, f = _split(h, ODD_SPLITS)
    sc = (b, L, N_HEADS_C, HEAD_DIM)
    q = _rms(q.reshape(sc), p['qn'])
    k = _rms(k.reshape(sc), p['kn'])
    logf = jax.nn.log_sigmoid(f.astype(F32) + p['b_f'].astype(F32))
    return q, k, v.reshape(sc), g, logf


def _odd_output(x, o, g, p):
    b, L, _ = x.shape
    return x + (o.reshape(b, L, WIDTH_C) * jax.nn.silu(g)) @ p['w_out']


def _odd_layer_prompt(x, p):
    q, k, v, g, logf = _odd_project(x, p)
    c = jnp.cumsum(logf, axis=1)
    pos = jnp.arange(SEQ)

    def block(args):
        qb, cqb, qpos = args
        mask = pos[None, :] <= qpos[:, None]
        return _fox_attend(qb, [k], [v], cqb, [c], [mask])

    o = _from_blocks(lax.map(block, (_to_blocks(q), _to_blocks(c), pos.reshape(-1, BLOCK_Q))))
    return _odd_output(x, o, g, p), (k, v, logf.astype(x.dtype))


def _odd_layer_sample(x, caches, page_table, p):
    cache_k, cache_v, cache_logf = caches
    q, k, v, g, logf = _odd_project(x, p)
    c_past = jnp.cumsum(_gather_pages(cache_logf, page_table).astype(F32), axis=1)
    c_new = c_past[:, -1:] + jnp.cumsum(logf, axis=1)
    masks = [jnp.ones((DEC_SEQ, PAST_LEN), bool), jnp.tril(jnp.ones((DEC_SEQ, DEC_SEQ), bool))]
    o = _fox_attend(q, [_gather_pages(cache_k, page_table).astype(k.dtype), k],
                    [_gather_pages(cache_v, page_table).astype(v.dtype), v], c_new, [c_past, c_new], masks)
    return _odd_output(x, o, g, p), (k, v, logf.astype(x.dtype))


def setup_inputs(seed: int = 0) -> dict:
    key = jax.random.key(seed)
    keys = iter(jax.random.split(key, 48))

    def nrm(shape, scale=1.0):
        return scale * jax.random.normal(next(keys), shape, F32)

    def gain(n):
        return 1.0 + 0.02 * nrm((n,))

    n_pages = PAST_LEN // PAGE_SIZE
    n_pool = (DEC_BATCH * n_pages * 5) // 4
    row = (n_pool, PAGE_SIZE)
    ws = D_MODEL ** -0.5
    inp = {}
    inp['x_prompt'] = nrm((BATCH, SEQ, D_MODEL))
    inp['x_sample'] = nrm((DEC_BATCH, DEC_SEQ, D_MODEL))
    inp['cache_l0_k_a'] = nrm(row + (N_HEADS_A, HEAD_DIM))
    inp['cache_l0_v_a'] = nrm(row + (N_HEADS_A, HEAD_DIM))
    inp['cache_l0_k_idx'] = nrm(row + (IDX_DIM,))
    inp['cache_l0_k_b'] = nrm(row + (N_HEADS_B, 2, HEAD_DIM))
    inp['cache_l0_v_b'] = nrm(row + (N_HEADS_B, 2 * HEAD_DIM))
    inp['cache_l1_k_c'] = nrm(row + (N_HEADS_C, HEAD_DIM))
    inp['cache_l1_v_c'] = nrm(row + (N_HEADS_C, HEAD_DIM))
    inp['cache_l1_logf_c'] = jax.nn.log_sigmoid(nrm(row + (N_HEADS_C,)) + FORGET_BIAS)
    perm = jax.random.permutation(next(keys), n_pool)
    inp['page_table'] = perm[: DEC_BATCH * n_pages].reshape(DEC_BATCH, n_pages).astype(jnp.int32)
    inp['l0_norm'] = gain(D_MODEL)
    inp['l0_w_in'] = nrm((D_MODEL, sum(EVEN_SPLITS)), ws)
    inp['l0_qn_a'] = gain(HEAD_DIM)
    inp['l0_kn_a'] = gain(HEAD_DIM)
    inp['l0_kn_idx'] = gain(IDX_DIM)
    inp['l0_qn_b'] = gain(HEAD_DIM)
    inp['l0_kn_b'] = gain(HEAD_DIM)
    inp['l0_lam_q1'] = nrm((HEAD_DIM,), 0.1)
    inp['l0_lam_k1'] = nrm((HEAD_DIM,), 0.1)
    inp['l0_lam_q2'] = nrm((HEAD_DIM,), 0.1)
    inp['l0_lam_k2'] = nrm((HEAD_DIM,), 0.1)
    inp['l0_subln_b'] = gain(2 * HEAD_DIM)
    inp['l0_w_out'] = nrm((WIDTH_A + WIDTH_B, D_MODEL), (WIDTH_A + WIDTH_B) ** -0.5)
    inp['l1_norm'] = gain(D_MODEL)
    inp['l1_w_in'] = nrm((D_MODEL, sum(ODD_SPLITS)), ws)
    inp['l1_b_f'] = FORGET_BIAS + nrm((N_HEADS_C,), 0.1)
    inp['l1_qn'] = gain(HEAD_DIM)
    inp['l1_kn'] = gain(HEAD_DIM)
    inp['l1_w_out'] = nrm((WIDTH_C, D_MODEL), WIDTH_C ** -0.5)
    return inp


def reference(x_prompt, x_sample, cache_l0_k_a, cache_l0_v_a, cache_l0_k_idx, cache_l0_k_b, cache_l0_v_b,
              cache_l1_k_c, cache_l1_v_c, cache_l1_logf_c, page_table,
              l0_norm, l0_w_in, l0_qn_a, l0_kn_a, l0_kn_idx, l0_qn_b, l0_kn_b,
              l0_lam_q1, l0_lam_k1, l0_lam_q2, l0_lam_k2, l0_subln_b, l0_w_out,
              l1_norm, l1_w_in, l1_b_f, l1_qn, l1_kn, l1_w_out):
    params = (
        dict(norm=l0_norm, w_in=l0_w_in, qn_a=l0_qn_a, kn_a=l0_kn_a, kn_idx=l0_kn_idx, qn_b=l0_qn_b, kn_b=l0_kn_b,
             lam_q1=l0_lam_q1, lam_k1=l0_lam_k1, lam_q2=l0_lam_q2, lam_k2=l0_lam_k2, subln_b=l0_subln_b, w_out=l0_w_out),
        dict(norm=l1_norm, w_in=l1_w_in, b_f=l1_b_f, qn=l1_qn, kn=l1_kn, w_out=l1_w_out),
    )
    caches = (
        (cache_l0_k_a, cache_l0_v_a, cache_l0_k_idx, cache_l0_k_b, cache_l0_v_b),
        (cache_l1_k_c, cache_l1_v_c, cache_l1_logf_c),
    )
    xp, xs = x_prompt, x_sample
    new_p, new_s = [], []
    for layer in range(DEPTH):
        if layer % 2 == 0:
            xp, sp = _even_layer_prompt(xp, layer, params[layer])
            xs, ss = _even_layer_sample(xs, layer, caches[layer], page_table, params[layer])
        else:
            xp, sp = _odd_layer_prompt(xp, params[layer])
            xs, ss = _odd_layer_sample(xs, caches[layer], page_table, params[layer])
        new_p.append(sp)
        new_s.append(ss)
    (p_k_a, p_v_a, p_k_idx, p_k_b, p_v_b), (p_k_c, p_v_c, p_logf_c) = new_p
    (s_k_a, s_v_a, s_k_idx, s_k_b, s_v_b), (s_k_c, s_v_c, s_logf_c) = new_s
    return (xp, xs, p_k_a, s_k_a, p_v_a, s_v_a, p_k_idx, s_k_idx, p_k_b, s_k_b, p_v_b, s_v_b,
            p_k_c, s_k_c, p_v_c, s_v_c, p_logf_c, s_logf_c)
```

```python
import functools
import math

import jax
import jax.numpy as jnp
from jax import lax
from jax.experimental import pallas as pl
from jax.experimental.pallas import tpu as pltpu

F32 = jnp.float32
BF16 = jnp.bfloat16
I32 = jnp.int32

HEAD_DIM = 64
ROPE_THETA = 500000.0
N_HEADS_A = 8
N_IDX_HEADS = 8
IDX_DIM = 64
TOPK_MAX = 256
N_HEADS_B = 4
N_HEADS_C = 16
EPS = 1e-6
WIDTH_A = N_HEADS_A * HEAD_DIM
WIDTH_B = N_HEADS_B * 2 * HEAD_DIM
WIDTH_C = N_HEADS_C * HEAD_DIM
WIDTH_I = N_IDX_HEADS * IDX_DIM
QK_SCALE = HEAD_DIM ** -0.5

LANES = 128
SUBLANES = 8
MXU_DIM = 256
VMEM_LIMIT = 56 * 1024 * 1024
ROW_TILE = 256
Q_TILE = 256
K_TILE = 512

NEG = -1e30
HIGHEST = lax.Precision.HIGHEST
NT_DIMS = (((1,), (1,)), ((), ()))

KEY_NEG_INF = -2139095041 - 0


def _cparams(n_axes):
    return pltpu.CompilerParams(
        dimension_semantics=("arbitrary",) * n_axes, vmem_limit_bytes=VMEM_LIMIT)


def _div_pow2(x, d):
    assert d & (d - 1) == 0
    return lax.shift_right_logical(x, jnp.int32(d.bit_length() - 1))


def _sort_key(x):
    bits = pltpu.bitcast(x, I32)
    return bits ^ ((bits >> 31) & jnp.int32(0x7FFFFFFF))


def _rms_rows(x, g):
    ms = jnp.mean(x * x, axis=-1, keepdims=True)
    return x * lax.rsqrt(ms + EPS) * g


def _head_norm(h, bd, gain):
    w = h.shape[-1]
    hs = h * h
    if w >= MXU_DIM:
        cols = [jnp.dot(hs[:, c * MXU_DIM:(c + 1) * MXU_DIM], bd, precision=HIGHEST,
                        preferred_element_type=F32) for c in range(w // MXU_DIM)]
        ms = cols[0] if len(cols) == 1 else jnp.concatenate(cols, axis=-1)
    else:
        ms = jnp.dot(hs, bd[:w, :w], precision=HIGHEST, preferred_element_type=F32)
    return h * lax.rsqrt(ms + EPS) * gain


def _rope(y, rope_ref):
    c = rope_ref[:, 0:LANES]
    s_lo = rope_ref[:, LANES:2 * LANES]
    s_hi = rope_ref[:, 2 * LANES:3 * LANES]
    outs = []
    for j in range(y.shape[-1] // LANES):
        yc = y[:, j * LANES:(j + 1) * LANES]
        outs.append(yc * c + pltpu.roll(yc, LANES - 8, 1) * s_lo + pltpu.roll(yc, 8, 1) * s_hi)
    return outs[0] if len(outs) == 1 else jnp.concatenate(outs, axis=-1)


def _proj0_kernel(x_ref, g_ref, w_ref, ws_ref, rope_ref, gains_ref, gki_ref, bd_ref,
                  qa_ref, ka_ref, ka16_ref, va_ref, va16_ref, ga_ref,
                  qi_ref, ki_ref, ki16_ref, wi_ref,
                  qb_ref, kb_ref, kb16_ref, vb_ref, vb16_ref, gb_ref):
    xb = _rms_rows(x_ref[...], g_ref[...]).astype(BF16)
    bd = bd_ref[...]
    w512 = WIDTH_A

    def piece(j):
        return jnp.dot(xb, w_ref[:, j * w512:(j + 1) * w512], preferred_element_type=F32)

    q_a = _rope(_head_norm(piece(0), bd, gains_ref[0:1, :]), rope_ref)
    qa_ref[...] = (q_a * QK_SCALE).astype(BF16)
    k_a = _rope(_head_norm(piece(1), bd, gains_ref[1:2, :]), rope_ref)
    ka_ref[...] = k_a
    ka16_ref[...] = k_a.astype(BF16)
    v_a = piece(2)
    va_ref[...] = v_a
    va16_ref[...] = v_a.astype(BF16)
    ga_ref[...] = piece(3)
    qi_ref[...] = _rope(piece(4), rope_ref).astype(BF16)
    q_b = _rope(_head_norm(piece(5), bd, gains_ref[2:3, :]), rope_ref)
    qb_ref[...] = (q_b * QK_SCALE).astype(BF16)
    k_b = _rope(_head_norm(piece(6), bd, gains_ref[3:4, :]), rope_ref)
    kb_ref[...] = k_b
    kb16_ref[...] = k_b.astype(BF16)
    v_b = piece(7)
    vb_ref[...] = v_b
    vb16_ref[...] = v_b.astype(BF16)
    gb_ref[...] = piece(8)
    hs = jnp.dot(xb, ws_ref[...], preferred_element_type=F32)
    k_i = _rope(_head_norm(hs[:, 0:LANES], bd, gki_ref[...]), rope_ref)
    ki_ref[...] = k_i[:, 0:IDX_DIM]
    ki16_ref[...] = k_i.astype(BF16)
    wi_ref[...] = hs[:, LANES:LANES + N_IDX_HEADS] * (WIDTH_I ** -0.5)


def _proj1_kernel(x_ref, g_ref, w_ref, wf_ref, bf_ref, gains_ref, bd_ref,
                  q_ref, k_ref, k16_ref, v_ref, v16_ref, gate_ref, logf_ref):
    xb = _rms_rows(x_ref[...], g_ref[...]).astype(BF16)
    bd = bd_ref[...]
    wc = WIDTH_C

    def piece(j):
        return jnp.dot(xb, w_ref[:, j * wc:(j + 1) * wc], preferred_element_type=F32)

    q_ref[...] = (_head_norm(piece(0), bd, gains_ref[0:1, :]) * QK_SCALE).astype(BF16)
    k = _head_norm(piece(1), bd, gains_ref[1:2, :])
    k_ref[...] = k
    k16_ref[...] = k.astype(BF16)
    v = piece(2)
    v_ref[...] = v
    v16_ref[...] = v.astype(BF16)
    gate_ref[...] = piece(3)
    f = jnp.dot(xb, wf_ref[...], preferred_element_type=F32)[:, 0:N_HEADS_C] + bf_ref[...]
    logf_ref[...] = jnp.minimum(f, 0.0) - jnp.log1p(jnp.exp(-jnp.abs(f)))


def _row_spec(tm, w):
    return pl.BlockSpec((tm, w), lambda i: (i, 0))


def _const_spec(shape):
    return pl.BlockSpec(shape, lambda i: (0,) * len(shape))


def _block_diag_mean():
    r = lax.broadcasted_iota(I32, (MXU_DIM, MXU_DIM), 0) // HEAD_DIM
    c = lax.broadcasted_iota(I32, (MXU_DIM, MXU_DIM), 1) // HEAD_DIM
    return jnp.where(r == c, 1.0 / HEAD_DIM, 0.0).astype(F32)


def _rope_table(pos):
    rot = HEAD_DIM // 4
    half = rot // 2
    inv = jnp.power(F32(ROPE_THETA), -jnp.arange(half, dtype=F32) * (2.0 / rot))
    ang = pos.astype(F32)[:, None] * inv[None, :]
    cos, sin = jnp.cos(ang), jnp.sin(ang)
    m = pos.shape[0]
    ones = jnp.ones((m, HEAD_DIM - rot), F32)
    zeros = jnp.zeros((m, HEAD_DIM - rot), F32)
    zh = jnp.zeros((m, half), F32)
    c = jnp.concatenate([cos, cos, ones], axis=-1)
    s_lo = jnp.concatenate([-sin, zh, zeros], axis=-1)
    s_hi = jnp.concatenate([zh, sin, zeros], axis=-1)
    return jnp.concatenate([c, c, s_lo, s_lo, s_hi, s_hi], axis=-1)


def _tile_gain(g, w):
    return jnp.tile(g.astype(F32), w // g.shape[0])[None, :]


def _proj0(x2d, pos, p, tm):
    m, d = x2d.shape
    sizes = (WIDTH_A,) * 4 + (WIDTH_I, IDX_DIM, N_IDX_HEADS) + (WIDTH_B,) * 4
    offs = [0]
    for s in sizes:
        offs.append(offs[-1] + s)
    w_in = p["w_in"]
    cols = [w_in[:, offs[i]:offs[i + 1]] for i in range(len(sizes))]
    w_big = jnp.concatenate([cols[0], cols[1], cols[2], cols[3], cols[4], cols[7], cols[8], cols[9], cols[10]],
                            axis=1).astype(BF16)
    w_small = jnp.concatenate(
        [cols[5], cols[5], cols[6], jnp.zeros((d, LANES - N_IDX_HEADS), w_in.dtype)], axis=1).astype(BF16)
    gains = jnp.concatenate([_tile_gain(p["qn_a"], WIDTH_A), _tile_gain(p["kn_a"], WIDTH_A),
                             _tile_gain(p["qn_b"], WIDTH_B), _tile_gain(p["kn_b"], WIDTH_B)], axis=0)
    gki = _tile_gain(p["kn_idx"], LANES)
    rope = _rope_table(pos)
    w5 = WIDTH_A
    f32o = lambda w: jax.ShapeDtypeStruct((m, w), F32)
    b16o = lambda w: jax.ShapeDtypeStruct((m, w), BF16)
    out_shape = (b16o(w5), f32o(w5), b16o(w5), f32o(w5), b16o(w5), f32o(w5),
                 b16o(w5), f32o(IDX_DIM), b16o(LANES), f32o(N_IDX_HEADS),
                 b16o(w5), f32o(w5), b16o(w5), f32o(w5), b16o(w5), f32o(w5))
    out_specs = tuple(_row_spec(tm, s.shape[1]) for s in out_shape)
    return pl.pallas_call(
        _proj0_kernel,
        grid=(m // tm,),
        in_specs=[_row_spec(tm, d), _const_spec((1, d)), _const_spec(w_big.shape), _const_spec(w_small.shape),
                  _row_spec(tm, 3 * LANES), _const_spec(gains.shape), _const_spec(gki.shape),
                  _const_spec((MXU_DIM, MXU_DIM))],
        out_specs=out_specs,
        out_shape=out_shape,
        compiler_params=_cparams(1),
        name="proj0",
    )(x2d, p["norm"].astype(F32)[None, :], w_big, w_small, rope, gains, gki, _block_diag_mean())


def _proj1(x2d, p, tm):
    m, d = x2d.shape
    wc = WIDTH_C
    w_in = p["w_in"]
    w_big = w_in[:, :4 * wc].astype(BF16)
    w_f = jnp.concatenate([w_in[:, 4 * wc:], jnp.zeros((d, LANES - N_HEADS_C), w_in.dtype)], axis=1).astype(BF16)
    gains = jnp.concatenate([_tile_gain(p["qn"], wc), _tile_gain(p["kn"], wc)], axis=0)
    f32o = lambda w: jax.ShapeDtypeStruct((m, w), F32)
    b16o = lambda w: jax.ShapeDtypeStruct((m, w), BF16)
    out_shape = (b16o(wc), f32o(wc), b16o(wc), f32o(wc), b16o(wc), f32o(wc), f32o(N_HEADS_C))
    out_specs = tuple(_row_spec(tm, s.shape[1]) for s in out_shape)
    return pl.pallas_call(
        _proj1_kernel,
        grid=(m // tm,),
        in_specs=[_row_spec(tm, d), _const_spec((1, d)), _const_spec(w_big.shape), _const_spec(w_f.shape),
                  _const_spec((1, N_HEADS_C)), _const_spec(gains.shape), _const_spec((MXU_DIM, MXU_DIM))],
        out_specs=out_specs,
        out_shape=out_shape,
        compiler_params=_cparams(1),
        name="proj1",
    )(x2d, p["norm"].astype(F32)[None, :], w_big, w_f, p["b_f"].astype(F32)[None, :], gains, _block_diag_mean())


def _silu(g):
    return g * jax.nn.sigmoid(g)


def _out0_kernel(x_ref, oa_ref, ga_ref, ob_ref, gb_ref, sub_ref, w_ref, y_ref, *, post_scale):
    ya = (oa_ref[...] * _silu(ga_ref[...])).astype(BF16)
    ob = ob_ref[...]
    cols = []
    for h in range(N_HEADS_B):
        oc = ob[:, h * LANES:(h + 1) * LANES]
        ms = jnp.mean(oc * oc, axis=-1, keepdims=True)
        cols.append(oc * lax.rsqrt(ms + EPS))
    obn = jnp.concatenate(cols, axis=-1) * sub_ref[...] * post_scale
    yb = (obn * _silu(gb_ref[...])).astype(BF16)
    y = jnp.dot(ya, w_ref[0:WIDTH_A, :], preferred_element_type=F32)
    y = y + jnp.dot(yb, w_ref[WIDTH_A:WIDTH_A + WIDTH_B, :], preferred_element_type=F32)
    y_ref[...] = x_ref[...] + y


def _out1_kernel(x_ref, o_ref, g_ref, w_ref, y_ref):
    yo = (o_ref[...] * _silu(g_ref[...])).astype(BF16)
    y_ref[...] = x_ref[...] + jnp.dot(yo, w_ref[...], preferred_element_type=F32)


def _out0(x2d, o_a, g_a, o_b, g_b, p, lam_init, tm):
    m, d = x2d.shape
    sub = _tile_gain(p["subln_b"], WIDTH_B)
    w = p["w_out"].astype(BF16)
    return pl.pallas_call(
        functools.partial(_out0_kernel, post_scale=1.0 - lam_init),
        grid=(m // tm,),
        in_specs=[_row_spec(tm, d), _row_spec(tm, WIDTH_A), _row_spec(tm, WIDTH_A), _row_spec(tm, WIDTH_B),
                  _row_spec(tm, WIDTH_B), _const_spec(sub.shape), _const_spec(w.shape)],
        out_specs=_row_spec(tm, d),
        out_shape=jax.ShapeDtypeStruct((m, d), F32),
        compiler_params=_cparams(1),
        name="out0",
    )(x2d, o_a, g_a, o_b, g_b, sub, w)


def _out1(x2d, o, g, p, tm):
    m, d = x2d.shape
    w = p["w_out"].astype(BF16)
    return pl.pallas_call(
        _out1_kernel,
        grid=(m // tm,),
        in_specs=[_row_spec(tm, d), _row_spec(tm, WIDTH_C), _row_spec(tm, WIDTH_C), _const_spec(w.shape)],
        out_specs=_row_spec(tm, d),
        out_shape=jax.ShapeDtypeStruct((m, d), F32),
        compiler_params=_cparams(1),
        name="out1",
    )(x2d, o, g, w)


def _select_topk(load_keys, n_chunks, groups, rows, nsel, idx_bits, idx_of):
    shape = (rows, LANES)

    def count(pred):
        def body(c, acc):
            blk = load_keys(c)
            for g in range(groups):
                acc = acc + jnp.where(pred(blk[:, g * LANES:(g + 1) * LANES], c, g), 1.0, 0.0)
            return acc
        acc = lax.fori_loop(0, n_chunks, body, jnp.zeros(shape, F32))
        return jnp.broadcast_to(jnp.sum(acc, axis=-1, keepdims=True), shape)

    kf = float(nsel)

    def bit_step(i, thr):
        bit = lax.shift_left(jnp.int32(1), jnp.int32(31) - i)
        cand = thr ^ bit
        n_ge = count(lambda k, c, g: k >= cand)
        return jnp.where(n_ge >= kf, cand, thr)

    thr = lax.fori_loop(0, 32, bit_step, jnp.full(shape, jnp.iinfo(jnp.int32).min, I32))
    thr = jnp.maximum(thr, KEY_NEG_INF + 1)
    n_gt = count(lambda k, c, g: k > thr)
    n_ge = count(lambda k, c, g: k >= thr)
    need = kf - n_gt
    excess = n_ge - kf

    def tie_cut():
        def idx_step(i, cut):
            bit = lax.shift_left(jnp.int32(1), jnp.int32(idx_bits - 1) - i)
            cand = cut | bit
            n_lt = count(lambda k, c, g: (k == thr) & (idx_of(c, g) < cand))
            return jnp.where(n_lt < need, cand, cut)
        return lax.fori_loop(0, idx_bits, idx_step, jnp.zeros(shape, I32))

    big = jnp.full(shape, jnp.iinfo(jnp.int32).max, I32)
    any_excess = jnp.max(excess) > 0.0
    cut = lax.cond(any_excess, lambda: jnp.where(excess > 0.0, tie_cut(), big), lambda: big)
    return thr, cut


def _pair_masks(rows):
    lane = lax.broadcasted_iota(I32, (rows, LANES), 1)
    return lane < HEAD_DIM


def _idx_prompt_kernel(qi_ref, wi_ref, ki_ref, bias_ref, qm_ref, wb_ref, s_ref, *, tq, tk, nsel, seq):
    qi = pl.program_id(1)
    q0 = qi * tq
    n_chunks = seq // tk
    nc = (q0 + tq + tk - 1) // tk
    lo = _pair_masks(tq)
    groups = tk // LANES
    for j in range(N_IDX_HEADS // 2):
        pair = qi_ref[0, :, j * LANES:(j + 1) * LANES]
        zero = jnp.zeros_like(pair)
        qm_ref[2 * j] = jnp.where(lo, pair, zero)
        qm_ref[2 * j + 1] = jnp.where(lo, zero, pair)
    w = wi_ref[0]
    for h in range(N_IDX_HEADS):
        wb_ref[h] = jnp.broadcast_to(w[:, h:h + 1], (tq, LANES))
    row = q0 + lax.broadcasted_iota(I32, (tq, tk), 0)
    col0 = lax.broadcasted_iota(I32, (tq, tk), 1)

    def score_body(c, carry):
        start = pl.multiple_of(c * tk, tk)
        kblk = ki_ref[0, pl.ds(start, tk), :]
        acc = jnp.zeros((tq, tk), F32)
        for h in range(N_IDX_HEADS):
            s = lax.dot_general(qm_ref[h], kblk, NT_DIMS, preferred_element_type=F32)
            acc = acc + jnp.maximum(s, 0.0) * jnp.tile(wb_ref[h], (1, groups))
        acc = jnp.where(col0 + c * tk <= row, acc, -jnp.inf)
        s_ref[c] = _sort_key(acc)
        return carry

    lax.fori_loop(0, nc, score_body, 0)

    lane = lax.broadcasted_iota(I32, (tq, LANES), 1)

    def idx_of(c, g):
        return lane + (c * tk + g * LANES)

    thr, cut = _select_topk(lambda c: s_ref[c], nc, groups, tq, nsel, int(math.log2(seq)) + 1, idx_of)
    thr_t = jnp.tile(thr, (1, groups))
    cut_t = jnp.tile(cut, (1, groups))

    def write_body(c, carry):
        k = s_ref[c]
        sel = (k > thr_t) | ((k == thr_t) & (col0 + c * tk <= cut_t))
        bias_ref[0, c] = jnp.where(sel, 0.0, NEG).astype(BF16)
        return carry

    lax.fori_loop(0, nc, write_body, 0)

    def fill_body(c, carry):
        bias_ref[0, c] = jnp.full((tq, tk), NEG, BF16)
        return carry

    lax.fori_loop(nc, n_chunks, fill_body, 0)


def _idx_prompt(qi16, wi, ki16, nsel, tq, tk):
    b, seq, _ = qi16.shape
    n_chunks = seq // tk
    return pl.pallas_call(
        functools.partial(_idx_prompt_kernel, tq=tq, tk=tk, nsel=nsel, seq=seq),
        grid=(b, seq // tq),
        in_specs=[pl.BlockSpec((1, tq, WIDTH_I), lambda bi, qi: (bi, qi, 0)),
                  pl.BlockSpec((1, tq, N_IDX_HEADS), lambda bi, qi: (bi, qi, 0)),
                  pl.BlockSpec((1, seq, LANES), lambda bi, qi: (bi, 0, 0))],
        out_specs=pl.BlockSpec((1, n_chunks, tq, tk), lambda bi, qi: (bi, 0, qi, 0)),
        out_shape=jax.ShapeDtypeStruct((b, n_chunks, seq, tk), BF16),
        scratch_shapes=[pltpu.VMEM((N_IDX_HEADS, tq, LANES), BF16),
                        pltpu.VMEM((N_IDX_HEADS, tq, LANES), F32),
                        pltpu.VMEM((n_chunks, tq, tk), I32)],
        compiler_params=_cparams(2),
        name="idx_prompt",
    )(qi16, wi, ki16)


def _flash_prompt_kernel(*refs, variant, n_pairs, tq, tk):
    it = iter(refs)
    q_ref, k_ref, v_ref = next(it), next(it), next(it)
    bias_ref = next(it) if variant == "dsa" else None
    cq_ref = next(it) if variant == "fox" else None
    ck_ref = next(it) if variant == "fox" else None
    lam_ref = next(it) if variant == "diff" else None
    o_ref = next(it)
    qm_ref, m_ref, l_ref, acc_ref = next(it), next(it), next(it), next(it)
    cqb_ref = next(it) if variant == "fox" else None

    qi = pl.program_id(1)
    kc = pl.program_id(2)
    nk = pl.num_programs(2)
    last = ((qi + 1) * tq - 1) // tk
    n_units = 2 * n_pairs
    groups = tk // LANES
    lo = _pair_masks(tq)

    @pl.when(kc == 0)
    def _init():
        for j in range(n_pairs):
            pair = q_ref[0, :, j * LANES:(j + 1) * LANES]
            zero = jnp.zeros_like(pair)
            qm_ref[2 * j] = jnp.where(lo, pair, zero)
            qm_ref[2 * j + 1] = jnp.where(lo, zero, pair)
        m_ref[...] = jnp.full(m_ref.shape, NEG, F32)
        l_ref[...] = jnp.zeros(l_ref.shape, F32)
        acc_ref[...] = jnp.zeros(acc_ref.shape, F32)
        if variant == "fox":
            cq = cq_ref[0]
            for u in range(n_units):
                cqb_ref[u] = jnp.broadcast_to(cq[:, u:u + 1], (tq, LANES))

    def compute(masked):
        if variant == "dsa":
            bias = bias_ref[0, 0].astype(F32)
            masked = False
        if masked:
            row = qi * tq + lax.broadcasted_iota(I32, (tq, tk), 0)
            col = kc * tk + lax.broadcasted_iota(I32, (tq, tk), 1)
            causal = col <= row
        for j in range(n_pairs):
            kp = k_ref[0, :, j * LANES:(j + 1) * LANES]
            vp = v_ref[0, :, j * LANES:(j + 1) * LANES]
            alphas, pvs = [], []
            for e in range(2):
                u = 2 * j + e
                s = lax.dot_general(qm_ref[u], kp, NT_DIMS, preferred_element_type=F32)
                if variant == "dsa":
                    s = s + bias
                if variant == "fox":
                    s = s + (jnp.tile(cqb_ref[u], (1, groups)) - ck_ref[0, u:u + 1, :])
                if masked:
                    s = jnp.where(causal, s, NEG)
                m_prev = m_ref[u]
                m_new = jnp.maximum(m_prev, jnp.max(s, axis=-1, keepdims=True))
                alpha = jnp.exp(m_prev - m_new)
                p = jnp.exp(s - jnp.tile(m_new, (1, groups)))
                l_ref[u] = alpha * l_ref[u] + jnp.sum(p, axis=-1, keepdims=True)
                m_ref[u] = m_new
                pv = jnp.dot(p.astype(BF16), vp, preferred_element_type=F32)
                if variant == "diff":
                    acc_ref[u] = alpha * acc_ref[u] + pv
                else:
                    alphas.append(alpha)
                    pvs.append(pv)
            if variant != "diff":
                acc_ref[j] = jnp.where(lo, alphas[0], alphas[1]) * acc_ref[j] + jnp.where(lo, pvs[0], pvs[1])

    needs_mask = (kc + 1) * tk - 1 > qi * tq

    @pl.when((kc <= last) & needs_mask)
    def _diag():
        compute(True)

    @pl.when((kc <= last) & jnp.logical_not(needs_mask))
    def _full():
        compute(False)

    @pl.when(kc == nk - 1)
    def _fin():
        for j in range(n_pairs):
            if variant == "diff":
                lam = lam_ref[0, 0]
                o_ref[0, :, j * LANES:(j + 1) * LANES] = (
                    acc_ref[2 * j] / l_ref[2 * j] - lam * (acc_ref[2 * j + 1] / l_ref[2 * j + 1]))
            else:
                o_ref[0, :, j * LANES:(j + 1) * LANES] = acc_ref[j] / jnp.where(lo, l_ref[2 * j], l_ref[2 * j + 1])


def _flash_prompt(variant, q16, k16, v16, tq, tk, bias=None, cq=None, ck=None, lam=None):
    b, seq, w = q16.shape
    n_pairs = w // LANES
    n_units = 2 * n_pairs

    def last_of(qi):
        return ((qi + 1) * tq - 1) // tk

    q_spec = pl.BlockSpec((1, tq, w), lambda bi, qi, kc: (bi, qi, 0))
    kv_spec = pl.BlockSpec((1, tk, w), lambda bi, qi, kc: (bi, jnp.minimum(kc, last_of(qi)), 0))
    in_specs = [q_spec, kv_spec, kv_spec]
    args = [q16, k16, v16]
    scratch = [pltpu.VMEM((n_units, tq, LANES), BF16), pltpu.VMEM((n_units, tq, LANES), F32),
               pltpu.VMEM((n_units, tq, LANES), F32),
               pltpu.VMEM((n_units if variant == "diff" else n_pairs, tq, LANES), F32)]
    if variant == "dsa":
        in_specs.append(pl.BlockSpec((1, 1, tq, tk), lambda bi, qi, kc: (bi, jnp.minimum(kc, last_of(qi)), qi, 0)))
        args.append(bias)
    if variant == "fox":
        in_specs.append(pl.BlockSpec((1, tq, n_units), lambda bi, qi, kc: (bi, qi, 0)))
        in_specs.append(pl.BlockSpec((1, n_units, tk), lambda bi, qi, kc: (bi, 0, jnp.minimum(kc, last_of(qi)))))
        args += [cq, ck]
        scratch.append(pltpu.VMEM((n_units, tq, LANES), F32))
    if variant == "diff":
        in_specs.append(pl.BlockSpec(memory_space=pltpu.SMEM))
        args.append(lam)
    return pl.pallas_call(
        functools.partial(_flash_prompt_kernel, variant=variant, n_pairs=n_pairs, tq=tq, tk=tk),
        grid=(b, seq // tq, seq // tk),
        in_specs=in_specs,
        out_specs=pl.BlockSpec((1, tq, w), lambda bi, qi, kc: (bi, qi, 0)),
        out_shape=jax.ShapeDtypeStruct((b, seq, w), F32),
        scratch_shapes=scratch,
        compiler_params=_cparams(3),
        name="flash_" + variant,
    )(*args)


def _cumsum_kernel(pt_ref, x_ref, xn_ref, c_ref, cn_ref, carry_ref):
    del pt_ref
    p = pl.program_id(1)
    n_pages = pl.num_programs(1)
    r = lax.broadcasted_iota(I32, (LANES, LANES), 0)
    c = lax.broadcasted_iota(I32, (LANES, LANES), 1)
    upper = jnp.where(r <= c, 1.0, 0.0).astype(F32)

    @pl.when(p == 0)
    def _():
        carry_ref[...] = jnp.zeros(carry_ref.shape, F32)

    cs = jnp.dot(x_ref[0], upper, precision=HIGHEST, preferred_element_type=F32) + carry_ref[...]
    c_ref[0] = cs
    total = jnp.broadcast_to(cs[:, LANES - 1:LANES], cs.shape)
    carry_ref[...] = total

    @pl.when(p == n_pages - 1)
    def _():
        cn_ref[0] = jnp.dot(xn_ref[0], upper, precision=HIGHEST, preferred_element_type=F32) + total


def _cumsum_pages(page_table, pool_t, new_t):
    b, n_pages = page_table.shape
    h = pool_t.shape[1]
    grid_spec = pltpu.PrefetchScalarGridSpec(
        num_scalar_prefetch=1,
        grid=(b, n_pages),
        in_specs=[pl.BlockSpec((1, h, LANES), lambda bi, p, pt: (pt[bi, p], 0, 0)),
                  pl.BlockSpec((1, h, LANES), lambda bi, p, pt: (bi, 0, 0))],
        out_specs=[pl.BlockSpec((1, h, LANES), lambda bi, p, pt: (bi, 0, p)),
                   pl.BlockSpec((1, h, LANES), lambda bi, p, pt: (bi, 0, 0))],
        scratch_shapes=[pltpu.VMEM((h, LANES), F32)],
    )
    return pl.pallas_call(
        _cumsum_kernel,
        grid_spec=grid_spec,
        out_shape=(jax.ShapeDtypeStruct((b, h, n_pages * LANES), F32), jax.ShapeDtypeStruct((b, h, LANES), F32)),
        compiler_params=_cparams(2),
        name="cumsum_pages",
    )(page_table, pool_t, new_t)


def _idx_decode_kernel(pt_ref, q_ref, w_ref, kpool_ref, knew_ref, bias_ref, s_ref, *, n_pages, n_new, nsel, page):
    del pt_ref
    p = pl.program_id(1)
    rows = n_new

    def scores(kblk):
        s = lax.dot_general(q_ref[0], kblk.astype(BF16), NT_DIMS, preferred_element_type=F32)
        t = jnp.maximum(s, 0.0) * w_ref[0]
        acc = t[0:rows]
        for h in range(1, N_IDX_HEADS):
            acc = acc + t[h * rows:(h + 1) * rows]
        return acc

    @pl.when(p < n_pages)
    def _past():
        s_ref[p] = _sort_key(scores(kpool_ref[0]))

    @pl.when(p == n_pages)
    def _new():
        sc = scores(knew_ref[0])
        i = lax.broadcasted_iota(I32, (rows, LANES), 0)
        lane = lax.broadcasted_iota(I32, (rows, LANES), 1)
        s_ref[n_pages] = _sort_key(jnp.where(lane <= i, sc, -jnp.inf))

        def idx_of(c, g):
            return lane + c * page

        n_keys = (n_pages + 1) * page
        thr, cut = _select_topk(lambda c: s_ref[c], n_pages + 1, 1, rows, nsel, int(math.log2(n_keys)) + 1, idx_of)

        def write_body(c, carry):
            k = s_ref[c]
            sel = (k > thr) | ((k == thr) & (lane + c * page <= cut))
            bias_ref[0, c] = jnp.where(sel, 0.0, NEG)
            return carry

        lax.fori_loop(0, n_pages + 1, write_body, 0)


def _idx_decode(page_table, q_st, w_st, kpool, knew, nsel):
    b, n_pages = page_table.shape
    page = kpool.shape[1]
    n_new = q_st.shape[1] // N_IDX_HEADS
    grid_spec = pltpu.PrefetchScalarGridSpec(
        num_scalar_prefetch=1,
        grid=(b, n_pages + 1),
        in_specs=[pl.BlockSpec((1,) + q_st.shape[1:], lambda bi, p, pt: (bi, 0, 0)),
                  pl.BlockSpec((1,) + w_st.shape[1:], lambda bi, p, pt: (bi, 0, 0)),
                  pl.BlockSpec((1, page, IDX_DIM), lambda bi, p, pt: (pt[bi, jnp.minimum(p, n_pages - 1)], 0, 0)),
                  pl.BlockSpec((1, page, IDX_DIM), lambda bi, p, pt: (bi, 0, 0))],
        out_specs=pl.BlockSpec((1, n_pages + 1, n_new, LANES), lambda bi, p, pt: (bi, 0, 0, 0)),
        scratch_shapes=[pltpu.VMEM((n_pages + 1, n_new, LANES), I32)],
    )
    return pl.pallas_call(
        functools.partial(_idx_decode_kernel, n_pages=n_pages, n_new=n_new, nsel=nsel, page=page),
        grid_spec=grid_spec,
        out_shape=jax.ShapeDtypeStruct((b, n_pages + 1, n_new, LANES), F32),
        compiler_params=_cparams(2),
        name="idx_decode",
    )(page_table, q_st, w_st, kpool, knew)


def _attn_decode_kernel(*refs, variant, n_units, n_new, n_pages, v_unit):
    it = iter(refs)
    _pt_ref = next(it)
    q_ref, kpool_ref, vpool_ref, knew_ref, vnew_ref = next(it), next(it), next(it), next(it), next(it)
    bias_ref = next(it) if variant == "dsa" else None
    cq_ref = next(it) if variant == "fox" else None
    ck_ref = next(it) if variant == "fox" else None
    cn_ref = next(it) if variant == "fox" else None
    lam_ref = next(it) if variant == "diff" else None
    o_ref = next(it)
    qbd_ref, m_ref, l_ref, acc_ref = next(it), next(it), next(it), next(it)

    p = pl.program_id(1)
    rows = n_units * n_new
    w = q_ref.shape[-1]
    wv = acc_ref.shape[-1]

    @pl.when(p == 0)
    def _init():
        q = q_ref[0].astype(F32)
        qt = jnp.concatenate([q] * n_units, axis=0)
        r = _div_pow2(lax.broadcasted_iota(I32, (rows, w), 0), n_new)
        c = _div_pow2(lax.broadcasted_iota(I32, (rows, w), 1), HEAD_DIM)
        qbd_ref[...] = jnp.where(r == c, qt, 0.0).astype(BF16)
        m_ref[...] = jnp.full(m_ref.shape, NEG, F32)
        l_ref[...] = jnp.zeros(l_ref.shape, F32)
        acc_ref[...] = jnp.zeros(acc_ref.shape, F32)

    def expand_rows(x8):
        return jnp.concatenate([x8] * n_units, axis=0)

    def expand_units(xu):
        return jnp.concatenate([jnp.broadcast_to(xu[u:u + 1, :], (n_new, LANES)) for u in range(n_units)], axis=0)

    def step(k, v, is_new):
        s = lax.dot_general(qbd_ref[...], k.astype(BF16), NT_DIMS, preferred_element_type=F32)
        if variant == "dsa":
            s = s + expand_rows(bias_ref[0, 0])
        if variant == "fox":
            ck = cn_ref[0] if is_new else ck_ref[0]
            s = s + (cq_ref[0] - expand_units(ck))
        if is_new:
            i = expand_rows(lax.broadcasted_iota(I32, (n_new, LANES), 0))
            lane = lax.broadcasted_iota(I32, (rows, LANES), 1)
            s = jnp.where(lane <= i, s, NEG)
        m_prev = m_ref[...]
        m_new = jnp.maximum(m_prev, jnp.max(s, axis=-1, keepdims=True))
        alpha = jnp.exp(m_prev - m_new)
        pr = jnp.exp(s - m_new)
        l_ref[...] = alpha * l_ref[...] + jnp.sum(pr, axis=-1, keepdims=True)
        m_ref[...] = m_new
        pv = jnp.dot(pr.astype(BF16), v.astype(BF16), preferred_element_type=F32)
        acc_ref[...] = jnp.tile(alpha, (1, wv // LANES)) * acc_ref[...] + pv

    @pl.when(p < n_pages)
    def _past():
        step(kpool_ref[0], vpool_ref[0], False)

    @pl.when(p == n_pages)
    def _new():
        step(knew_ref[0], vnew_ref[0], True)
        accn = acc_ref[...] / jnp.tile(l_ref[...], (1, wv // LANES))
        cu = _div_pow2(lax.broadcasted_iota(I32, (n_new, wv), 1), v_unit)
        out = jnp.zeros((n_new, wv), F32)
        if variant == "diff":
            lam = lam_ref[0, 0]
            for h in range(n_units // 2):
                a0 = accn[(2 * h) * n_new:(2 * h + 1) * n_new]
                a1 = accn[(2 * h + 1) * n_new:(2 * h + 2) * n_new]
                out = out + jnp.where(cu == h, a0 - lam * a1, 0.0)
        else:
            for u in range(n_units):
                out = out + jnp.where(cu == u, accn[u * n_new:(u + 1) * n_new], 0.0)
        o_ref[0] = out


def _attn_decode(variant, page_table, q16, kpool, vpool, knew, vnew, bias=None, cq=None, ck=None, cn=None, lam=None):
    b, n_pages = page_table.shape
    page = kpool.shape[1]
    n_new, w = q16.shape[1], q16.shape[2]
    wv = vpool.shape[2]
    n_units = w // HEAD_DIM
    rows = n_units * n_new
    v_unit = wv // (n_units // 2) if variant == "diff" else HEAD_DIM

    def pool_map(bi, p, pt):
        return (pt[bi, jnp.minimum(p, n_pages - 1)], 0, 0)

    def req_map(bi, p, pt):
        return (bi, 0, 0)

    in_specs = [pl.BlockSpec((1, n_new, w), req_map),
                pl.BlockSpec((1, page, w), pool_map), pl.BlockSpec((1, page, wv), pool_map),
                pl.BlockSpec((1, page, w), req_map), pl.BlockSpec((1, page, wv), req_map)]
    args = [q16, kpool, vpool, knew, vnew]
    if variant == "dsa":
        in_specs.append(pl.BlockSpec((1, 1, n_new, LANES), lambda bi, p, pt: (bi, p, 0, 0)))
        args.append(bias)
    if variant == "fox":
        in_specs.append(pl.BlockSpec((1, rows, LANES), req_map))
        in_specs.append(pl.BlockSpec((1, n_units, LANES), lambda bi, p, pt: (bi, 0, jnp.minimum(p, n_pages - 1))))
        in_specs.append(pl.BlockSpec((1, n_units, LANES), req_map))
        args += [cq, ck, cn]
    if variant == "diff":
        in_specs.append(pl.BlockSpec(memory_space=pltpu.SMEM))
        args.append(lam)
    grid_spec = pltpu.PrefetchScalarGridSpec(
        num_scalar_prefetch=1,
        grid=(b, n_pages + 1),
        in_specs=in_specs,
        out_specs=pl.BlockSpec((1, n_new, wv), req_map),
        scratch_shapes=[pltpu.VMEM((rows, w), BF16), pltpu.VMEM((rows, LANES), F32),
                        pltpu.VMEM((rows, LANES), F32), pltpu.VMEM((rows, wv), F32)],
    )
    return pl.pallas_call(
        functools.partial(_attn_decode_kernel, variant=variant, n_units=n_units, n_new=n_new, n_pages=n_pages,
                          v_unit=v_unit),
        grid_spec=grid_spec,
        out_shape=jax.ShapeDtypeStruct((b, n_new, wv), F32),
        compiler_params=_cparams(2),
        name="attn_decode_" + variant,
    )(page_table, *args)


def _diff_lambda(p, lam_init):
    def e(a, c):
        return jnp.exp(jnp.sum(a.astype(F32) * c.astype(F32)))
    return (e(p["lam_q1"], p["lam_k1"]) - e(p["lam_q2"], p["lam_k2"]) + lam_init).reshape(1, 1).astype(F32)


def _pad_rows(x, rows):
    return jnp.pad(x, ((0, 0), (0, rows - x.shape[1]), (0, 0)))


def _tile_for(m, pref):
    t = min(m, pref)
    while m % t:
        t //= 2
    return t


def _even_prompt(x, layer, p, tiles):
    b, seq, d = x.shape
    tm, tq, tk = tiles
    pos = jnp.tile(jnp.arange(seq), b)
    (qa, ka, ka16, va, va16, ga, qi, ki, ki16, wi, qb, kb, kb16, vb, vb16, gb) = _proj0(
        x.reshape(b * seq, d), pos, p, tm)
    lam_init = 0.8 - 0.6 * math.exp(-0.3 * layer)
    lam = _diff_lambda(p, lam_init)
    nsel = min(TOPK_MAX, seq // 4)
    r3 = lambda t: t.reshape(b, seq, t.shape[-1])
    bias = _idx_prompt(r3(qi), r3(wi), r3(ki16), nsel, tq, tk)
    o_a = _flash_prompt("dsa", r3(qa), r3(ka16), r3(va16), tq, tk, bias=bias)
    o_b = _flash_prompt("diff", r3(qb), r3(kb16), r3(vb16), tq, tk, lam=lam)
    y = _out0(x.reshape(b * seq, d), o_a.reshape(b * seq, -1), ga, o_b.reshape(b * seq, -1), gb, p, lam_init, tm)
    new = (ka.reshape(b, seq, N_HEADS_A, HEAD_DIM), va.reshape(b, seq, N_HEADS_A, HEAD_DIM),
           ki.reshape(b, seq, IDX_DIM), kb.reshape(b, seq, N_HEADS_B, 2, HEAD_DIM),
           vb.reshape(b, seq, N_HEADS_B, 2 * HEAD_DIM))
    return y.reshape(b, seq, d), new


def _even_sample(x, layer, caches, page_table, p):
    cache_k_a, cache_v_a, cache_k_i, cache_k_b, cache_v_b = caches
    b, n_new, d = x.shape
    n_pages = page_table.shape[1]
    n_pool, page = cache_k_a.shape[0], cache_k_a.shape[1]
    past = n_pages * page
    pos = jnp.tile(past + jnp.arange(n_new), b)
    m = b * n_new
    (qa, ka, _ka16, va, _va16, ga, qi, ki, _ki16, wi, qb, kb, _kb16, vb, _vb16, gb) = _proj0(
        x.reshape(m, d), pos, p, _tile_for(m, 256))
    lam_init = 0.8 - 0.6 * math.exp(-0.3 * layer)
    lam = _diff_lambda(p, lam_init)
    nsel = min(TOPK_MAX, (past + n_new) // 4)
    r3 = lambda t: t.reshape(b, n_new, t.shape[-1])
    q_st = jnp.swapaxes(qi.reshape(b, n_new, N_IDX_HEADS, IDX_DIM), 1, 2).reshape(b, N_IDX_HEADS * n_new, IDX_DIM)
    w_st = jnp.swapaxes(wi.reshape(b, n_new, N_IDX_HEADS), 1, 2).reshape(b, N_IDX_HEADS * n_new, 1)
    w_st = jnp.broadcast_to(w_st, (b, N_IDX_HEADS * n_new, LANES))
    bias = _idx_decode(page_table, q_st, w_st, cache_k_i, _pad_rows(r3(ki), page), nsel)
    o_a = _attn_decode("dsa", page_table, r3(qa), cache_k_a.reshape(n_pool, page, WIDTH_A),
                       cache_v_a.reshape(n_pool, page, WIDTH_A), _pad_rows(r3(ka), page), _pad_rows(r3(va), page),
                       bias=bias)
    o_b = _attn_decode("diff", page_table, r3(qb), cache_k_b.reshape(n_pool, page, WIDTH_B),
                       cache_v_b.reshape(n_pool, page, WIDTH_B), _pad_rows(r3(kb), page), _pad_rows(r3(vb), page),
                       lam=lam)
    y = _out0(x.reshape(m, d), o_a.reshape(m, -1), ga, o_b.reshape(m, -1), gb, p, lam_init, _tile_for(m, 256))
    new = (ka.reshape(b, n_new, N_HEADS_A, HEAD_DIM), va.reshape(b, n_new, N_HEADS_A, HEAD_DIM),
           ki.reshape(b, n_new, IDX_DIM), kb.reshape(b, n_new, N_HEADS_B, 2, HEAD_DIM),
           vb.reshape(b, n_new, N_HEADS_B, 2 * HEAD_DIM))
    return y.reshape(b, n_new, d), new


def _odd_prompt(x, p, tiles):
    b, seq, d = x.shape
    tm, tq, tk = tiles
    q16, k, k16, v, v16, gate, logf = _proj1(x.reshape(b * seq, d), p, tm)
    r3 = lambda t: t.reshape(b, seq, t.shape[-1])
    n_blk = seq // LANES
    logf_t = jnp.swapaxes(logf.reshape(b * n_blk, LANES, N_HEADS_C), 1, 2)
    ident = jnp.arange(b * n_blk, dtype=I32).reshape(b, n_blk)
    c_t, _ = _cumsum_pages(ident, logf_t, jnp.zeros((b, N_HEADS_C, LANES), F32))
    cq = jnp.swapaxes(c_t, 1, 2)
    o = _flash_prompt("fox", r3(q16), r3(k16), r3(v16), tq, tk, cq=cq, ck=c_t)
    y = _out1(x.reshape(b * seq, d), o.reshape(b * seq, -1), gate, p, tm)
    new = (k.reshape(b, seq, N_HEADS_C, HEAD_DIM), v.reshape(b, seq, N_HEADS_C, HEAD_DIM),
           logf.reshape(b, seq, N_HEADS_C))
    return y.reshape(b, seq, d), new


def _odd_sample(x, caches, page_table, p):
    cache_k, cache_v, cache_logf = caches
    b, n_new, d = x.shape
    n_pool, page = cache_k.shape[0], cache_k.shape[1]
    m = b * n_new
    q16, k, _k16, v, _v16, gate, logf = _proj1(x.reshape(m, d), p, _tile_for(m, 256))
    r3 = lambda t: t.reshape(b, n_new, t.shape[-1])
    logf_pool_t = jnp.swapaxes(cache_logf.astype(F32), 1, 2)
    logf_new_t = jnp.swapaxes(_pad_rows(r3(logf), page), 1, 2)
    c_past_t, c_new_t = _cumsum_pages(page_table, logf_pool_t, logf_new_t)
    cq = jnp.broadcast_to(c_new_t[:, :, :n_new].reshape(b, N_HEADS_C * n_new, 1), (b, N_HEADS_C * n_new, LANES))
    o = _attn_decode("fox", page_table, r3(q16), cache_k.reshape(n_pool, page, WIDTH_C),
                     cache_v.reshape(n_pool, page, WIDTH_C), _pad_rows(r3(k), page), _pad_rows(r3(v), page),
                     cq=cq, ck=c_past_t, cn=c_new_t)
    y = _out1(x.reshape(m, d), o.reshape(m, -1), gate, p, _tile_for(m, 256))
    new = (k.reshape(b, n_new, N_HEADS_C, HEAD_DIM), v.reshape(b, n_new, N_HEADS_C, HEAD_DIM),
           logf.reshape(b, n_new, N_HEADS_C))
    return y.reshape(b, n_new, d), new


def kernel(x_prompt, x_sample, cache_l0_k_a, cache_l0_v_a, cache_l0_k_idx, cache_l0_k_b, cache_l0_v_b,
           cache_l1_k_c, cache_l1_v_c, cache_l1_logf_c, page_table,
           l0_norm, l0_w_in, l0_qn_a, l0_kn_a, l0_kn_idx, l0_qn_b, l0_kn_b,
           l0_lam_q1, l0_lam_k1, l0_lam_q2, l0_lam_k2, l0_subln_b, l0_w_out,
           l1_norm, l1_w_in, l1_b_f, l1_qn, l1_kn, l1_w_out):
    p0 = dict(norm=l0_norm, w_in=l0_w_in, qn_a=l0_qn_a, kn_a=l0_kn_a, kn_idx=l0_kn_idx, qn_b=l0_qn_b, kn_b=l0_kn_b,
              lam_q1=l0_lam_q1, lam_k1=l0_lam_k1, lam_q2=l0_lam_q2, lam_k2=l0_lam_k2, subln_b=l0_subln_b,
              w_out=l0_w_out)
    p1 = dict(norm=l1_norm, w_in=l1_w_in, b_f=l1_b_f, qn=l1_qn, kn=l1_kn, w_out=l1_w_out)
    b, seq, _ = x_prompt.shape
    tiles = (_tile_for(b * seq, ROW_TILE), _tile_for(seq, Q_TILE), _tile_for(seq, K_TILE))
    page_table = page_table.astype(I32)
    xp, sp0 = _even_prompt(x_prompt, 0, p0, tiles)
    xs, ss0 = _even_sample(x_sample, 0, (cache_l0_k_a, cache_l0_v_a, cache_l0_k_idx, cache_l0_k_b, cache_l0_v_b),
                           page_table, p0)
    xp, sp1 = _odd_prompt(xp, p1, tiles)
    xs, ss1 = _odd_sample(xs, (cache_l1_k_c, cache_l1_v_c, cache_l1_logf_c), page_table, p1)
    (p_k_a, p_v_a, p_k_idx, p_k_b, p_v_b), (p_k_c, p_v_c, p_logf_c) = sp0, sp1
    (s_k_a, s_v_a, s_k_idx, s_k_b, s_v_b), (s_k_c, s_v_c, s_logf_c) = ss0, ss1
    return (xp, xs, p_k_a, s_k_a, p_v_a, s_v_a, p_k_idx, s_k_idx, p_k_b, s_k_b, p_v_b, s_v_b,
            p_k_c, s_k_c, p_v_c, s_v_c, p_logf_c, s_logf_c)
```

```python
import functools
import math

import jax
import jax.numpy as jnp
from jax import lax
from jax.experimental import pallas as pl
from jax.experimental.pallas import tpu as pltpu

F32 = jnp.float32
BF16 = jnp.bfloat16
I32 = jnp.int32

HEAD_DIM = 64
ROPE_THETA = 500000.0
N_HEADS_A = 8
N_IDX_HEADS = 8
IDX_DIM = 64
TOPK_MAX = 256
N_HEADS_B = 4
N_HEADS_C = 16
EPS = 1e-6
WIDTH_A = N_HEADS_A * HEAD_DIM
WIDTH_B = N_HEADS_B * 2 * HEAD_DIM
WIDTH_C = N_HEADS_C * HEAD_DIM
WIDTH_I = N_IDX_HEADS * IDX_DIM
QK_SCALE = HEAD_DIM ** -0.5

LANES = 128
SUBLANES = 8
MXU_DIM = 256
VMEM_LIMIT = 56 * 1024 * 1024
ROW_TILE = 256
Q_TILE = 256
K_TILE = 512
COUNT_ROWS = 64

NEG = -1e30
HIGHEST = lax.Precision.HIGHEST
NT_DIMS = (((1,), (1,)), ((), ()))

KEY_NEG_INF = -2139095041 - 0


def _cparams(n_axes):
    return pltpu.CompilerParams(
        dimension_semantics=("arbitrary",) * n_axes, vmem_limit_bytes=VMEM_LIMIT)


def _div_pow2(x, d):
    assert d & (d - 1) == 0
    return lax.shift_right_logical(x, jnp.int32(d.bit_length() - 1))


def _sort_key(x):
    bits = pltpu.bitcast(x, I32)
    return bits ^ ((bits >> 31) & jnp.int32(0x7FFFFFFF))


def _rms_rows(x, g):
    ms = jnp.mean(x * x, axis=-1, keepdims=True)
    return x * lax.rsqrt(ms + EPS) * g


def _head_norm(h, bd, gain):
    w = h.shape[-1]
    hs = h * h
    if w >= MXU_DIM:
        cols = [jnp.dot(hs[:, c * MXU_DIM:(c + 1) * MXU_DIM], bd, precision=HIGHEST,
                        preferred_element_type=F32) for c in range(w // MXU_DIM)]
        ms = cols[0] if len(cols) == 1 else jnp.concatenate(cols, axis=-1)
    else:
        ms = jnp.dot(hs, bd[:w, :w], precision=HIGHEST, preferred_element_type=F32)
    return h * lax.rsqrt(ms + EPS) * gain


def _rope(y, rope_ref):
    c = rope_ref[:, 0:LANES]
    s_lo = rope_ref[:, LANES:2 * LANES]
    s_hi = rope_ref[:, 2 * LANES:3 * LANES]
    outs = []
    for j in range(y.shape[-1] // LANES):
        yc = y[:, j * LANES:(j + 1) * LANES]
        outs.append(yc * c + pltpu.roll(yc, LANES - 8, 1) * s_lo + pltpu.roll(yc, 8, 1) * s_hi)
    return outs[0] if len(outs) == 1 else jnp.concatenate(outs, axis=-1)


def _proj0_kernel(x_ref, g_ref, w_ref, ws_ref, rope_ref, gains_ref, gki_ref, bd_ref,
                  qa_ref, ka_ref, ka16_ref, va_ref, va16_ref, ga_ref,
                  qi_ref, ki_ref, ki16_ref, wi_ref,
                  qb_ref, kb_ref, kb16_ref, vb_ref, vb16_ref, gb_ref):
    xb = _rms_rows(x_ref[...], g_ref[...]).astype(BF16)
    bd = bd_ref[...]
    w512 = WIDTH_A

    def piece(j):
        return jnp.dot(xb, w_ref[:, j * w512:(j + 1) * w512], preferred_element_type=F32)

    q_a = _rope(_head_norm(piece(0), bd, gains_ref[0:1, :]), rope_ref)
    qa_ref[...] = (q_a * QK_SCALE).astype(BF16)
    k_a = _rope(_head_norm(piece(1), bd, gains_ref[1:2, :]), rope_ref)
    ka_ref[...] = k_a
    ka16_ref[...] = k_a.astype(BF16)
    v_a = piece(2)
    va_ref[...] = v_a
    va16_ref[...] = v_a.astype(BF16)
    ga_ref[...] = piece(3)
    qi_ref[...] = _rope(piece(4), rope_ref).astype(BF16)
    q_b = _rope(_head_norm(piece(5), bd, gains_ref[2:3, :]), rope_ref)
    qb_ref[...] = (q_b * QK_SCALE).astype(BF16)
    k_b = _rope(_head_norm(piece(6), bd, gains_ref[3:4, :]), rope_ref)
    kb_ref[...] = k_b
    kb16_ref[...] = k_b.astype(BF16)
    v_b = piece(7)
    vb_ref[...] = v_b
    vb16_ref[...] = v_b.astype(BF16)
    gb_ref[...] = piece(8)
    hs = jnp.dot(xb, ws_ref[...], preferred_element_type=F32)
    k_i = _rope(_head_norm(hs[:, 0:LANES], bd, gki_ref[...]), rope_ref)
    ki_ref[...] = k_i[:, 0:IDX_DIM]
    ki16_ref[...] = k_i.astype(BF16)
    wi_ref[...] = hs[:, LANES:LANES + N_IDX_HEADS] * (WIDTH_I ** -0.5)


def _proj1_kernel(x_ref, g_ref, w_ref, wf_ref, bf_ref, gains_ref, bd_ref,
                  q_ref, k_ref, k16_ref, v_ref, v16_ref, gate_ref, logf_ref):
    xb = _rms_rows(x_ref[...], g_ref[...]).astype(BF16)
    bd = bd_ref[...]
    wc = WIDTH_C

    def piece(j):
        return jnp.dot(xb, w_ref[:, j * wc:(j + 1) * wc], preferred_element_type=F32)

    q_ref[...] = (_head_norm(piece(0), bd, gains_ref[0:1, :]) * QK_SCALE).astype(BF16)
    k = _head_norm(piece(1), bd, gains_ref[1:2, :])
    k_ref[...] = k
    k16_ref[...] = k.astype(BF16)
    v = piece(2)
    v_ref[...] = v
    v16_ref[...] = v.astype(BF16)
    gate_ref[...] = piece(3)
    f = jnp.dot(xb, wf_ref[...], preferred_element_type=F32)[:, 0:N_HEADS_C] + bf_ref[...]
    logf_ref[...] = jnp.minimum(f, 0.0) - jnp.log1p(jnp.exp(-jnp.abs(f)))


def _row_spec(tm, w):
    return pl.BlockSpec((tm, w), lambda i: (i, 0))


def _const_spec(shape):
    return pl.BlockSpec(shape, lambda i: (0,) * len(shape))


def _block_diag_mean():
    r = lax.broadcasted_iota(I32, (MXU_DIM, MXU_DIM), 0) // HEAD_DIM
    c = lax.broadcasted_iota(I32, (MXU_DIM, MXU_DIM), 1) // HEAD_DIM
    return jnp.where(r == c, 1.0 / HEAD_DIM, 0.0).astype(F32)


def _rope_table(pos):
    rot = HEAD_DIM // 4
    half = rot // 2
    inv = jnp.power(F32(ROPE_THETA), -jnp.arange(half, dtype=F32) * (2.0 / rot))
    ang = pos.astype(F32)[:, None] * inv[None, :]
    cos, sin = jnp.cos(ang), jnp.sin(ang)
    m = pos.shape[0]
    ones = jnp.ones((m, HEAD_DIM - rot), F32)
    zeros = jnp.zeros((m, HEAD_DIM - rot), F32)
    zh = jnp.zeros((m, half), F32)
    c = jnp.concatenate([cos, cos, ones], axis=-1)
    s_lo = jnp.concatenate([-sin, zh, zeros], axis=-1)
    s_hi = jnp.concatenate([zh, sin, zeros], axis=-1)
    return jnp.concatenate([c, c, s_lo, s_lo, s_hi, s_hi], axis=-1)


def _tile_gain(g, w):
    return jnp.tile(g.astype(F32), w // g.shape[0])[None, :]


def _proj0(x2d, pos, p, tm):
    m, d = x2d.shape
    sizes = (WIDTH_A,) * 4 + (WIDTH_I, IDX_DIM, N_IDX_HEADS) + (WIDTH_B,) * 4
    offs = [0]
    for s in sizes:
        offs.append(offs[-1] + s)
    w_in = p["w_in"]
    cols = [w_in[:, offs[i]:offs[i + 1]] for i in range(len(sizes))]
    w_big = jnp.concatenate([cols[0], cols[1], cols[2], cols[3], cols[4], cols[7], cols[8], cols[9], cols[10]],
                            axis=1).astype(BF16)
    w_small = jnp.concatenate(
        [cols[5], cols[5], cols[6], jnp.zeros((d, LANES - N_IDX_HEADS), w_in.dtype)], axis=1).astype(BF16)
    gains = jnp.concatenate([_tile_gain(p["qn_a"], WIDTH_A), _tile_gain(p["kn_a"], WIDTH_A),
                             _tile_gain(p["qn_b"], WIDTH_B), _tile_gain(p["kn_b"], WIDTH_B)], axis=0)
    gki = _tile_gain(p["kn_idx"], LANES)
    rope = _rope_table(pos)
    w5 = WIDTH_A
    f32o = lambda w: jax.ShapeDtypeStruct((m, w), F32)
    b16o = lambda w: jax.ShapeDtypeStruct((m, w), BF16)
    out_shape = (b16o(w5), f32o(w5), b16o(w5), f32o(w5), b16o(w5), f32o(w5),
                 b16o(w5), f32o(IDX_DIM), b16o(LANES), f32o(N_IDX_HEADS),
                 b16o(w5), f32o(w5), b16o(w5), f32o(w5), b16o(w5), f32o(w5))
    out_specs = tuple(_row_spec(tm, s.shape[1]) for s in out_shape)
    return pl.pallas_call(
        _proj0_kernel,
        grid=(m // tm,),
        in_specs=[_row_spec(tm, d), _const_spec((1, d)), _const_spec(w_big.shape), _const_spec(w_small.shape),
                  _row_spec(tm, 3 * LANES), _const_spec(gains.shape), _const_spec(gki.shape),
                  _const_spec((MXU_DIM, MXU_DIM))],
        out_specs=out_specs,
        out_shape=out_shape,
        compiler_params=_cparams(1),
        name="proj0",
    )(x2d, p["norm"].astype(F32)[None, :], w_big, w_small, rope, gains, gki, _block_diag_mean())


def _proj1(x2d, p, tm):
    m, d = x2d.shape
    wc = WIDTH_C
    w_in = p["w_in"]
    w_big = w_in[:, :4 * wc].astype(BF16)
    w_f = jnp.concatenate([w_in[:, 4 * wc:], jnp.zeros((d, LANES - N_HEADS_C), w_in.dtype)], axis=1).astype(BF16)
    gains = jnp.concatenate([_tile_gain(p["qn"], wc), _tile_gain(p["kn"], wc)], axis=0)
    f32o = lambda w: jax.ShapeDtypeStruct((m, w), F32)
    b16o = lambda w: jax.ShapeDtypeStruct((m, w), BF16)
    out_shape = (b16o(wc), f32o(wc), b16o(wc), f32o(wc), b16o(wc), f32o(wc), f32o(N_HEADS_C))
    out_specs = tuple(_row_spec(tm, s.shape[1]) for s in out_shape)
    return pl.pallas_call(
        _proj1_kernel,
        grid=(m // tm,),
        in_specs=[_row_spec(tm, d), _const_spec((1, d)), _const_spec(w_big.shape), _const_spec(w_f.shape),
                  _const_spec((1, N_HEADS_C)), _const_spec(gains.shape), _const_spec((MXU_DIM, MXU_DIM))],
        out_specs=out_specs,
        out_shape=out_shape,
        compiler_params=_cparams(1),
        name="proj1",
    )(x2d, p["norm"].astype(F32)[None, :], w_big, w_f, p["b_f"].astype(F32)[None, :], gains, _block_diag_mean())


def _silu(g):
    return g * jax.nn.sigmoid(g)


def _out0_kernel(x_ref, oa_ref, ga_ref, ob_ref, gb_ref, sub_ref, w_ref, y_ref, *, post_scale):
    ya = (oa_ref[...] * _silu(ga_ref[...])).astype(BF16)
    ob = ob_ref[...]
    cols = []
    for h in range(N_HEADS_B):
        oc = ob[:, h * LANES:(h + 1) * LANES]
        ms = jnp.mean(oc * oc, axis=-1, keepdims=True)
        cols.append(oc * lax.rsqrt(ms + EPS))
    obn = jnp.concatenate(cols, axis=-1) * sub_ref[...] * post_scale
    yb = (obn * _silu(gb_ref[...])).astype(BF16)
    y = jnp.dot(ya, w_ref[0:WIDTH_A, :], preferred_element_type=F32)
    y = y + jnp.dot(yb, w_ref[WIDTH_A:WIDTH_A + WIDTH_B, :], preferred_element_type=F32)
    y_ref[...] = x_ref[...] + y


def _out1_kernel(x_ref, o_ref, g_ref, w_ref, y_ref):
    yo = (o_ref[...] * _silu(g_ref[...])).astype(BF16)
    y_ref[...] = x_ref[...] + jnp.dot(yo, w_ref[...], preferred_element_type=F32)


def _out0(x2d, o_a, g_a, o_b, g_b, p, lam_init, tm):
    m, d = x2d.shape
    sub = _tile_gain(p["subln_b"], WIDTH_B)
    w = p["w_out"].astype(BF16)
    return pl.pallas_call(
        functools.partial(_out0_kernel, post_scale=1.0 - lam_init),
        grid=(m // tm,),
        in_specs=[_row_spec(tm, d), _row_spec(tm, WIDTH_A), _row_spec(tm, WIDTH_A), _row_spec(tm, WIDTH_B),
                  _row_spec(tm, WIDTH_B), _const_spec(sub.shape), _const_spec(w.shape)],
        out_specs=_row_spec(tm, d),
        out_shape=jax.ShapeDtypeStruct((m, d), F32),
        compiler_params=_cparams(1),
        name="out0",
    )(x2d, o_a, g_a, o_b, g_b, sub, w)


def _out1(x2d, o, g, p, tm):
    m, d = x2d.shape
    w = p["w_out"].astype(BF16)
    return pl.pallas_call(
        _out1_kernel,
        grid=(m // tm,),
        in_specs=[_row_spec(tm, d), _row_spec(tm, WIDTH_C), _row_spec(tm, WIDTH_C), _const_spec(w.shape)],
        out_specs=_row_spec(tm, d),
        out_shape=jax.ShapeDtypeStruct((m, d), F32),
        compiler_params=_cparams(1),
        name="out1",
    )(x2d, o, g, w)


def _select_topk(load_keys, n_chunks, groups, rows, nsel, idx_bits, idx_of):
    shape = (rows, LANES)
    rb = min(rows, COUNT_ROWS)
    n_blk = rows // rb

    def count(pred_of):
        accs = []
        for b in range(n_blk):
            pred = pred_of(slice(b * rb, (b + 1) * rb))

            def body(c, acc, b=b, pred=pred):
                blk = load_keys(c, b * rb, rb)
                for g in range(groups):
                    acc = jnp.where(pred(blk[:, g * LANES:(g + 1) * LANES], c, g), acc + 1.0, acc)
                return acc
            accs.append(lax.fori_loop(0, n_chunks, body, jnp.zeros((rb, LANES), F32)))
        outs = [jnp.broadcast_to(jnp.sum(a, axis=-1, keepdims=True), (rb, LANES)) for a in accs]
        return outs[0] if n_blk == 1 else jnp.concatenate(outs, axis=0)

    kf = float(nsel)

    def bit_step(i, thr):
        bit = lax.shift_left(jnp.int32(1), jnp.int32(31) - i)
        cand = thr ^ bit

        def pred_of(sl):
            cb = cand[sl]
            return lambda k, c, g: k >= cb
        return jnp.where(count(pred_of) >= kf, cand, thr)

    thr = lax.fori_loop(0, 32, bit_step, jnp.full(shape, jnp.iinfo(jnp.int32).min, I32))
    thr = jnp.maximum(thr, KEY_NEG_INF + 1)

    def gt_of(sl):
        tb = thr[sl]
        return lambda k, c, g: k > tb

    def ge_of(sl):
        tb = thr[sl]
        return lambda k, c, g: k >= tb

    n_gt = count(gt_of)
    n_ge = count(ge_of)
    need = kf - n_gt
    excess = n_ge - kf

    def tie_cut():
        def idx_step(i, cut):
            bit = lax.shift_left(jnp.int32(1), jnp.int32(idx_bits - 1) - i)
            cand = cut | bit

            def pred_of(sl):
                tb, cb = thr[sl], cand[sl]
                return lambda k, c, g: (k == tb) & (idx_of(c, g, rb) < cb)
            return jnp.where(count(pred_of) < need, cand, cut)
        return lax.fori_loop(0, idx_bits, idx_step, jnp.zeros(shape, I32))

    big = jnp.full(shape, jnp.iinfo(jnp.int32).max, I32)
    any_excess = jnp.max(excess) > 0.0
    cut = lax.cond(any_excess, lambda: jnp.where(excess > 0.0, tie_cut(), big), lambda: big)
    return thr, cut


def _pair_masks(rows):
    lane = lax.broadcasted_iota(I32, (rows, LANES), 1)
    return lane < HEAD_DIM


def _idx_prompt_kernel(qi_ref, wi_ref, ki_ref, bias_ref, qm_ref, wb_ref, s_ref, *, tq, tk, nsel, seq):
    qi = pl.program_id(1)
    q0 = qi * tq
    n_chunks = seq // tk
    nc = (q0 + tq + tk - 1) // tk
    lo = _pair_masks(tq)
    groups = tk // LANES
    for j in range(N_IDX_HEADS // 2):
        pair = qi_ref[0, :, j * LANES:(j + 1) * LANES]
        zero = jnp.zeros_like(pair)
        qm_ref[2 * j] = jnp.where(lo, pair, zero)
        qm_ref[2 * j + 1] = jnp.where(lo, zero, pair)
    w = wi_ref[0]
    for h in range(N_IDX_HEADS):
        wb_ref[h] = jnp.broadcast_to(w[:, h:h + 1], (tq, LANES))
    row = q0 + lax.broadcasted_iota(I32, (tq, tk), 0)
    col0 = lax.broadcasted_iota(I32, (tq, tk), 1)

    def score_body(c, carry):
        start = pl.multiple_of(c * tk, tk)
        kblk = ki_ref[0, pl.ds(start, tk), :]
        acc = jnp.zeros((tq, tk), F32)
        for h in range(N_IDX_HEADS):
            s = lax.dot_general(qm_ref[h], kblk, NT_DIMS, preferred_element_type=F32)
            acc = acc + jnp.maximum(s, 0.0) * jnp.tile(wb_ref[h], (1, groups))
        acc = jnp.where(col0 + c * tk <= row, acc, -jnp.inf)
        s_ref[c] = _sort_key(acc)
        return carry

    lax.fori_loop(0, nc, score_body, 0)

    def idx_of(c, g, nr):
        return lax.broadcasted_iota(I32, (nr, LANES), 1) + (c * tk + g * LANES)

    thr, cut = _select_topk(lambda c, r0, nr: s_ref[c, r0:r0 + nr, :], nc, groups, tq, nsel,
                            int(math.log2(seq)) + 1, idx_of)
    thr_t = jnp.tile(thr, (1, groups))
    cut_t = jnp.tile(cut, (1, groups))

    def write_body(c, carry):
        k = s_ref[c]
        sel = (k > thr_t) | ((k == thr_t) & (col0 + c * tk <= cut_t))
        bias_ref[0, c] = jnp.where(sel, 0.0, NEG).astype(BF16)
        return carry

    lax.fori_loop(0, nc, write_body, 0)

    def fill_body(c, carry):
        bias_ref[0, c] = jnp.full((tq, tk), NEG, BF16)
        return carry

    lax.fori_loop(nc, n_chunks, fill_body, 0)


def _idx_prompt(qi16, wi, ki16, nsel, tq, tk):
    b, seq, _ = qi16.shape
    n_chunks = seq // tk
    return pl.pallas_call(
        functools.partial(_idx_prompt_kernel, tq=tq, tk=tk, nsel=nsel, seq=seq),
        grid=(b, seq // tq),
        in_specs=[pl.BlockSpec((1, tq, WIDTH_I), lambda bi, qi: (bi, qi, 0)),
                  pl.BlockSpec((1, tq, N_IDX_HEADS), lambda bi, qi: (bi, qi, 0)),
                  pl.BlockSpec((1, seq, LANES), lambda bi, qi: (bi, 0, 0))],
        out_specs=pl.BlockSpec((1, n_chunks, tq, tk), lambda bi, qi: (bi, 0, qi, 0)),
        out_shape=jax.ShapeDtypeStruct((b, n_chunks, seq, tk), BF16),
        scratch_shapes=[pltpu.VMEM((N_IDX_HEADS, tq, LANES), BF16),
                        pltpu.VMEM((N_IDX_HEADS, tq, LANES), F32),
                        pltpu.VMEM((n_chunks, tq, tk), I32)],
        compiler_params=_cparams(2),
        name="idx_prompt",
    )(qi16, wi, ki16)


def _flash_prompt_kernel(*refs, variant, n_pairs, tq, tk):
    it = iter(refs)
    q_ref, k_ref, v_ref = next(it), next(it), next(it)
    bias_ref = next(it) if variant == "dsa" else None
    cq_ref = next(it) if variant == "fox" else None
    ck_ref = next(it) if variant == "fox" else None
    lam_ref = next(it) if variant == "diff" else None
    o_ref = next(it)
    qm_ref, m_ref, l_ref, acc_ref = next(it), next(it), next(it), next(it)
    cqb_ref = next(it) if variant == "fox" else None

    qi = pl.program_id(1)
    kc = pl.program_id(2)
    nk = pl.num_programs(2)
    last = ((qi + 1) * tq - 1) // tk
    n_units = 2 * n_pairs
    groups = tk // LANES
    lo = _pair_masks(tq)

    @pl.when(kc == 0)
    def _init():
        for j in range(n_pairs):
            pair = q_ref[0, :, j * LANES:(j + 1) * LANES]
            zero = jnp.zeros_like(pair)
            qm_ref[2 * j] = jnp.where(lo, pair, zero)
            qm_ref[2 * j + 1] = jnp.where(lo, zero, pair)
        m_ref[...] = jnp.full(m_ref.shape, NEG, F32)
        l_ref[...] = jnp.zeros(l_ref.shape, F32)
        acc_ref[...] = jnp.zeros(acc_ref.shape, F32)
        if variant == "fox":
            cq = cq_ref[0]
            for u in range(n_units):
                cqb_ref[u] = jnp.broadcast_to(cq[:, u:u + 1], (tq, LANES))

    def compute(masked):
        if variant == "dsa":
            bias = bias_ref[0, 0].astype(F32)
            masked = False
        if masked:
            row = qi * tq + lax.broadcasted_iota(I32, (tq, tk), 0)
            col = kc * tk + lax.broadcasted_iota(I32, (tq, tk), 1)
            causal = col <= row
        for j in range(n_pairs):
            kp = k_ref[0, :, j * LANES:(j + 1) * LANES]
            vp = v_ref[0, :, j * LANES:(j + 1) * LANES]
            alphas, pvs = [], []
            for e in range(2):
                u = 2 * j + e
                s = lax.dot_general(qm_ref[u], kp, NT_DIMS, preferred_element_type=F32)
                if variant == "dsa":
                    s = s + bias
                if variant == "fox":
                    s = s + (jnp.tile(cqb_ref[u], (1, groups)) - ck_ref[0, u:u + 1, :])
                if masked:
                    s = jnp.where(causal, s, NEG)
                m_prev = m_ref[u]
                m_new = jnp.maximum(m_prev, jnp.max(s, axis=-1, keepdims=True))
                alpha = jnp.exp(m_prev - m_new)
                p = jnp.exp(s - jnp.tile(m_new, (1, groups)))
                l_ref[u] = alpha * l_ref[u] + jnp.sum(p, axis=-1, keepdims=True)
                m_ref[u] = m_new
                pv = jnp.dot(p.astype(BF16), vp, preferred_element_type=F32)
                if variant == "diff":
                    acc_ref[u] = alpha * acc_ref[u] + pv
                else:
                    alphas.append(alpha)
                    pvs.append(pv)
            if variant != "diff":
                acc_ref[j] = jnp.where(lo, alphas[0], alphas[1]) * acc_ref[j] + jnp.where(lo, pvs[0], pvs[1])

    needs_mask = (kc + 1) * tk - 1 > qi * tq

    @pl.when((kc <= last) & needs_mask)
    def _diag():
        compute(True)

    @pl.when((kc <= last) & jnp.logical_not(needs_mask))
    def _full():
        compute(False)

    @pl.when(kc == nk - 1)
    def _fin():
        for j in range(n_pairs):
            if variant == "diff":
                lam = lam_ref[0, 0]
                o_ref[0, :, j * LANES:(j + 1) * LANES] = (
                    acc_ref[2 * j] / l_ref[2 * j] - lam * (acc_ref[2 * j + 1] / l_ref[2 * j + 1]))
            else:
                o_ref[0, :, j * LANES:(j + 1) * LANES] = acc_ref[j] / jnp.where(lo, l_ref[2 * j], l_ref[2 * j + 1])


def _flash_prompt(variant, q16, k16, v16, tq, tk, bias=None, cq=None, ck=None, lam=None):
    b, seq, w = q16.shape
    n_pairs = w // LANES
    n_units = 2 * n_pairs

    def last_of(qi):
        return ((qi + 1) * tq - 1) // tk

    q_spec = pl.BlockSpec((1, tq, w), lambda bi, qi, kc: (bi, qi, 0))
    kv_spec = pl.BlockSpec((1, tk, w), lambda bi, qi, kc: (bi, jnp.minimum(kc, last_of(qi)), 0))
    in_specs = [q_spec, kv_spec, kv_spec]
    args = [q16, k16, v16]
    scratch = [pltpu.VMEM((n_units, tq, LANES), BF16), pltpu.VMEM((n_units, tq, LANES), F32),
               pltpu.VMEM((n_units, tq, LANES), F32),
               pltpu.VMEM((n_units if variant == "diff" else n_pairs, tq, LANES), F32)]
    if variant == "dsa":
        in_specs.append(pl.BlockSpec((1, 1, tq, tk), lambda bi, qi, kc: (bi, jnp.minimum(kc, last_of(qi)), qi, 0)))
        args.append(bias)
    if variant == "fox":
        in_specs.append(pl.BlockSpec((1, tq, n_units), lambda bi, qi, kc: (bi, qi, 0)))
        in_specs.append(pl.BlockSpec((1, n_units, tk), lambda bi, qi, kc: (bi, 0, jnp.minimum(kc, last_of(qi)))))
        args += [cq, ck]
        scratch.append(pltpu.VMEM((n_units, tq, LANES), F32))
    if variant == "diff":
        in_specs.append(pl.BlockSpec(memory_space=pltpu.SMEM))
        args.append(lam)
    return pl.pallas_call(
        functools.partial(_flash_prompt_kernel, variant=variant, n_pairs=n_pairs, tq=tq, tk=tk),
        grid=(b, seq // tq, seq // tk),
        in_specs=in_specs,
        out_specs=pl.BlockSpec((1, tq, w), lambda bi, qi, kc: (bi, qi, 0)),
        out_shape=jax.ShapeDtypeStruct((b, seq, w), F32),
        scratch_shapes=scratch,
        compiler_params=_cparams(3),
        name="flash_" + variant,
    )(*args)


def _cumsum_kernel(pt_ref, x_ref, xn_ref, c_ref, cn_ref, carry_ref):
    del pt_ref
    p = pl.program_id(1)
    n_pages = pl.num_programs(1)
    r = lax.broadcasted_iota(I32, (LANES, LANES), 0)
    c = lax.broadcasted_iota(I32, (LANES, LANES), 1)
    upper = jnp.where(r <= c, 1.0, 0.0).astype(F32)

    @pl.when(p == 0)
    def _():
        carry_ref[...] = jnp.zeros(carry_ref.shape, F32)

    cs = jnp.dot(x_ref[0], upper, precision=HIGHEST, preferred_element_type=F32) + carry_ref[...]
    c_ref[0] = cs
    total = jnp.broadcast_to(cs[:, LANES - 1:LANES], cs.shape)
    carry_ref[...] = total

    @pl.when(p == n_pages - 1)
    def _():
        cn_ref[0] = jnp.dot(xn_ref[0], upper, precision=HIGHEST, preferred_element_type=F32) + total


def _cumsum_pages(page_table, pool_t, new_t):
    b, n_pages = page_table.shape
    h = pool_t.shape[1]
    grid_spec = pltpu.PrefetchScalarGridSpec(
        num_scalar_prefetch=1,
        grid=(b, n_pages),
        in_specs=[pl.BlockSpec((1, h, LANES), lambda bi, p, pt: (pt[bi, p], 0, 0)),
                  pl.BlockSpec((1, h, LANES), lambda bi, p, pt: (bi, 0, 0))],
        out_specs=[pl.BlockSpec((1, h, LANES), lambda bi, p, pt: (bi, 0, p)),
                   pl.BlockSpec((1, h, LANES), lambda bi, p, pt: (bi, 0, 0))],
        scratch_shapes=[pltpu.VMEM((h, LANES), F32)],
    )
    return pl.pallas_call(
        _cumsum_kernel,
        grid_spec=grid_spec,
        out_shape=(jax.ShapeDtypeStruct((b, h, n_pages * LANES), F32), jax.ShapeDtypeStruct((b, h, LANES), F32)),
        compiler_params=_cparams(2),
        name="cumsum_pages",
    )(page_table, pool_t, new_t)


def _idx_decode_kernel(pt_ref, q_ref, w_ref, kpool_ref, knew_ref, bias_ref, s_ref, *, n_pages, n_new, nsel, page):
    del pt_ref
    p = pl.program_id(1)
    rows = n_new

    def scores(kblk):
        s = lax.dot_general(q_ref[0], kblk.astype(BF16), NT_DIMS, preferred_element_type=F32)
        t = jnp.maximum(s, 0.0) * w_ref[0]
        acc = t[0:rows]
        for h in range(1, N_IDX_HEADS):
            acc = acc + t[h * rows:(h + 1) * rows]
        return acc

    @pl.when(p < n_pages)
    def _past():
        s_ref[p] = _sort_key(scores(kpool_ref[0]))

    @pl.when(p == n_pages)
    def _new():
        sc = scores(knew_ref[0])
        i = lax.broadcasted_iota(I32, (rows, LANES), 0)
        lane = lax.broadcasted_iota(I32, (rows, LANES), 1)
        s_ref[n_pages] = _sort_key(jnp.where(lane <= i, sc, -jnp.inf))

        def idx_of(c, g):
            return lane + c * page

        n_keys = (n_pages + 1) * page
        thr, cut = _select_topk(lambda c: s_ref[c], n_pages + 1, 1, rows, nsel, int(math.log2(n_keys)) + 1, idx_of)

        def write_body(c, carry):
            k = s_ref[c]
            sel = (k > thr) | ((k == thr) & (lane + c * page <= cut))
            bias_ref[0, c] = jnp.where(sel, 0.0, NEG)
            return carry

        lax.fori_loop(0, n_pages + 1, write_body, 0)


def _idx_decode(page_table, q_st, w_st, kpool, knew, nsel):
    b, n_pages = page_table.shape
    page = kpool.shape[1]
    n_new = q_st.shape[1] // N_IDX_HEADS
    grid_spec = pltpu.PrefetchScalarGridSpec(
        num_scalar_prefetch=1,
        grid=(b, n_pages + 1),
        in_specs=[pl.BlockSpec((1,) + q_st.shape[1:], lambda bi, p, pt: (bi, 0, 0)),
                  pl.BlockSpec((1,) + w_st.shape[1:], lambda bi, p, pt: (bi, 0, 0)),
                  pl.BlockSpec((1, page, IDX_DIM), lambda bi, p, pt: (pt[bi, jnp.minimum(p, n_pages - 1)], 0, 0)),
                  pl.BlockSpec((1, page, IDX_DIM), lambda bi, p, pt: (bi, 0, 0))],
        out_specs=pl.BlockSpec((1, n_pages + 1, n_new, LANES), lambda bi, p, pt: (bi, 0, 0, 0)),
        scratch_shapes=[pltpu.VMEM((n_pages + 1, n_new, LANES), I32)],
    )
    return pl.pallas_call(
        functools.partial(_idx_decode_kernel, n_pages=n_pages, n_new=n_new, nsel=nsel, page=page),
        grid_spec=grid_spec,
        out_shape=jax.ShapeDtypeStruct((b, n_pages + 1, n_new, LANES), F32),
        compiler_params=_cparams(2),
        name="idx_decode",
    )(page_table, q_st, w_st, kpool, knew)


def _attn_decode_kernel(*refs, variant, n_units, n_new, n_pages, v_unit):
    it = iter(refs)
    _pt_ref = next(it)
    q_ref, kpool_ref, vpool_ref, knew_ref, vnew_ref = next(it), next(it), next(it), next(it), next(it)
    bias_ref = next(it) if variant == "dsa" else None
    cq_ref = next(it) if variant == "fox" else None
    ck_ref = next(it) if variant == "fox" else None
    cn_ref = next(it) if variant == "fox" else None
    lam_ref = next(it) if variant == "diff" else None
    o_ref = next(it)
    qbd_ref, m_ref, l_ref, acc_ref = next(it), next(it), next(it), next(it)

    p = pl.program_id(1)
    rows = n_units * n_new
    w = q_ref.shape[-1]
    wv = acc_ref.shape[-1]

    @pl.when(p == 0)
    def _init():
        q = q_ref[0].astype(F32)
        qt = jnp.concatenate([q] * n_units, axis=0)
        r = _div_pow2(lax.broadcasted_iota(I32, (rows, w), 0), n_new)
        c = _div_pow2(lax.broadcasted_iota(I32, (rows, w), 1), HEAD_DIM)
        qbd_ref[...] = jnp.where(r == c, qt, 0.0).astype(BF16)
        m_ref[...] = jnp.full(m_ref.shape, NEG, F32)
        l_ref[...] = jnp.zeros(l_ref.shape, F32)
        acc_ref[...] = jnp.zeros(acc_ref.shape, F32)

    def expand_rows(x8):
        return jnp.concatenate([x8] * n_units, axis=0)

    def expand_units(xu):
        return jnp.concatenate([jnp.broadcast_to(xu[u:u + 1, :], (n_new, LANES)) for u in range(n_units)], axis=0)

    def step(k, v, is_new):
        s = lax.dot_general(qbd_ref[...], k.astype(BF16), NT_DIMS, preferred_element_type=F32)
        if variant == "dsa":
            s = s + expand_rows(bias_ref[0, 0])
        if variant == "fox":
            ck = cn_ref[0] if is_new else ck_ref[0]
            s = s + (cq_ref[0] - expand_units(ck))
        if is_new:
            i = expand_rows(lax.broadcasted_iota(I32, (n_new, LANES), 0))
            lane = lax.broadcasted_iota(I32, (rows, LANES), 1)
            s = jnp.where(lane <= i, s, NEG)
        m_prev = m_ref[...]
        m_new = jnp.maximum(m_prev, jnp.max(s, axis=-1, keepdims=True))
        alpha = jnp.exp(m_prev - m_new)
        pr = jnp.exp(s - m_new)
        l_ref[...] = alpha * l_ref[...] + jnp.sum(pr, axis=-1, keepdims=True)
        m_ref[...] = m_new
        pv = jnp.dot(pr.astype(BF16), v.astype(BF16), preferred_element_type=F32)
        acc_ref[...] = jnp.tile(alpha, (1, wv // LANES)) * acc_ref[...] + pv

    @pl.when(p < n_pages)
    def _past():
        step(kpool_ref[0], vpool_ref[0], False)

    @pl.when(p == n_pages)
    def _new():
        step(knew_ref[0], vnew_ref[0], True)
        accn = acc_ref[...] / jnp.tile(l_ref[...], (1, wv // LANES))
        cu = _div_pow2(lax.broadcasted_iota(I32, (n_new, wv), 1), v_unit)
        out = jnp.zeros((n_new, wv), F32)
        if variant == "diff":
            lam = lam_ref[0, 0]
            for h in range(n_units // 2):
                a0 = accn[(2 * h) * n_new:(2 * h + 1) * n_new]
                a1 = accn[(2 * h + 1) * n_new:(2 * h + 2) * n_new]
                out = out + jnp.where(cu == h, a0 - lam * a1, 0.0)
        else:
            for u in range(n_units):
                out = out + jnp.where(cu == u, accn[u * n_new:(u + 1) * n_new], 0.0)
        o_ref[0] = out


def _attn_decode(variant, page_table, q16, kpool, vpool, knew, vnew, bias=None, cq=None, ck=None, cn=None, lam=None):
    b, n_pages = page_table.shape
    page = kpool.shape[1]
    n_new, w = q16.shape[1], q16.shape[2]
    wv = vpool.shape[2]
    n_units = w // HEAD_DIM
    rows = n_units * n_new
    v_unit = wv // (n_units // 2) if variant == "diff" else HEAD_DIM

    def pool_map(bi, p, pt):
        return (pt[bi, jnp.minimum(p, n_pages - 1)], 0, 0)

    def req_map(bi, p, pt):
        return (bi, 0, 0)

    in_specs = [pl.BlockSpec((1, n_new, w), req_map),
                pl.BlockSpec((1, page, w), pool_map), pl.BlockSpec((1, page, wv), pool_map),
                pl.BlockSpec((1, page, w), req_map), pl.BlockSpec((1, page, wv), req_map)]
    args = [q16, kpool, vpool, knew, vnew]
    if variant == "dsa":
        in_specs.append(pl.BlockSpec((1, 1, n_new, LANES), lambda bi, p, pt: (bi, p, 0, 0)))
        args.append(bias)
    if variant == "fox":
        in_specs.append(pl.BlockSpec((1, rows, LANES), req_map))
        in_specs.append(pl.BlockSpec((1, n_units, LANES), lambda bi, p, pt: (bi, 0, jnp.minimum(p, n_pages - 1))))
        in_specs.append(pl.BlockSpec((1, n_units, LANES), req_map))
        args += [cq, ck, cn]
    if variant == "diff":
        in_specs.append(pl.BlockSpec(memory_space=pltpu.SMEM))
        args.append(lam)
    grid_spec = pltpu.PrefetchScalarGridSpec(
        num_scalar_prefetch=1,
        grid=(b, n_pages + 1),
        in_specs=in_specs,
        out_specs=pl.BlockSpec((1, n_new, wv), req_map),
        scratch_shapes=[pltpu.VMEM((rows, w), BF16), pltpu.VMEM((rows, LANES), F32),
                        pltpu.VMEM((rows, LANES), F32), pltpu.VMEM((rows, wv), F32)],
    )
    return pl.pallas_call(
        functools.partial(_attn_decode_kernel, variant=variant, n_units=n_units, n_new=n_new, n_pages=n_pages,
                          v_unit=v_unit),
        grid_spec=grid_spec,
        out_shape=jax.ShapeDtypeStruct((b, n_new, wv), F32),
        compiler_params=_cparams(2),
        name="attn_decode_" + variant,
    )(page_table, *args)


PAGES_PER_STEP = 8


def _pages_per_step(n_pages):
    g = PAGES_PER_STEP
    while n_pages % g:
        g //= 2
    return g


def _upper_ones():
    r = lax.broadcasted_iota(I32, (LANES, LANES), 0)
    c = lax.broadcasted_iota(I32, (LANES, LANES), 1)
    return jnp.where(r <= c, 1.0, 0.0).astype(F32)


def _cumsum_kernel2(*refs, g):
    x_refs = refs[1:1 + g]
    xn_ref, c_ref, cn_ref, carry_ref = refs[1 + g:5 + g]
    p = pl.program_id(1)
    upper = _upper_ones()

    @pl.when(p == 0)
    def _():
        carry_ref[...] = jnp.zeros(carry_ref.shape, F32)

    total = carry_ref[...]
    for j in range(g):
        cs = jnp.dot(x_refs[j][0], upper, precision=HIGHEST, preferred_element_type=F32) + total
        c_ref[0, :, j * LANES:(j + 1) * LANES] = cs
        total = jnp.broadcast_to(cs[:, LANES - 1:LANES], cs.shape)
    carry_ref[...] = total

    @pl.when(p == pl.num_programs(1) - 1)
    def _():
        cn_ref[0] = jnp.dot(xn_ref[0], upper, precision=HIGHEST, preferred_element_type=F32) + total


def _cumsum_pages2(page_table, pool_t, new_t):
    b, n_pages = page_table.shape
    h = pool_t.shape[1]
    g = _pages_per_step(n_pages)
    pool_specs = [pl.BlockSpec((1, h, LANES), lambda bi, p, pt, j=j: (pt[bi, p * g + j], 0, 0)) for j in range(g)]
    grid_spec = pltpu.PrefetchScalarGridSpec(
        num_scalar_prefetch=1,
        grid=(b, n_pages // g),
        in_specs=pool_specs + [pl.BlockSpec((1, h, LANES), lambda bi, p, pt: (bi, 0, 0))],
        out_specs=[pl.BlockSpec((1, h, g * LANES), lambda bi, p, pt: (bi, 0, p)),
                   pl.BlockSpec((1, h, LANES), lambda bi, p, pt: (bi, 0, 0))],
        scratch_shapes=[pltpu.VMEM((h, LANES), F32)],
    )
    return pl.pallas_call(
        functools.partial(_cumsum_kernel2, g=g),
        grid_spec=grid_spec,
        out_shape=(jax.ShapeDtypeStruct((b, h, n_pages * LANES), F32), jax.ShapeDtypeStruct((b, h, LANES), F32)),
        compiler_params=_cparams(2),
        name="cumsum_pages",
    )(page_table, *([pool_t] * g), new_t)


def _idx_decode_kernel2(*refs, g, n_pages, n_new, nsel, page):
    q_ref, w_ref = refs[1:3]
    k_refs = refs[3:3 + g]
    knew_ref, bias_ref, biasn_ref, s_ref = refs[3 + g:7 + g]
    p = pl.program_id(1)
    n_steps = n_pages // g
    rows = n_new

    def scores(kt):
        s = jnp.dot(q_ref[0], kt.astype(BF16), preferred_element_type=F32)
        t = jnp.maximum(s, 0.0) * w_ref[0]
        acc = t[0:rows]
        for h in range(1, N_IDX_HEADS):
            acc = acc + t[h * rows:(h + 1) * rows]
        return acc

    @pl.when(p < n_steps)
    def _past():
        for j in range(g):
            s_ref[p * g + j] = _sort_key(scores(k_refs[j][0]))

    @pl.when(p == n_steps)
    def _new():
        sc = scores(knew_ref[0])
        i = lax.broadcasted_iota(I32, (rows, LANES), 0)
        lane = lax.broadcasted_iota(I32, (rows, LANES), 1)
        s_ref[n_pages] = _sort_key(jnp.where(lane <= i, sc, -jnp.inf))

        def idx_of(c, grp, nr):
            return lane + c * page

        n_keys = (n_pages + 1) * page
        thr, cut = _select_topk(lambda c, r0, nr: s_ref[c], n_pages + 1, 1, rows, nsel,
                                int(math.log2(n_keys)) + 1, idx_of)

        def selected(c):
            k = s_ref[c]
            sel = (k > thr) | ((k == thr) & (lane + c * page <= cut))
            return jnp.where(sel, 0.0, NEG)

        def write_body(c, carry):
            bias_ref[0, c] = selected(c)
            return carry

        lax.fori_loop(0, n_pages, write_body, 0)
        biasn_ref[0] = selected(n_pages)


def _idx_decode2(page_table, q_st, w_st, kt_pool, kt_new, nsel):
    b, n_pages = page_table.shape
    page = kt_pool.shape[2]
    n_new = q_st.shape[1] // N_IDX_HEADS
    g = _pages_per_step(n_pages)
    n_steps = n_pages // g

    def req_map(bi, p, pt):
        return (bi, 0, 0)

    pool_specs = [pl.BlockSpec((1, IDX_DIM, page),
                               lambda bi, p, pt, j=j: (pt[bi, jnp.minimum(p, n_steps - 1) * g + j], 0, 0))
                  for j in range(g)]
    grid_spec = pltpu.PrefetchScalarGridSpec(
        num_scalar_prefetch=1,
        grid=(b, n_steps + 1),
        in_specs=[pl.BlockSpec((1,) + q_st.shape[1:], req_map), pl.BlockSpec((1,) + w_st.shape[1:], req_map)]
        + pool_specs + [pl.BlockSpec((1, IDX_DIM, page), req_map)],
        out_specs=[pl.BlockSpec((1, n_pages, n_new, LANES), lambda bi, p, pt: (bi, 0, 0, 0)),
                   pl.BlockSpec((1, n_new, LANES), req_map)],
        scratch_shapes=[pltpu.VMEM((n_pages + 1, n_new, LANES), I32)],
    )
    return pl.pallas_call(
        functools.partial(_idx_decode_kernel2, g=g, n_pages=n_pages, n_new=n_new, nsel=nsel, page=page),
        grid_spec=grid_spec,
        out_shape=(jax.ShapeDtypeStruct((b, n_pages, n_new, LANES), F32),
                   jax.ShapeDtypeStruct((b, n_new, LANES), F32)),
        compiler_params=_cparams(2),
        name="idx_decode",
    )(page_table, q_st, w_st, *([kt_pool] * g), kt_new)


def _attn_decode_kernel2(*refs, variant, g, n_units, n_new, n_pages):
    it = iter(refs)
    next(it)
    q_ref = next(it)
    k_refs = [next(it) for _ in range(g)]
    v_refs = [next(it) for _ in range(g)]
    knew_ref, vnew_ref = next(it), next(it)
    bias_ref, biasn_ref = (next(it), next(it)) if variant == "dsa" else (None, None)
    cq_ref, ck_ref, cn_ref = (next(it), next(it), next(it)) if variant == "fox" else (None, None, None)
    lam_ref = next(it) if variant == "diff" else None
    o_ref = next(it)
    qbd_ref, m_ref, l_ref, acc_ref = next(it), next(it), next(it), next(it)

    p = pl.program_id(1)
    n_steps = n_pages // g
    rows = n_units * n_new
    w = q_ref.shape[-1]
    wa = acc_ref.shape[-1]

    @pl.when(p == 0)
    def _init():
        q = q_ref[0].astype(F32)
        qt = jnp.concatenate([q] * n_units, axis=0)
        r = _div_pow2(lax.broadcasted_iota(I32, (rows, w), 0), n_new)
        c = _div_pow2(lax.broadcasted_iota(I32, (rows, w), 1), HEAD_DIM)
        qbd_ref[...] = jnp.where(r == c, qt, 0.0).astype(BF16)
        m_ref[...] = jnp.full(m_ref.shape, NEG, F32)
        l_ref[...] = jnp.zeros(l_ref.shape, F32)
        acc_ref[...] = jnp.zeros(acc_ref.shape, F32)

    def expand_rows(x8):
        return jnp.concatenate([x8] * n_units, axis=0)

    def expand_units(xu):
        return jnp.concatenate([jnp.broadcast_to(xu[u:u + 1, :], (n_new, xu.shape[1])) for u in range(n_units)],
                               axis=0)

    def v_head(v_ref, h, is_new):
        if is_new or len(v_ref.shape) == 3:
            return v_ref[0, :, h * LANES:(h + 1) * LANES]
        return v_ref[0, :, h, :]

    def step(kts, vs, bias, ck, is_new):
        ng = len(kts)
        parts = [jnp.dot(qbd_ref[...], kts[j][0].astype(BF16), preferred_element_type=F32) for j in range(ng)]
        s = parts[0] if ng == 1 else jnp.concatenate(parts, axis=1)
        if variant == "dsa":
            s = s + bias
        if variant == "fox":
            s = s + (jnp.tile(cq_ref[0], (1, ng)) - expand_units(ck))
        if is_new:
            i = expand_rows(lax.broadcasted_iota(I32, (n_new, LANES), 0))
            lane = lax.broadcasted_iota(I32, (rows, LANES), 1)
            s = jnp.where(lane <= i, s, NEG)
        m_prev = m_ref[...]
        m_new = jnp.maximum(m_prev, jnp.max(s, axis=-1, keepdims=True))
        alpha = jnp.exp(m_prev - m_new)
        pr = jnp.exp(s - jnp.tile(m_new, (1, ng)))
        l_ref[...] = alpha * l_ref[...] + jnp.sum(pr, axis=-1, keepdims=True)
        m_ref[...] = m_new
        p16 = pr.astype(BF16)
        acc = jnp.tile(alpha, (1, wa // LANES)) * acc_ref[...]
        for j in range(ng):
            pj = p16[:, j * LANES:(j + 1) * LANES]
            if variant == "diff":
                hr = 2 * n_new
                pv = jnp.concatenate(
                    [jnp.dot(pj[h * hr:(h + 1) * hr], v_head(vs[j], h, is_new).astype(BF16),
                             preferred_element_type=F32) for h in range(n_units // 2)], axis=0)
            else:
                pv = lax.dot_general(pj, vs[j][0].astype(BF16), NT_DIMS, preferred_element_type=F32)
            acc = acc + pv
        acc_ref[...] = acc

    @pl.when(p < n_steps)
    def _past():
        bias = ck = None
        if variant == "dsa":
            bias = jnp.concatenate([expand_rows(bias_ref[0, j]) for j in range(g)], axis=1)
        if variant == "fox":
            ck = ck_ref[0]
        step(k_refs, v_refs, bias, ck, False)

    @pl.when(p == n_steps)
    def _new():
        bias = expand_rows(biasn_ref[0]) if variant == "dsa" else None
        ck = cn_ref[0] if variant == "fox" else None
        step([knew_ref], [vnew_ref], bias, ck, True)
        accn = acc_ref[...] / jnp.tile(l_ref[...], (1, wa // LANES))
        if variant == "diff":
            lam = lam_ref[0, 0]
            o_ref[0] = jnp.concatenate(
                [accn[(2 * h) * n_new:(2 * h + 1) * n_new] - lam * accn[(2 * h + 1) * n_new:(2 * h + 2) * n_new]
                 for h in range(n_units // 2)], axis=1)
        else:
            cu = _div_pow2(lax.broadcasted_iota(I32, (n_new, wa), 1), HEAD_DIM)
            out = jnp.zeros((n_new, wa), F32)
            for u in range(n_units):
                out = out + jnp.where(cu == u, accn[u * n_new:(u + 1) * n_new], 0.0)
            o_ref[0] = out


def _attn_decode2(variant, page_table, q16, kt_pool, v_pool, kt_new, v_new,
                  bias=None, bias_new=None, cq=None, ck=None, cn=None, lam=None):
    b, n_pages = page_table.shape
    page = kt_pool.shape[2]
    n_new, w = q16.shape[1], q16.shape[2]
    n_units = w // HEAD_DIM
    rows = n_units * n_new
    g = _pages_per_step(n_pages)
    n_steps = n_pages // g
    wa = LANES if variant == "diff" else w

    def req_map(bi, p, pt):
        return (bi, 0, 0)

    def pool_spec(arr, j):
        nd = arr.ndim
        return pl.BlockSpec((1,) + arr.shape[1:],
                            lambda bi, p, pt: (pt[bi, jnp.minimum(p, n_steps - 1) * g + j],) + (0,) * (nd - 1))

    in_specs = ([pl.BlockSpec((1, n_new, w), req_map)]
                + [pool_spec(kt_pool, j) for j in range(g)] + [pool_spec(v_pool, j) for j in range(g)]
                + [pl.BlockSpec((1,) + kt_new.shape[1:], req_map), pl.BlockSpec((1,) + v_new.shape[1:], req_map)])
    args = [q16] + [kt_pool] * g + [v_pool] * g + [kt_new, v_new]
    if variant == "dsa":
        in_specs.append(pl.BlockSpec((1, g, n_new, LANES),
                                     lambda bi, p, pt: (bi, jnp.minimum(p, n_steps - 1), 0, 0)))
        in_specs.append(pl.BlockSpec((1, n_new, LANES), req_map))
        args += [bias, bias_new]
    if variant == "fox":
        in_specs.append(pl.BlockSpec((1, rows, LANES), req_map))
        in_specs.append(pl.BlockSpec((1, n_units, g * LANES), lambda bi, p, pt: (bi, 0, jnp.minimum(p, n_steps - 1))))
        in_specs.append(pl.BlockSpec((1, n_units, LANES), req_map))
        args += [cq, ck, cn]
    if variant == "diff":
        in_specs.append(pl.BlockSpec(memory_space=pltpu.SMEM))
        args.append(lam)
    wo = v_new.shape[2] if variant == "diff" else w
    grid_spec = pltpu.PrefetchScalarGridSpec(
        num_scalar_prefetch=1,
        grid=(b, n_steps + 1),
        in_specs=in_specs,
        out_specs=pl.BlockSpec((1, n_new, wo), req_map),
        scratch_shapes=[pltpu.VMEM((rows, w), BF16), pltpu.VMEM((rows, LANES), F32),
                        pltpu.VMEM((rows, LANES), F32), pltpu.VMEM((rows, wa), F32)],
    )
    return pl.pallas_call(
        functools.partial(_attn_decode_kernel2, variant=variant, g=g, n_units=n_units, n_new=n_new,
                          n_pages=n_pages),
        grid_spec=grid_spec,
        out_shape=jax.ShapeDtypeStruct((b, n_new, wo), F32),
        compiler_params=_cparams(2),
        name="attn_decode_" + variant,
    )(page_table, *args)


def _feature_major_pages(cache):
    n_pool, page = cache.shape[0], cache.shape[1]
    perm = (0,) + tuple(range(2, cache.ndim)) + (1,)
    return jnp.transpose(cache, perm).reshape(n_pool, -1, page)


def _feature_major_new(x3, page):
    xt = jnp.swapaxes(x3, 1, 2)
    return jnp.pad(xt, ((0, 0), (0, 0), (0, page - xt.shape[2])))


def _diff_lambda(p, lam_init):
    def e(a, c):
        return jnp.exp(jnp.sum(a.astype(F32) * c.astype(F32)))
    return (e(p["lam_q1"], p["lam_k1"]) - e(p["lam_q2"], p["lam_k2"]) + lam_init).reshape(1, 1).astype(F32)


def _pad_rows(x, rows):
    return jnp.pad(x, ((0, 0), (0, rows - x.shape[1]), (0, 0)))


def _tile_for(m, pref):
    t = min(m, pref)
    while m % t:
        t //= 2
    return t


def _even_prompt(x, layer, p, tiles):
    b, seq, d = x.shape
    tm, tq, tk = tiles
    pos = jnp.tile(jnp.arange(seq), b)
    (qa, ka, ka16, va, va16, ga, qi, ki, ki16, wi, qb, kb, kb16, vb, vb16, gb) = _proj0(
        x.reshape(b * seq, d), pos, p, tm)
    lam_init = 0.8 - 0.6 * math.exp(-0.3 * layer)
    lam = _diff_lambda(p, lam_init)
    nsel = min(TOPK_MAX, seq // 4)
    r3 = lambda t: t.reshape(b, seq, t.shape[-1])
    bias = _idx_prompt(r3(qi), r3(wi), r3(ki16), nsel, tq, tk)
    o_a = _flash_prompt("dsa", r3(qa), r3(ka16), r3(va16), tq, tk, bias=bias)
    o_b = _flash_prompt("diff", r3(qb), r3(kb16), r3(vb16), tq, tk, lam=lam)
    y = _out0(x.reshape(b * seq, d), o_a.reshape(b * seq, -1), ga, o_b.reshape(b * seq, -1), gb, p, lam_init, tm)
    new = (ka.reshape(b, seq, N_HEADS_A, HEAD_DIM), va.reshape(b, seq, N_HEADS_A, HEAD_DIM),
           ki.reshape(b, seq, IDX_DIM), kb.reshape(b, seq, N_HEADS_B, 2, HEAD_DIM),
           vb.reshape(b, seq, N_HEADS_B, 2 * HEAD_DIM))
    return y.reshape(b, seq, d), new


def _even_sample(x, layer, caches, page_table, p):
    cache_k_a, cache_v_a, cache_k_i, cache_k_b, cache_v_b = caches
    b, n_new, d = x.shape
    n_pages = page_table.shape[1]
    n_pool, page = cache_k_a.shape[0], cache_k_a.shape[1]
    past = n_pages * page
    pos = jnp.tile(past + jnp.arange(n_new), b)
    m = b * n_new
    (qa, ka, _ka16, va, _va16, ga, qi, ki, _ki16, wi, qb, kb, _kb16, vb, _vb16, gb) = _proj0(
        x.reshape(m, d), pos, p, _tile_for(m, 256))
    lam_init = 0.8 - 0.6 * math.exp(-0.3 * layer)
    lam = _diff_lambda(p, lam_init)
    nsel = min(TOPK_MAX, (past + n_new) // 4)
    r3 = lambda t: t.reshape(b, n_new, t.shape[-1])
    q_st = jnp.swapaxes(qi.reshape(b, n_new, N_IDX_HEADS, IDX_DIM), 1, 2).reshape(b, N_IDX_HEADS * n_new, IDX_DIM)
    w_st = jnp.swapaxes(wi.reshape(b, n_new, N_IDX_HEADS), 1, 2).reshape(b, N_IDX_HEADS * n_new, 1)
    w_st = jnp.broadcast_to(w_st, (b, N_IDX_HEADS * n_new, LANES))
    bias, bias_new = _idx_decode2(page_table, q_st, w_st, _feature_major_pages(cache_k_i),
                                  _feature_major_new(r3(ki), page), nsel)
    o_a = _attn_decode2("dsa", page_table, r3(qa), _feature_major_pages(cache_k_a), _feature_major_pages(cache_v_a),
                        _feature_major_new(r3(ka), page), _feature_major_new(r3(va), page),
                        bias=bias, bias_new=bias_new)
    o_b = _attn_decode2("diff", page_table, r3(qb), _feature_major_pages(cache_k_b), cache_v_b,
                        _feature_major_new(r3(kb), page), _pad_rows(r3(vb), page), lam=lam)
    y = _out0(x.reshape(m, d), o_a.reshape(m, -1), ga, o_b.reshape(m, -1), gb, p, lam_init, _tile_for(m, 256))
    new = (ka.reshape(b, n_new, N_HEADS_A, HEAD_DIM), va.reshape(b, n_new, N_HEADS_A, HEAD_DIM),
           ki.reshape(b, n_new, IDX_DIM), kb.reshape(b, n_new, N_HEADS_B, 2, HEAD_DIM),
           vb.reshape(b, n_new, N_HEADS_B, 2 * HEAD_DIM))
    return y.reshape(b, n_new, d), new


def _odd_prompt(x, p, tiles):
    b, seq, d = x.shape
    tm, tq, tk = tiles
    q16, k, k16, v, v16, gate, logf = _proj1(x.reshape(b * seq, d), p, tm)
    r3 = lambda t: t.reshape(b, seq, t.shape[-1])
    n_blk = seq // LANES
    logf_t = jnp.swapaxes(logf.reshape(b * n_blk, LANES, N_HEADS_C), 1, 2)
    ident = jnp.arange(b * n_blk, dtype=I32).reshape(b, n_blk)
    c_t, _ = _cumsum_pages2(ident, logf_t, jnp.zeros((b, N_HEADS_C, LANES), F32))
    cq = jnp.swapaxes(c_t, 1, 2)
    o = _flash_prompt("fox", r3(q16), r3(k16), r3(v16), tq, tk, cq=cq, ck=c_t)
    y = _out1(x.reshape(b * seq, d), o.reshape(b * seq, -1), gate, p, tm)
    new = (k.reshape(b, seq, N_HEADS_C, HEAD_DIM), v.reshape(b, seq, N_HEADS_C, HEAD_DIM),
           logf.reshape(b, seq, N_HEADS_C))
    return y.reshape(b, seq, d), new


def _odd_sample(x, caches, page_table, p):
    cache_k, cache_v, cache_logf = caches
    b, n_new, d = x.shape
    n_pool, page = cache_k.shape[0], cache_k.shape[1]
    m = b * n_new
    q16, k, _k16, v, _v16, gate, logf = _proj1(x.reshape(m, d), p, _tile_for(m, 256))
    r3 = lambda t: t.reshape(b, n_new, t.shape[-1])
    c_past_t, c_new_t = _cumsum_pages2(page_table, _feature_major_pages(cache_logf.astype(F32)),
                                       _feature_major_new(r3(logf), page))
    cq = jnp.broadcast_to(c_new_t[:, :, :n_new].reshape(b, N_HEADS_C * n_new, 1), (b, N_HEADS_C * n_new, LANES))
    o = _attn_decode2("fox", page_table, r3(q16), _feature_major_pages(cache_k), _feature_major_pages(cache_v),
                      _feature_major_new(r3(k), page), _feature_major_new(r3(v), page),
                      cq=cq, ck=c_past_t, cn=c_new_t)
    y = _out1(x.reshape(m, d), o.reshape(m, -1), gate, p, _tile_for(m, 256))
    new = (k.reshape(b, n_new, N_HEADS_C, HEAD_DIM), v.reshape(b, n_new, N_HEADS_C, HEAD_DIM),
           logf.reshape(b, n_new, N_HEADS_C))
    return y.reshape(b, n_new, d), new


def kernel(x_prompt, x_sample, cache_l0_k_a, cache_l0_v_a, cache_l0_k_idx, cache_l0_k_b, cache_l0_v_b,
           cache_l1_k_c, cache_l1_v_c, cache_l1_logf_c, page_table,
           l0_norm, l0_w_in, l0_qn_a, l0_kn_a, l0_kn_idx, l0_qn_b, l0_kn_b,
           l0_lam_q1, l0_lam_k1, l0_lam_q2, l0_lam_k2, l0_subln_b, l0_w_out,
           l1_norm, l1_w_in, l1_b_f, l1_qn, l1_kn, l1_w_out):
    p0 = dict(norm=l0_norm, w_in=l0_w_in, qn_a=l0_qn_a, kn_a=l0_kn_a, kn_idx=l0_kn_idx, qn_b=l0_qn_b, kn_b=l0_kn_b,
              lam_q1=l0_lam_q1, lam_k1=l0_lam_k1, lam_q2=l0_lam_q2, lam_k2=l0_lam_k2, subln_b=l0_subln_b,
              w_out=l0_w_out)
    p1 = dict(norm=l1_norm, w_in=l1_w_in, b_f=l1_b_f, qn=l1_qn, kn=l1_kn, w_out=l1_w_out)
    b, seq, _ = x_prompt.shape
    tiles = (_tile_for(b * seq, ROW_TILE), _tile_for(seq, Q_TILE), _tile_for(seq, K_TILE))
    page_table = page_table.astype(I32)
    xp, sp0 = _even_prompt(x_prompt, 0, p0, tiles)
    xs, ss0 = _even_sample(x_sample, 0, (cache_l0_k_a, cache_l0_v_a, cache_l0_k_idx, cache_l0_k_b, cache_l0_v_b),
                           page_table, p0)
    xp, sp1 = _odd_prompt(xp, p1, tiles)
    xs, ss1 = _odd_sample(xs, (cache_l1_k_c, cache_l1_v_c, cache_l1_logf_c), page_table, p1)
    (p_k_a, p_v_a, p_k_idx, p_k_b, p_v_b), (p_k_c, p_v_c, p_logf_c) = sp0, sp1
    (s_k_a, s_v_a, s_k_idx, s_k_b, s_v_b), (s_k_c, s_v_c, s_logf_c) = ss0, ss1
    return (xp, xs, p_k_a, s_k_a, p_v_a, s_v_a, p_k_idx, s_k_idx, p_k_b, s_k_b, p_v_b, s_v_b,
            p_k_c, s_k_c, p_v_c, s_v_c, p_logf_c, s_logf_c)
```

```python
import functools
import math

import jax
import jax.numpy as jnp
from jax import lax
from jax.experimental import pallas as pl
from jax.experimental.pallas import tpu as pltpu

F32 = jnp.float32
BF16 = jnp.bfloat16
I32 = jnp.int32

HEAD_DIM = 64
ROPE_THETA = 500000.0
N_HEADS_A = 8
N_IDX_HEADS = 8
IDX_DIM = 64
TOPK_MAX = 256
N_HEADS_B = 4
N_HEADS_C = 16
EPS = 1e-6
WIDTH_A = N_HEADS_A * HEAD_DIM
WIDTH_B = N_HEADS_B * 2 * HEAD_DIM
WIDTH_C = N_HEADS_C * HEAD_DIM
WIDTH_I = N_IDX_HEADS * IDX_DIM
QK_SCALE = HEAD_DIM ** -0.5
LOG2E = 1.4426950408889634

LANES = 128
SUBLANES = 8
MXU_DIM = 256
VMEM_LIMIT = 56 * 1024 * 1024
ROW_TILE = 256
Q_TILE = 256
K_TILE = 512
COUNT_ROWS = 64
COUNT_PARTS = 8
FLASH_SUB = 512
FLASH_GROUP = 8

NEG = -1e30
HIGHEST = lax.Precision.HIGHEST
NT_DIMS = (((1,), (1,)), ((), ()))

KEY_NEG_INF = -2139095041 - 0


def _cparams(n_axes):
    return pltpu.CompilerParams(
        dimension_semantics=("arbitrary",) * n_axes, vmem_limit_bytes=VMEM_LIMIT)


def _div_pow2(x, d):
    assert d & (d - 1) == 0
    return lax.shift_right_logical(x, jnp.int32(d.bit_length() - 1))


def _sort_key(x):
    bits = pltpu.bitcast(x, I32)
    return bits ^ ((bits >> 31) & jnp.int32(0x7FFFFFFF))


def _rms_rows(x, g):
    ms = jnp.mean(x * x, axis=-1, keepdims=True)
    return x * lax.rsqrt(ms + EPS) * g


def _head_norm(h, bd, gain):
    w = h.shape[-1]
    hs = h * h
    if w >= MXU_DIM:
        cols = [jnp.dot(hs[:, c * MXU_DIM:(c + 1) * MXU_DIM], bd, precision=HIGHEST,
                        preferred_element_type=F32) for c in range(w // MXU_DIM)]
        ms = cols[0] if len(cols) == 1 else jnp.concatenate(cols, axis=-1)
    else:
        ms = jnp.dot(hs, bd[:w, :w], precision=HIGHEST, preferred_element_type=F32)
    return h * lax.rsqrt(ms + EPS) * gain


def _rope(y, rope_ref):
    c = rope_ref[:, 0:LANES]
    s_lo = rope_ref[:, LANES:2 * LANES]
    s_hi = rope_ref[:, 2 * LANES:3 * LANES]
    outs = []
    for j in range(y.shape[-1] // LANES):
        yc = y[:, j * LANES:(j + 1) * LANES]
        outs.append(yc * c + pltpu.roll(yc, LANES - 8, 1) * s_lo + pltpu.roll(yc, 8, 1) * s_hi)
    return outs[0] if len(outs) == 1 else jnp.concatenate(outs, axis=-1)


def _proj0_kernel(x_ref, g_ref, w_ref, ws_ref, rope_ref, gains_ref, gki_ref, bd_ref,
                  qa_ref, ka_ref, ka16_ref, va_ref, va16_ref, ga_ref,
                  qi_ref, ki_ref, ki16_ref, wi_ref,
                  qb_ref, kb_ref, kb16_ref, vb_ref, vb16_ref, gb_ref):
    xb = _rms_rows(x_ref[...], g_ref[...]).astype(BF16)
    bd = bd_ref[...]
    w512 = WIDTH_A

    def piece(j):
        return jnp.dot(xb, w_ref[:, j * w512:(j + 1) * w512], preferred_element_type=F32)

    q_a = _rope(_head_norm(piece(0), bd, gains_ref[0:1, :]), rope_ref)
    qa_ref[...] = (q_a * QK_SCALE).astype(BF16)
    k_a = _rope(_head_norm(piece(1), bd, gains_ref[1:2, :]), rope_ref)
    ka_ref[...] = k_a
    ka16_ref[...] = k_a.astype(BF16)
    v_a = piece(2)
    va_ref[...] = v_a
    va16_ref[...] = v_a.astype(BF16)
    ga_ref[...] = piece(3)
    qi_ref[...] = _rope(piece(4), rope_ref).astype(BF16)
    q_b = _rope(_head_norm(piece(5), bd, gains_ref[2:3, :]), rope_ref)
    qb_ref[...] = (q_b * QK_SCALE).astype(BF16)
    k_b = _rope(_head_norm(piece(6), bd, gains_ref[3:4, :]), rope_ref)
    kb_ref[...] = k_b
    kb16_ref[...] = k_b.astype(BF16)
    v_b = piece(7)
    vb_ref[...] = v_b
    vb16_ref[...] = v_b.astype(BF16)
    gb_ref[...] = piece(8)
    hs = jnp.dot(xb, ws_ref[...], preferred_element_type=F32)
    k_i = _rope(_head_norm(hs[:, 0:LANES], bd, gki_ref[...]), rope_ref)
    ki_ref[...] = k_i[:, 0:IDX_DIM]
    ki16_ref[...] = k_i.astype(BF16)
    wi_ref[...] = hs[:, LANES:LANES + N_IDX_HEADS] * (WIDTH_I ** -0.5)


def _proj1_kernel(x_ref, g_ref, w_ref, wf_ref, bf_ref, gains_ref, bd_ref,
                  q_ref, k_ref, k16_ref, v_ref, v16_ref, gate_ref, logf_ref):
    xb = _rms_rows(x_ref[...], g_ref[...]).astype(BF16)
    bd = bd_ref[...]
    wc = WIDTH_C

    def piece(j):
        return jnp.dot(xb, w_ref[:, j * wc:(j + 1) * wc], preferred_element_type=F32)

    q_ref[...] = (_head_norm(piece(0), bd, gains_ref[0:1, :]) * QK_SCALE).astype(BF16)
    k = _head_norm(piece(1), bd, gains_ref[1:2, :])
    k_ref[...] = k
    k16_ref[...] = k.astype(BF16)
    v = piece(2)
    v_ref[...] = v
    v16_ref[...] = v.astype(BF16)
    gate_ref[...] = piece(3)
    f = jnp.dot(xb, wf_ref[...], preferred_element_type=F32)[:, 0:N_HEADS_C] + bf_ref[...]
    logf_ref[...] = jnp.minimum(f, 0.0) - jnp.log1p(jnp.exp(-jnp.abs(f)))


def _row_spec(tm, w):
    return pl.BlockSpec((tm, w), lambda i: (i, 0))


def _const_spec(shape):
    return pl.BlockSpec(shape, lambda i: (0,) * len(shape))


def _block_diag_mean():
    r = lax.broadcasted_iota(I32, (MXU_DIM, MXU_DIM), 0) // HEAD_DIM
    c = lax.broadcasted_iota(I32, (MXU_DIM, MXU_DIM), 1) // HEAD_DIM
    return jnp.where(r == c, 1.0 / HEAD_DIM, 0.0).astype(F32)


def _rope_table(pos):
    rot = HEAD_DIM // 4
    half = rot // 2
    inv = jnp.power(F32(ROPE_THETA), -jnp.arange(half, dtype=F32) * (2.0 / rot))
    ang = pos.astype(F32)[:, None] * inv[None, :]
    cos, sin = jnp.cos(ang), jnp.sin(ang)
    m = pos.shape[0]
    ones = jnp.ones((m, HEAD_DIM - rot), F32)
    zeros = jnp.zeros((m, HEAD_DIM - rot), F32)
    zh = jnp.zeros((m, half), F32)
    c = jnp.concatenate([cos, cos, ones], axis=-1)
    s_lo = jnp.concatenate([-sin, zh, zeros], axis=-1)
    s_hi = jnp.concatenate([zh, sin, zeros], axis=-1)
    return jnp.concatenate([c, c, s_lo, s_lo, s_hi, s_hi], axis=-1)


def _tile_gain(g, w):
    return jnp.tile(g.astype(F32), w // g.shape[0])[None, :]


def _proj0(x2d, pos, p, tm):
    m, d = x2d.shape
    sizes = (WIDTH_A,) * 4 + (WIDTH_I, IDX_DIM, N_IDX_HEADS) + (WIDTH_B,) * 4
    offs = [0]
    for s in sizes:
        offs.append(offs[-1] + s)
    w_in = p["w_in"]
    cols = [w_in[:, offs[i]:offs[i + 1]] for i in range(len(sizes))]
    w_big = jnp.concatenate([cols[0], cols[1], cols[2], cols[3], cols[4], cols[7], cols[8], cols[9], cols[10]],
                            axis=1).astype(BF16)
    w_small = jnp.concatenate(
        [cols[5], cols[5], cols[6], jnp.zeros((d, LANES - N_IDX_HEADS), w_in.dtype)], axis=1).astype(BF16)
    gains = jnp.concatenate([_tile_gain(p["qn_a"], WIDTH_A), _tile_gain(p["kn_a"], WIDTH_A),
                             _tile_gain(p["qn_b"], WIDTH_B), _tile_gain(p["kn_b"], WIDTH_B)], axis=0)
    gki = _tile_gain(p["kn_idx"], LANES)
    rope = _rope_table(pos)
    w5 = WIDTH_A
    f32o = lambda w: jax.ShapeDtypeStruct((m, w), F32)
    b16o = lambda w: jax.ShapeDtypeStruct((m, w), BF16)
    out_shape = (b16o(w5), f32o(w5), b16o(w5), f32o(w5), b16o(w5), f32o(w5),
                 b16o(w5), f32o(IDX_DIM), b16o(LANES), f32o(N_IDX_HEADS),
                 b16o(w5), f32o(w5), b16o(w5), f32o(w5), b16o(w5), f32o(w5))
    out_specs = tuple(_row_spec(tm, s.shape[1]) for s in out_shape)
    return pl.pallas_call(
        _proj0_kernel,
        grid=(m // tm,),
        in_specs=[_row_spec(tm, d), _const_spec((1, d)), _const_spec(w_big.shape), _const_spec(w_small.shape),
                  _row_spec(tm, 3 * LANES), _const_spec(gains.shape), _const_spec(gki.shape),
                  _const_spec((MXU_DIM, MXU_DIM))],
        out_specs=out_specs,
        out_shape=out_shape,
        compiler_params=_cparams(1),
        name="proj0",
    )(x2d, p["norm"].astype(F32)[None, :], w_big, w_small, rope, gains, gki, _block_diag_mean())


def _proj1(x2d, p, tm):
    m, d = x2d.shape
    wc = WIDTH_C
    w_in = p["w_in"]
    w_big = w_in[:, :4 * wc].astype(BF16)
    w_f = jnp.concatenate([w_in[:, 4 * wc:], jnp.zeros((d, LANES - N_HEADS_C), w_in.dtype)], axis=1).astype(BF16)
    gains = jnp.concatenate([_tile_gain(p["qn"], wc), _tile_gain(p["kn"], wc)], axis=0)
    f32o = lambda w: jax.ShapeDtypeStruct((m, w), F32)
    b16o = lambda w: jax.ShapeDtypeStruct((m, w), BF16)
    out_shape = (b16o(wc), f32o(wc), b16o(wc), f32o(wc), b16o(wc), f32o(wc), f32o(N_HEADS_C))
    out_specs = tuple(_row_spec(tm, s.shape[1]) for s in out_shape)
    return pl.pallas_call(
        _proj1_kernel,
        grid=(m // tm,),
        in_specs=[_row_spec(tm, d), _const_spec((1, d)), _const_spec(w_big.shape), _const_spec(w_f.shape),
                  _const_spec((1, N_HEADS_C)), _const_spec(gains.shape), _const_spec((MXU_DIM, MXU_DIM))],
        out_specs=out_specs,
        out_shape=out_shape,
        compiler_params=_cparams(1),
        name="proj1",
    )(x2d, p["norm"].astype(F32)[None, :], w_big, w_f, p["b_f"].astype(F32)[None, :], gains, _block_diag_mean())


def _head_norm_t(h, bd, gain):
    w, tm = h.shape
    hs = h * h
    slabs = [jnp.dot(bd, hs[c * MXU_DIM:(c + 1) * MXU_DIM, :], precision=HIGHEST, preferred_element_type=F32)
             for c in range(w // MXU_DIM)]
    ms = slabs[0] if len(slabs) == 1 else jnp.concatenate(slabs, axis=0)
    return h * lax.rsqrt(ms + EPS) * jnp.tile(gain, (1, tm // LANES))


def _rope_t(y, rope_ref):
    half = HEAD_DIM // 8
    cos = rope_ref[0:half, :]
    sin = rope_ref[half:2 * half, :]
    parts = []
    for h in range(y.shape[0] // HEAD_DIM):
        r0 = h * HEAD_DIM
        x1 = y[r0:r0 + half, :]
        x2 = y[r0 + half:r0 + 2 * half, :]
        parts += [x1 * cos - x2 * sin, x2 * cos + x1 * sin, y[r0 + 2 * half:r0 + HEAD_DIM, :]]
    return jnp.concatenate(parts, axis=0)


def _proj_t_kernel(*refs, specs, pw, n_small):
    x_ref, g_ref, wt_ref, ws_ref, rope_ref, gains_ref, bd_ref = refs[:7]
    outs = refs[7:]
    xb = _rms_rows(x_ref[...], g_ref[...]).astype(BF16)
    bd = bd_ref[...]
    for j, (gain_row, rope, scale) in enumerate(specs):
        h = lax.dot_general(wt_ref[j * pw:(j + 1) * pw, :], xb, NT_DIMS, preferred_element_type=F32)
        if gain_row is not None:
            h = _head_norm_t(h, bd, gains_ref[gain_row])
        if rope:
            h = _rope_t(h, rope_ref)
        if scale != 1.0:
            h = h * scale
        outs[j][...] = h.astype(BF16)
    if n_small:
        hs = lax.dot_general(ws_ref[...], xb, NT_DIMS, preferred_element_type=F32)
        outs[len(specs)][...] = hs * (WIDTH_I ** -0.5)


def _rope_table_t(pos):
    rot = HEAD_DIM // 4
    half = rot // 2
    inv = jnp.power(F32(ROPE_THETA), -jnp.arange(half, dtype=F32) * (2.0 / rot))
    ang = inv[:, None] * pos.astype(F32)[None, :]
    return jnp.concatenate([jnp.cos(ang), jnp.sin(ang)], axis=0)


def _proj_t(x2d, norm, w_cols, specs, gains, pos, w_small, tm):
    m, d = x2d.shape
    pw = w_cols[0].shape[1]
    wt = jnp.concatenate([w.T for w in w_cols], axis=0).astype(BF16)
    n_small = 0 if w_small is None else w_small.shape[1]
    ws = jnp.zeros((SUBLANES, d), BF16) if w_small is None else w_small.T.astype(BF16)
    gains_arr = (jnp.zeros((1, pw, LANES), F32) if not gains else
                 jnp.stack([jnp.broadcast_to(jnp.tile(g.astype(F32), pw // g.shape[0])[:, None], (pw, LANES))
                            for g in gains]))
    rope = jnp.zeros((2 * SUBLANES, m), F32) if pos is None else _rope_table_t(pos)
    out_shape = [jax.ShapeDtypeStruct((pw, m), BF16) for _ in specs]
    out_specs = [pl.BlockSpec((pw, tm), lambda i: (0, i)) for _ in specs]
    if n_small:
        out_shape.append(jax.ShapeDtypeStruct((n_small, m), F32))
        out_specs.append(pl.BlockSpec((n_small, tm), lambda i: (0, i)))
    return pl.pallas_call(
        functools.partial(_proj_t_kernel, specs=tuple(specs), pw=pw, n_small=n_small),
        grid=(m // tm,),
        in_specs=[_row_spec(tm, d), _const_spec((1, d)), _const_spec(wt.shape), _const_spec(ws.shape),
                  pl.BlockSpec((2 * SUBLANES, tm), lambda i: (0, i)), _const_spec(gains_arr.shape),
                  _const_spec((MXU_DIM, MXU_DIM))],
        out_specs=out_specs,
        out_shape=out_shape,
        compiler_params=_cparams(1),
        name="proj_t",
    )(x2d, norm.astype(F32)[None, :], wt, ws, rope, gains_arr, _block_diag_mean())


def _silu(g):
    return g * jax.nn.sigmoid(g)


def _out0_kernel(x_ref, oa_ref, ga_ref, ob_ref, gb_ref, sub_ref, w_ref, y_ref, *, post_scale):
    ya = (oa_ref[...] * _silu(ga_ref[...])).astype(BF16)
    ob = ob_ref[...]
    cols = []
    for h in range(N_HEADS_B):
        oc = ob[:, h * LANES:(h + 1) * LANES]
        ms = jnp.mean(oc * oc, axis=-1, keepdims=True)
        cols.append(oc * lax.rsqrt(ms + EPS))
    obn = jnp.concatenate(cols, axis=-1) * sub_ref[...] * post_scale
    yb = (obn * _silu(gb_ref[...])).astype(BF16)
    y = jnp.dot(ya, w_ref[0:WIDTH_A, :], preferred_element_type=F32)
    y = y + jnp.dot(yb, w_ref[WIDTH_A:WIDTH_A + WIDTH_B, :], preferred_element_type=F32)
    y_ref[...] = x_ref[...] + y


def _out1_kernel(x_ref, o_ref, g_ref, w_ref, y_ref):
    yo = (o_ref[...] * _silu(g_ref[...])).astype(BF16)
    y_ref[...] = x_ref[...] + jnp.dot(yo, w_ref[...], preferred_element_type=F32)


def _out0(x2d, o_a, g_a, o_b, g_b, p, lam_init, tm):
    m, d = x2d.shape
    sub = _tile_gain(p["subln_b"], WIDTH_B)
    w = p["w_out"].astype(BF16)
    return pl.pallas_call(
        functools.partial(_out0_kernel, post_scale=1.0 - lam_init),
        grid=(m // tm,),
        in_specs=[_row_spec(tm, d), _row_spec(tm, WIDTH_A), _row_spec(tm, WIDTH_A), _row_spec(tm, WIDTH_B),
                  _row_spec(tm, WIDTH_B), _const_spec(sub.shape), _const_spec(w.shape)],
        out_specs=_row_spec(tm, d),
        out_shape=jax.ShapeDtypeStruct((m, d), F32),
        compiler_params=_cparams(1),
        name="out0",
    )(x2d, o_a, g_a, o_b, g_b, sub, w)


def _out1(x2d, o, g, p, tm):
    m, d = x2d.shape
    w = p["w_out"].astype(BF16)
    return pl.pallas_call(
        _out1_kernel,
        grid=(m // tm,),
        in_specs=[_row_spec(tm, d), _row_spec(tm, WIDTH_C), _row_spec(tm, WIDTH_C), _const_spec(w.shape)],
        out_specs=_row_spec(tm, d),
        out_shape=jax.ShapeDtypeStruct((m, d), F32),
        compiler_params=_cparams(1),
        name="out1",
    )(x2d, o, g, w)


def _select_topk(load_keys, n_chunks, groups, rows, nsel, idx_bits, idx_of):
    shape = (rows, LANES)
    rb = min(rows, COUNT_ROWS)
    n_blk = rows // rb

    def count(pred_of):
        accs = []
        for b in range(n_blk):
            pred = pred_of(slice(b * rb, (b + 1) * rb))

            def body(c, acc, b=b, pred=pred):
                blk = load_keys(c, b * rb, rb)
                for g in range(groups):
                    acc = jnp.where(pred(blk[:, g * LANES:(g + 1) * LANES], c, g), acc + 1.0, acc)
                return acc
            accs.append(lax.fori_loop(0, n_chunks, body, jnp.zeros((rb, LANES), F32),
                                      unroll=isinstance(n_chunks, int)))
        outs = [jnp.broadcast_to(jnp.sum(a, axis=-1, keepdims=True), (rb, LANES)) for a in accs]
        return outs[0] if n_blk == 1 else jnp.concatenate(outs, axis=0)

    kf = float(nsel)

    def bit_step(i, thr):
        bit = lax.shift_left(jnp.int32(1), jnp.int32(31) - i)
        cand = thr ^ bit

        def pred_of(sl):
            cb = cand[sl]
            return lambda k, c, g: k >= cb
        return jnp.where(count(pred_of) >= kf, cand, thr)

    thr = lax.fori_loop(0, 32, bit_step, jnp.full(shape, jnp.iinfo(jnp.int32).min, I32))
    thr = jnp.maximum(thr, KEY_NEG_INF + 1)

    def gt_of(sl):
        tb = thr[sl]
        return lambda k, c, g: k > tb

    def ge_of(sl):
        tb = thr[sl]
        return lambda k, c, g: k >= tb

    n_gt = count(gt_of)
    n_ge = count(ge_of)
    need = kf - n_gt
    excess = n_ge - kf

    def tie_cut():
        def idx_step(i, cut):
            bit = lax.shift_left(jnp.int32(1), jnp.int32(idx_bits - 1) - i)
            cand = cut | bit

            def pred_of(sl):
                tb, cb = thr[sl], cand[sl]
                return lambda k, c, g: (k == tb) & (idx_of(c, g, rb) < cb)
            return jnp.where(count(pred_of) < need, cand, cut)
        return lax.fori_loop(0, idx_bits, idx_step, jnp.zeros(shape, I32))

    big = jnp.full(shape, jnp.iinfo(jnp.int32).max, I32)
    any_excess = jnp.max(excess) > 0.0
    cut = lax.cond(any_excess, lambda: jnp.where(excess > 0.0, tie_cut(), big), lambda: big)
    return thr, cut


def _pair_masks(rows):
    lane = lax.broadcasted_iota(I32, (rows, LANES), 1)
    return lane < HEAD_DIM


def _idx_prompt_kernel(qi_ref, wi_ref, ki_ref, bias_ref, qm_ref, wb_ref, s_ref, *, tq, tk, nsel, seq):
    qi = pl.program_id(1)
    q0 = qi * tq
    n_chunks = seq // tk
    nc = (q0 + tq + tk - 1) // tk
    lo = _pair_masks(tq)
    groups = tk // LANES
    for j in range(N_IDX_HEADS // 2):
        pair = qi_ref[0, :, j * LANES:(j + 1) * LANES]
        zero = jnp.zeros_like(pair)
        qm_ref[2 * j] = jnp.where(lo, pair, zero)
        qm_ref[2 * j + 1] = jnp.where(lo, zero, pair)
    w = wi_ref[0]
    for h in range(N_IDX_HEADS):
        wb_ref[h] = jnp.broadcast_to(w[:, h:h + 1], (tq, LANES))
    row = q0 + lax.broadcasted_iota(I32, (tq, tk), 0)
    col0 = lax.broadcasted_iota(I32, (tq, tk), 1)

    def score_body(c, carry):
        start = pl.multiple_of(c * tk, tk)
        kblk = ki_ref[0, pl.ds(start, tk), :]
        acc = jnp.zeros((tq, tk), F32)
        for h in range(N_IDX_HEADS):
            s = lax.dot_general(qm_ref[h], kblk, NT_DIMS, preferred_element_type=F32)
            acc = acc + jnp.maximum(s, 0.0) * jnp.tile(wb_ref[h], (1, groups))
        acc = jnp.where(col0 + c * tk <= row, acc, -jnp.inf)
        s_ref[c] = _sort_key(acc)
        return carry

    lax.fori_loop(0, nc, score_body, 0)

    def idx_of(c, g, nr):
        return lax.broadcasted_iota(I32, (nr, LANES), 1) + (c * tk + g * LANES)

    thr, cut = _select_topk(lambda c, r0, nr: s_ref[c, r0:r0 + nr, :], nc, groups, tq, nsel,
                            int(math.log2(seq)) + 1, idx_of)
    thr_t = jnp.tile(thr, (1, groups))
    cut_t = jnp.tile(cut, (1, groups))

    def write_body(c, carry):
        k = s_ref[c]
        sel = (k > thr_t) | ((k == thr_t) & (col0 + c * tk <= cut_t))
        bias_ref[0, c] = jnp.where(sel, 0.0, NEG).astype(BF16)
        return carry

    lax.fori_loop(0, nc, write_body, 0)

    def fill_body(c, carry):
        bias_ref[0, c] = jnp.full((tq, tk), NEG, BF16)
        return carry

    lax.fori_loop(nc, n_chunks, fill_body, 0)


def _idx_prompt(qi16, wi, ki16, nsel, tq, tk):
    b, seq, _ = qi16.shape
    n_chunks = seq // tk
    return pl.pallas_call(
        functools.partial(_idx_prompt_kernel, tq=tq, tk=tk, nsel=nsel, seq=seq),
        grid=(b, seq // tq),
        in_specs=[pl.BlockSpec((1, tq, WIDTH_I), lambda bi, qi: (bi, qi, 0)),
                  pl.BlockSpec((1, tq, N_IDX_HEADS), lambda bi, qi: (bi, qi, 0)),
                  pl.BlockSpec((1, seq, LANES), lambda bi, qi: (bi, 0, 0))],
        out_specs=pl.BlockSpec((1, n_chunks, tq, tk), lambda bi, qi: (bi, 0, qi, 0)),
        out_shape=jax.ShapeDtypeStruct((b, n_chunks, seq, tk), BF16),
        scratch_shapes=[pltpu.VMEM((N_IDX_HEADS, tq, LANES), BF16),
                        pltpu.VMEM((N_IDX_HEADS, tq, LANES), F32),
                        pltpu.VMEM((n_chunks, tq, tk), I32)],
        compiler_params=_cparams(2),
        name="idx_prompt",
    )(qi16, wi, ki16)


def _flash_prompt_kernel(*refs, variant, n_pairs, tq, tk):
    it = iter(refs)
    q_ref, k_ref, v_ref = next(it), next(it), next(it)
    bias_ref = next(it) if variant == "dsa" else None
    cq_ref = next(it) if variant == "fox" else None
    ck_ref = next(it) if variant == "fox" else None
    lam_ref = next(it) if variant == "diff" else None
    o_ref = next(it)
    qm_ref, m_ref, l_ref, acc_ref = next(it), next(it), next(it), next(it)
    cqb_ref = next(it) if variant == "fox" else None

    qi = pl.program_id(1)
    kc = pl.program_id(2)
    nk = pl.num_programs(2)
    last = ((qi + 1) * tq - 1) // tk
    n_units = 2 * n_pairs
    groups = tk // LANES
    lo = _pair_masks(tq)

    @pl.when(kc == 0)
    def _init():
        for j in range(n_pairs):
            pair = q_ref[0, :, j * LANES:(j + 1) * LANES]
            zero = jnp.zeros_like(pair)
            qm_ref[2 * j] = jnp.where(lo, pair, zero)
            qm_ref[2 * j + 1] = jnp.where(lo, zero, pair)
        m_ref[...] = jnp.full(m_ref.shape, NEG, F32)
        l_ref[...] = jnp.zeros(l_ref.shape, F32)
        acc_ref[...] = jnp.zeros(acc_ref.shape, F32)
        if variant == "fox":
            cq = cq_ref[0]
            for u in range(n_units):
                cqb_ref[u] = jnp.broadcast_to(cq[:, u:u + 1], (tq, LANES))

    def compute(masked):
        if variant == "dsa":
            bias = bias_ref[0, 0].astype(F32)
            masked = False
        if masked:
            row = qi * tq + lax.broadcasted_iota(I32, (tq, tk), 0)
            col = kc * tk + lax.broadcasted_iota(I32, (tq, tk), 1)
            causal = col <= row
        for j in range(n_pairs):
            kp = k_ref[0, :, j * LANES:(j + 1) * LANES]
            vp = v_ref[0, :, j * LANES:(j + 1) * LANES]
            alphas, pvs = [], []
            for e in range(2):
                u = 2 * j + e
                s = lax.dot_general(qm_ref[u], kp, NT_DIMS, preferred_element_type=F32)
                if variant == "dsa":
                    s = s + bias
                if variant == "fox":
                    s = s + (jnp.tile(cqb_ref[u], (1, groups)) - ck_ref[0, u:u + 1, :])
                if masked:
                    s = jnp.where(causal, s, NEG)
                m_prev = m_ref[u]
                m_new = jnp.maximum(m_prev, jnp.max(s, axis=-1, keepdims=True))
                alpha = jnp.exp(m_prev - m_new)
                p = jnp.exp(s - jnp.tile(m_new, (1, groups)))
                l_ref[u] = alpha * l_ref[u] + jnp.sum(p, axis=-1, keepdims=True)
                m_ref[u] = m_new
                pv = jnp.dot(p.astype(BF16), vp, preferred_element_type=F32)
                if variant == "diff":
                    acc_ref[u] = alpha * acc_ref[u] + pv
                else:
                    alphas.append(alpha)
                    pvs.append(pv)
            if variant != "diff":
                acc_ref[j] = jnp.where(lo, alphas[0], alphas[1]) * acc_ref[j] + jnp.where(lo, pvs[0], pvs[1])

    needs_mask = (kc + 1) * tk - 1 > qi * tq

    @pl.when((kc <= last) & needs_mask)
    def _diag():
        compute(True)

    @pl.when((kc <= last) & jnp.logical_not(needs_mask))
    def _full():
        compute(False)

    @pl.when(kc == nk - 1)
    def _fin():
        for j in range(n_pairs):
            if variant == "diff":
                lam = lam_ref[0, 0]
                o_ref[0, :, j * LANES:(j + 1) * LANES] = (
                    acc_ref[2 * j] / l_ref[2 * j] - lam * (acc_ref[2 * j + 1] / l_ref[2 * j + 1]))
            else:
                o_ref[0, :, j * LANES:(j + 1) * LANES] = acc_ref[j] / jnp.where(lo, l_ref[2 * j], l_ref[2 * j + 1])


def _flash_prompt(variant, q16, k16, v16, tq, tk, bias=None, cq=None, ck=None, lam=None):
    b, seq, w = q16.shape
    n_pairs = w // LANES
    n_units = 2 * n_pairs

    def last_of(qi):
        return ((qi + 1) * tq - 1) // tk

    q_spec = pl.BlockSpec((1, tq, w), lambda bi, qi, kc: (bi, qi, 0))
    kv_spec = pl.BlockSpec((1, tk, w), lambda bi, qi, kc: (bi, jnp.minimum(kc, last_of(qi)), 0))
    in_specs = [q_spec, kv_spec, kv_spec]
    args = [q16, k16, v16]
    scratch = [pltpu.VMEM((n_units, tq, LANES), BF16), pltpu.VMEM((n_units, tq, LANES), F32),
               pltpu.VMEM((n_units, tq, LANES), F32),
               pltpu.VMEM((n_units if variant == "diff" else n_pairs, tq, LANES), F32)]
    if variant == "dsa":
        in_specs.append(pl.BlockSpec((1, 1, tq, tk), lambda bi, qi, kc: (bi, jnp.minimum(kc, last_of(qi)), qi, 0)))
        args.append(bias)
    if variant == "fox":
        in_specs.append(pl.BlockSpec((1, tq, n_units), lambda bi, qi, kc: (bi, qi, 0)))
        in_specs.append(pl.BlockSpec((1, n_units, tk), lambda bi, qi, kc: (bi, 0, jnp.minimum(kc, last_of(qi)))))
        args += [cq, ck]
        scratch.append(pltpu.VMEM((n_units, tq, LANES), F32))
    if variant == "diff":
        in_specs.append(pl.BlockSpec(memory_space=pltpu.SMEM))
        args.append(lam)
    return pl.pallas_call(
        functools.partial(_flash_prompt_kernel, variant=variant, n_pairs=n_pairs, tq=tq, tk=tk),
        grid=(b, seq // tq, seq // tk),
        in_specs=in_specs,
        out_specs=pl.BlockSpec((1, tq, w), lambda bi, qi, kc: (bi, qi, 0)),
        out_shape=jax.ShapeDtypeStruct((b, seq, w), F32),
        scratch_shapes=scratch,
        compiler_params=_cparams(3),
        name="flash_" + variant,
    )(*args)


def _cumsum_kernel(pt_ref, x_ref, xn_ref, c_ref, cn_ref, carry_ref):
    del pt_ref
    p = pl.program_id(1)
    n_pages = pl.num_programs(1)
    r = lax.broadcasted_iota(I32, (LANES, LANES), 0)
    c = lax.broadcasted_iota(I32, (LANES, LANES), 1)
    upper = jnp.where(r <= c, 1.0, 0.0).astype(F32)

    @pl.when(p == 0)
    def _():
        carry_ref[...] = jnp.zeros(carry_ref.shape, F32)

    cs = jnp.dot(x_ref[0], upper, precision=HIGHEST, preferred_element_type=F32) + carry_ref[...]
    c_ref[0] = cs
    total = jnp.broadcast_to(cs[:, LANES - 1:LANES], cs.shape)
    carry_ref[...] = total

    @pl.when(p == n_pages - 1)
    def _():
        cn_ref[0] = jnp.dot(xn_ref[0], upper, precision=HIGHEST, preferred_element_type=F32) + total


def _cumsum_pages(page_table, pool_t, new_t):
    b, n_pages = page_table.shape
    h = pool_t.shape[1]
    grid_spec = pltpu.PrefetchScalarGridSpec(
        num_scalar_prefetch=1,
        grid=(b, n_pages),
        in_specs=[pl.BlockSpec((1, h, LANES), lambda bi, p, pt: (pt[bi, p], 0, 0)),
                  pl.BlockSpec((1, h, LANES), lambda bi, p, pt: (bi, 0, 0))],
        out_specs=[pl.BlockSpec((1, h, LANES), lambda bi, p, pt: (bi, 0, p)),
                   pl.BlockSpec((1, h, LANES), lambda bi, p, pt: (bi, 0, 0))],
        scratch_shapes=[pltpu.VMEM((h, LANES), F32)],
    )
    return pl.pallas_call(
        _cumsum_kernel,
        grid_spec=grid_spec,
        out_shape=(jax.ShapeDtypeStruct((b, h, n_pages * LANES), F32), jax.ShapeDtypeStruct((b, h, LANES), F32)),
        compiler_params=_cparams(2),
        name="cumsum_pages",
    )(page_table, pool_t, new_t)


def _idx_decode_kernel(pt_ref, q_ref, w_ref, kpool_ref, knew_ref, bias_ref, s_ref, *, n_pages, n_new, nsel, page):
    del pt_ref
    p = pl.program_id(1)
    rows = n_new

    def scores(kblk):
        s = lax.dot_general(q_ref[0], kblk.astype(BF16), NT_DIMS, preferred_element_type=F32)
        t = jnp.maximum(s, 0.0) * w_ref[0]
        acc = t[0:rows]
        for h in range(1, N_IDX_HEADS):
            acc = acc + t[h * rows:(h + 1) * rows]
        return acc

    @pl.when(p < n_pages)
    def _past():
        s_ref[p] = _sort_key(scores(kpool_ref[0]))

    @pl.when(p == n_pages)
    def _new():
        sc = scores(knew_ref[0])
        i = lax.broadcasted_iota(I32, (rows, LANES), 0)
        lane = lax.broadcasted_iota(I32, (rows, LANES), 1)
        s_ref[n_pages] = _sort_key(jnp.where(lane <= i, sc, -jnp.inf))

        def idx_of(c, g):
            return lane + c * page

        n_keys = (n_pages + 1) * page
        thr, cut = _select_topk(lambda c: s_ref[c], n_pages + 1, 1, rows, nsel, int(math.log2(n_keys)) + 1, idx_of)

        def write_body(c, carry):
            k = s_ref[c]
            sel = (k > thr) | ((k == thr) & (lane + c * page <= cut))
            bias_ref[0, c] = jnp.where(sel, 0.0, NEG)
            return carry

        lax.fori_loop(0, n_pages + 1, write_body, 0)


def _idx_decode(page_table, q_st, w_st, kpool, knew, nsel):
    b, n_pages = page_table.shape
    page = kpool.shape[1]
    n_new = q_st.shape[1] // N_IDX_HEADS
    grid_spec = pltpu.PrefetchScalarGridSpec(
        num_scalar_prefetch=1,
        grid=(b, n_pages + 1),
        in_specs=[pl.BlockSpec((1,) + q_st.shape[1:], lambda bi, p, pt: (bi, 0, 0)),
                  pl.BlockSpec((1,) + w_st.shape[1:], lambda bi, p, pt: (bi, 0, 0)),
                  pl.BlockSpec((1, page, IDX_DIM), lambda bi, p, pt: (pt[bi, jnp.minimum(p, n_pages - 1)], 0, 0)),
                  pl.BlockSpec((1, page, IDX_DIM), lambda bi, p, pt: (bi, 0, 0))],
        out_specs=pl.BlockSpec((1, n_pages + 1, n_new, LANES), lambda bi, p, pt: (bi, 0, 0, 0)),
        scratch_shapes=[pltpu.VMEM((n_pages + 1, n_new, LANES), I32)],
    )
    return pl.pallas_call(
        functools.partial(_idx_decode_kernel, n_pages=n_pages, n_new=n_new, nsel=nsel, page=page),
        grid_spec=grid_spec,
        out_shape=jax.ShapeDtypeStruct((b, n_pages + 1, n_new, LANES), F32),
        compiler_params=_cparams(2),
        name="idx_decode",
    )(page_table, q_st, w_st, kpool, knew)


def _attn_decode_kernel(*refs, variant, n_units, n_new, n_pages, v_unit):
    it = iter(refs)
    _pt_ref = next(it)
    q_ref, kpool_ref, vpool_ref, knew_ref, vnew_ref = next(it), next(it), next(it), next(it), next(it)
    bias_ref = next(it) if variant == "dsa" else None
    cq_ref = next(it) if variant == "fox" else None
    ck_ref = next(it) if variant == "fox" else None
    cn_ref = next(it) if variant == "fox" else None
    lam_ref = next(it) if variant == "diff" else None
    o_ref = next(it)
    qbd_ref, m_ref, l_ref, acc_ref = next(it), next(it), next(it), next(it)

    p = pl.program_id(1)
    rows = n_units * n_new
    w = q_ref.shape[-1]
    wv = acc_ref.shape[-1]

    @pl.when(p == 0)
    def _init():
        q = q_ref[0].astype(F32)
        qt = jnp.concatenate([q] * n_units, axis=0)
        r = _div_pow2(lax.broadcasted_iota(I32, (rows, w), 0), n_new)
        c = _div_pow2(lax.broadcasted_iota(I32, (rows, w), 1), HEAD_DIM)
        qbd_ref[...] = jnp.where(r == c, qt, 0.0).astype(BF16)
        m_ref[...] = jnp.full(m_ref.shape, NEG, F32)
        l_ref[...] = jnp.zeros(l_ref.shape, F32)
        acc_ref[...] = jnp.zeros(acc_ref.shape, F32)

    def expand_rows(x8):
        return jnp.concatenate([x8] * n_units, axis=0)

    def expand_units(xu):
        return jnp.concatenate([jnp.broadcast_to(xu[u:u + 1, :], (n_new, LANES)) for u in range(n_units)], axis=0)

    def step(k, v, is_new):
        s = lax.dot_general(qbd_ref[...], k.astype(BF16), NT_DIMS, preferred_element_type=F32)
        if variant == "dsa":
            s = s + expand_rows(bias_ref[0, 0])
        if variant == "fox":
            ck = cn_ref[0] if is_new else ck_ref[0]
            s = s + (cq_ref[0] - expand_units(ck))
        if is_new:
            i = expand_rows(lax.broadcasted_iota(I32, (n_new, LANES), 0))
            lane = lax.broadcasted_iota(I32, (rows, LANES), 1)
            s = jnp.where(lane <= i, s, NEG)
        m_prev = m_ref[...]
        m_new = jnp.maximum(m_prev, jnp.max(s, axis=-1, keepdims=True))
        alpha = jnp.exp(m_prev - m_new)
        pr = jnp.exp(s - m_new)
        l_ref[...] = alpha * l_ref[...] + jnp.sum(pr, axis=-1, keepdims=True)
        m_ref[...] = m_new
        pv = jnp.dot(pr.astype(BF16), v.astype(BF16), preferred_element_type=F32)
        acc_ref[...] = jnp.tile(alpha, (1, wv // LANES)) * acc_ref[...] + pv

    @pl.when(p < n_pages)
    def _past():
        step(kpool_ref[0], vpool_ref[0], False)

    @pl.when(p == n_pages)
    def _new():
        step(knew_ref[0], vnew_ref[0], True)
        accn = acc_ref[...] / jnp.tile(l_ref[...], (1, wv // LANES))
        cu = _div_pow2(lax.broadcasted_iota(I32, (n_new, wv), 1), v_unit)
        out = jnp.zeros((n_new, wv), F32)
        if variant == "diff":
            lam = lam_ref[0, 0]
            for h in range(n_units // 2):
                a0 = accn[(2 * h) * n_new:(2 * h + 1) * n_new]
                a1 = accn[(2 * h + 1) * n_new:(2 * h + 2) * n_new]
                out = out + jnp.where(cu == h, a0 - lam * a1, 0.0)
        else:
            for u in range(n_units):
                out = out + jnp.where(cu == u, accn[u * n_new:(u + 1) * n_new], 0.0)
        o_ref[0] = out


def _attn_decode(variant, page_table, q16, kpool, vpool, knew, vnew, bias=None, cq=None, ck=None, cn=None, lam=None):
    b, n_pages = page_table.shape
    page = kpool.shape[1]
    n_new, w = q16.shape[1], q16.shape[2]
    wv = vpool.shape[2]
    n_units = w // HEAD_DIM
    rows = n_units * n_new
    v_unit = wv // (n_units // 2) if variant == "diff" else HEAD_DIM

    def pool_map(bi, p, pt):
        return (pt[bi, jnp.minimum(p, n_pages - 1)], 0, 0)

    def req_map(bi, p, pt):
        return (bi, 0, 0)

    in_specs = [pl.BlockSpec((1, n_new, w), req_map),
                pl.BlockSpec((1, page, w), pool_map), pl.BlockSpec((1, page, wv), pool_map),
                pl.BlockSpec((1, page, w), req_map), pl.BlockSpec((1, page, wv), req_map)]
    args = [q16, kpool, vpool, knew, vnew]
    if variant == "dsa":
        in_specs.append(pl.BlockSpec((1, 1, n_new, LANES), lambda bi, p, pt: (bi, p, 0, 0)))
        args.append(bias)
    if variant == "fox":
        in_specs.append(pl.BlockSpec((1, rows, LANES), req_map))
        in_specs.append(pl.BlockSpec((1, n_units, LANES), lambda bi, p, pt: (bi, 0, jnp.minimum(p, n_pages - 1))))
        in_specs.append(pl.BlockSpec((1, n_units, LANES), req_map))
        args += [cq, ck, cn]
    if variant == "diff":
        in_specs.append(pl.BlockSpec(memory_space=pltpu.SMEM))
        args.append(lam)
    grid_spec = pltpu.PrefetchScalarGridSpec(
        num_scalar_prefetch=1,
        grid=(b, n_pages + 1),
        in_specs=in_specs,
        out_specs=pl.BlockSpec((1, n_new, wv), req_map),
        scratch_shapes=[pltpu.VMEM((rows, w), BF16), pltpu.VMEM((rows, LANES), F32),
                        pltpu.VMEM((rows, LANES), F32), pltpu.VMEM((rows, wv), F32)],
    )
    return pl.pallas_call(
        functools.partial(_attn_decode_kernel, variant=variant, n_units=n_units, n_new=n_new, n_pages=n_pages,
                          v_unit=v_unit),
        grid_spec=grid_spec,
        out_shape=jax.ShapeDtypeStruct((b, n_new, wv), F32),
        compiler_params=_cparams(2),
        name="attn_decode_" + variant,
    )(page_table, *args)


def _select_topk_t(load_keys, n_chunks, rpc, cols, nsel, idx_bits):
    shape = (SUBLANES, cols)
    reps = rpc // SUBLANES

    def count(pred):
        def body(c, acc):
            x = jnp.where(pred(load_keys(c), c), 1.0, 0.0)
            n_part = math.gcd(reps, COUNT_PARTS)
            part = jnp.sum(x.reshape(reps // n_part, n_part, SUBLANES, cols), axis=0)
            return acc + jnp.sum(part, axis=0)
        acc = lax.fori_loop(0, n_chunks, body, jnp.zeros(shape, F32))
        return jnp.broadcast_to(jnp.sum(acc, axis=0, keepdims=True), shape)

    kf = float(nsel)

    def bit_step(i, thr):
        bit = lax.shift_left(jnp.int32(1), jnp.int32(31) - i)
        cand = thr ^ bit
        cand_t = jnp.tile(cand, (reps, 1))
        return jnp.where(count(lambda k, c: k >= cand_t) >= kf, cand, thr)

    thr = lax.fori_loop(0, 32, bit_step, jnp.full(shape, jnp.iinfo(jnp.int32).min, I32))
    thr = jnp.maximum(thr, KEY_NEG_INF + 1)
    thr_t = jnp.tile(thr, (reps, 1))
    n_gt = count(lambda k, c: k > thr_t)
    n_ge = count(lambda k, c: k >= thr_t)
    need = kf - n_gt
    excess = n_ge - kf
    row = lax.broadcasted_iota(I32, (rpc, cols), 0)

    def tie_cut():
        def idx_step(i, cut):
            bit = lax.shift_left(jnp.int32(1), jnp.int32(idx_bits - 1) - i)
            cand = cut | bit
            cand_t = jnp.tile(cand, (reps, 1))
            n_lt = count(lambda k, c: (k == thr_t) & (row + c * rpc < cand_t))
            return jnp.where(n_lt < need, cand, cut)
        return lax.fori_loop(0, idx_bits, idx_step, jnp.zeros(shape, I32))

    big = jnp.full(shape, jnp.iinfo(jnp.int32).max, I32)
    any_excess = jnp.max(excess) > 0.0
    cut = lax.cond(any_excess, lambda: jnp.where(excess > 0.0, tie_cut(), big), lambda: big)
    return thr, cut


def _masked_pairs_t(qt_ref, qm_ref, n_pairs, tq):
    top = lax.broadcasted_iota(I32, (LANES, tq), 0) < HEAD_DIM
    for j in range(n_pairs):
        pair = qt_ref[j * LANES:(j + 1) * LANES, :]
        zero = jnp.zeros_like(pair)
        qm_ref[2 * j] = jnp.where(top, pair, zero)
        qm_ref[2 * j + 1] = jnp.where(top, zero, pair)


def _idx_prompt_t_kernel(qit_ref, wit_ref, ki_ref, bias_ref, qm_ref, s_ref, *, tq, tk, nsel, seq):
    qi = pl.program_id(1)
    q0 = qi * tq
    n_chunks = seq // tk
    nc = (q0 + tq + tk - 1) // tk
    _masked_pairs_t(qit_ref, qm_ref, N_IDX_HEADS // 2, tq)
    wt = wit_ref[...]
    col = q0 + lax.broadcasted_iota(I32, (tk, tq), 1)
    row0 = lax.broadcasted_iota(I32, (tk, tq), 0)

    def score_body(c, carry):
        start = pl.multiple_of(c * tk, tk)
        kblk = ki_ref[0, pl.ds(start, tk), :]
        acc = jnp.zeros((tk, tq), F32)
        for h in range(N_IDX_HEADS):
            s = jnp.dot(kblk, qm_ref[h], preferred_element_type=F32)
            acc = acc + jnp.maximum(s, 0.0) * wt[h:h + 1, :]
        acc = jnp.where(row0 + c * tk <= col, acc, -jnp.inf)
        s_ref[c] = _sort_key(acc)
        return carry

    lax.fori_loop(0, nc, score_body, 0)
    thr, cut = _select_topk_t(lambda c: s_ref[c], nc, tk, tq, nsel, int(math.log2(seq)) + 1)
    thr_t = jnp.tile(thr, (tk // SUBLANES, 1))
    cut_t = jnp.tile(cut, (tk // SUBLANES, 1))

    def write_body(c, carry):
        k = s_ref[c]
        sel = (k > thr_t) | ((k == thr_t) & (row0 + c * tk <= cut_t))
        start = pl.multiple_of(c * tk, tk)
        bias_ref[0, 0, pl.ds(start, tk), :] = jnp.where(sel, 0.0, NEG).astype(BF16)
        return carry

    lax.fori_loop(0, nc, write_body, 0)

    def fill_body(c, carry):
        start = pl.multiple_of(c * tk, tk)
        bias_ref[0, 0, pl.ds(start, tk), :] = jnp.full((tk, tq), NEG, BF16)
        return carry

    lax.fori_loop(nc, n_chunks, fill_body, 0)


def _idx_prompt_t(qit, wit, ki16, b, nsel, tq, tk):
    seq = ki16.shape[1]
    nq = seq // tq
    return pl.pallas_call(
        functools.partial(_idx_prompt_t_kernel, tq=tq, tk=tk, nsel=nsel, seq=seq),
        grid=(b, nq),
        in_specs=[pl.BlockSpec((WIDTH_I, tq), lambda bi, qi: (0, bi * nq + qi)),
                  pl.BlockSpec((N_IDX_HEADS, tq), lambda bi, qi: (0, bi * nq + qi)),
                  pl.BlockSpec((1, seq, LANES), lambda bi, qi: (bi, 0, 0))],
        out_specs=pl.BlockSpec((1, 1, seq, tq), lambda bi, qi: (bi, qi, 0, 0)),
        out_shape=jax.ShapeDtypeStruct((b, nq, seq, tq), BF16),
        scratch_shapes=[pltpu.VMEM((N_IDX_HEADS, LANES, tq), BF16),
                        pltpu.VMEM((seq // tk, tk, tq), I32)],
        compiler_params=_cparams(2),
        name="idx_prompt",
    )(qit, wit, ki16)


def _flash_t_kernel(*refs, variant, n_pairs, tq, tk, sub):
    it = iter(refs)
    k_ref, qt_ref, vt_ref = next(it), next(it), next(it)
    bias_ref = next(it) if variant == "dsa" else None
    cqt_ref, ck_ref = (next(it), next(it)) if variant == "fox" else (None, None)
    lam_ref = next(it) if variant == "diff" else None
    o_ref = next(it)
    qm_ref, m_ref, l_ref, acc_ref = next(it), next(it), next(it), next(it)
    bias32_ref = next(it) if variant == "dsa" else None

    qi = pl.program_id(1)
    kc = pl.program_id(2)
    nk = pl.num_programs(2)
    last = ((qi + 1) * tq - 1) // tk
    n_units = 2 * n_pairs
    vr = acc_ref.shape[1]

    @pl.when(kc == 0)
    def _init():
        _masked_pairs_t(qt_ref, qm_ref, n_pairs, tq)
        m_ref[...] = jnp.full(m_ref.shape, NEG, F32)
        l_ref[...] = jnp.zeros(l_ref.shape, F32)
        acc_ref[...] = jnp.zeros(acc_ref.shape, F32)

    def compute(masked):
        if variant == "dsa":
            bias32_ref[...] = bias_ref[0, 0].astype(F32)
            masked = False
        if masked:
            col = qi * tq + lax.broadcasted_iota(I32, (sub, tq), 1)
            row0 = kc * tk + lax.broadcasted_iota(I32, (sub, tq), 0)
        ones_rows = jnp.ones((2 * SUBLANES, sub), BF16)
        if variant == "fox":
            ck_all = ck_ref[0] * LOG2E
        for g0 in range(0, n_units, FLASH_GROUP):
            units = list(range(g0, min(g0 + FLASH_GROUP, n_units)))
            state = [[m_ref[u], l_ref[u], acc_ref[u]] for u in units]
            for r0 in range(0, tk, sub):
                ss = [jnp.dot(k_ref[0, r0:r0 + sub, (u // 2) * LANES:(u // 2 + 1) * LANES], qm_ref[u],
                              preferred_element_type=F32) for u in units]
                ps = []
                for e, u in enumerate(units):
                    s = ss[e]
                    if variant == "dsa":
                        s = s + bias32_ref[r0:r0 + sub, :]
                    if variant == "fox":
                        s = s + (cqt_ref[0, u:u + 1, :] * LOG2E
                                 - jnp.broadcast_to(ck_all[r0:r0 + sub, u:u + 1], (sub, tq)))
                    if masked:
                        s = jnp.where(row0 + r0 <= col, s, NEG)
                    m_prev, l_prev, acc = state[e]
                    m_new = jnp.maximum(m_prev, jnp.max(s, axis=0, keepdims=True))
                    alpha = jnp.exp2(m_prev - m_new)
                    p = jnp.exp2(s - jnp.tile(m_new, (sub // SUBLANES, 1)))
                    state[e][0] = m_new
                    state[e][1] = alpha * l_prev
                    state[e][2] = jnp.tile(alpha, (vr // SUBLANES, 1)) * acc
                    ps.append(p.astype(BF16))
                for e, u in enumerate(units):
                    v0 = (u // 2) * vr if variant == "diff" else u * vr
                    v_aug = jnp.concatenate([vt_ref[v0:v0 + vr, r0:r0 + sub], ones_rows], axis=0)
                    pv = jnp.dot(v_aug, ps[e], preferred_element_type=F32)
                    state[e][1] = state[e][1] + pv[vr:vr + SUBLANES, :]
                    state[e][2] = state[e][2] + pv[0:vr, :]
            for e, u in enumerate(units):
                m_ref[u], l_ref[u], acc_ref[u] = state[e]

    needs_mask = (kc + 1) * tk - 1 > qi * tq

    @pl.when((kc <= last) & needs_mask)
    def _diag():
        compute(True)

    @pl.when((kc <= last) & jnp.logical_not(needs_mask))
    def _full():
        compute(False)

    @pl.when(kc == nk - 1)
    def _fin():
        def norm(u):
            return acc_ref[u] / jnp.tile(l_ref[u], (vr // SUBLANES, 1))
        for j in range(n_pairs):
            if variant == "diff":
                ot = norm(2 * j) - lam_ref[0, 0] * norm(2 * j + 1)
            else:
                ot = jnp.concatenate([norm(2 * j), norm(2 * j + 1)], axis=0)
            o_ref[0, :, j * LANES:(j + 1) * LANES] = ot.T


def _flash_t(variant, k16, qt, vt, tq, tk, bias=None, cqt=None, ck=None, lam=None):
    b, seq, w = k16.shape
    n_pairs = w // LANES
    n_units = 2 * n_pairs
    nq, nk = seq // tq, seq // tk
    vr = LANES if variant == "diff" else HEAD_DIM
    sub = min(tk, FLASH_SUB)

    def last_of(qi):
        return ((qi + 1) * tq - 1) // tk

    def kc_of(qi, kc):
        return jnp.minimum(kc, last_of(qi))

    in_specs = [pl.BlockSpec((1, tk, w), lambda bi, qi, kc: (bi, kc_of(qi, kc), 0)),
                pl.BlockSpec((w, tq), lambda bi, qi, kc: (0, bi * nq + qi)),
                pl.BlockSpec((vt.shape[0], tk), lambda bi, qi, kc: (0, bi * nk + kc_of(qi, kc)))]
    args = [k16, qt, vt]
    scratch = [pltpu.VMEM((n_units, LANES, tq), BF16), pltpu.VMEM((n_units, SUBLANES, tq), F32),
               pltpu.VMEM((n_units, SUBLANES, tq), F32), pltpu.VMEM((n_units, vr, tq), F32)]
    if variant == "dsa":
        in_specs.append(pl.BlockSpec((1, 1, tk, tq), lambda bi, qi, kc: (bi, qi, kc_of(qi, kc), 0)))
        args.append(bias)
        scratch.append(pltpu.VMEM((tk, tq), F32))
    if variant == "fox":
        in_specs.append(pl.BlockSpec((1, n_units, tq), lambda bi, qi, kc: (bi, 0, qi)))
        in_specs.append(pl.BlockSpec((1, tk, n_units), lambda bi, qi, kc: (bi, kc_of(qi, kc), 0)))
        args += [cqt, ck]
    if variant == "diff":
        in_specs.append(pl.BlockSpec(memory_space=pltpu.SMEM))
        args.append(lam)
    return pl.pallas_call(
        functools.partial(_flash_t_kernel, variant=variant, n_pairs=n_pairs, tq=tq, tk=tk, sub=sub),
        grid=(b, nq, nk),
        in_specs=in_specs,
        out_specs=pl.BlockSpec((1, tq, w), lambda bi, qi, kc: (bi, qi, 0)),
        out_shape=jax.ShapeDtypeStruct((b, seq, w), F32),
        scratch_shapes=scratch,
        compiler_params=_cparams(3),
        name="flash_" + variant,
    )(*args)


PAGES_PER_STEP = 8


def _pages_per_step(n_pages):
    g = PAGES_PER_STEP
    while n_pages % g:
        g //= 2
    return g


def _upper_ones():
    r = lax.broadcasted_iota(I32, (LANES, LANES), 0)
    c = lax.broadcasted_iota(I32, (LANES, LANES), 1)
    return jnp.where(r <= c, 1.0, 0.0).astype(F32)


def _cumsum_kernel2(*refs, g):
    x_refs = refs[1:1 + g]
    xn_ref, c_ref, cn_ref, carry_ref = refs[1 + g:5 + g]
    p = pl.program_id(1)
    upper = _upper_ones()

    @pl.when(p == 0)
    def _():
        carry_ref[...] = jnp.zeros(carry_ref.shape, F32)

    total = carry_ref[...]
    for j in range(g):
        cs = jnp.dot(x_refs[j][0], upper, precision=HIGHEST, preferred_element_type=F32) + total
        c_ref[0, :, j * LANES:(j + 1) * LANES] = cs
        total = jnp.broadcast_to(cs[:, LANES - 1:LANES], cs.shape)
    carry_ref[...] = total

    @pl.when(p == pl.num_programs(1) - 1)
    def _():
        cn_ref[0] = jnp.dot(xn_ref[0], upper, precision=HIGHEST, preferred_element_type=F32) + total


def _cumsum_pages2(page_table, pool_t, new_t):
    b, n_pages = page_table.shape
    h = pool_t.shape[1]
    g = _pages_per_step(n_pages)
    pool_specs = [pl.BlockSpec((1, h, LANES), lambda bi, p, pt, j=j: (pt[bi, p * g + j], 0, 0)) for j in range(g)]
    grid_spec = pltpu.PrefetchScalarGridSpec(
        num_scalar_prefetch=1,
        grid=(b, n_pages // g),
        in_specs=pool_specs + [pl.BlockSpec((1, h, LANES), lambda bi, p, pt: (bi, 0, 0))],
        out_specs=[pl.BlockSpec((1, h, g * LANES), lambda bi, p, pt: (bi, 0, p)),
                   pl.BlockSpec((1, h, LANES), lambda bi, p, pt: (bi, 0, 0))],
        scratch_shapes=[pltpu.VMEM((h, LANES), F32)],
    )
    return pl.pallas_call(
        functools.partial(_cumsum_kernel2, g=g),
        grid_spec=grid_spec,
        out_shape=(jax.ShapeDtypeStruct((b, h, n_pages * LANES), F32), jax.ShapeDtypeStruct((b, h, LANES), F32)),
        compiler_params=_cparams(2),
        name="cumsum_pages",
    )(page_table, *([pool_t] * g), new_t)


def _idx_decode_kernel2(*refs, g, n_pages, n_new, nsel, page):
    q_ref, w_ref = refs[1:3]
    k_refs = refs[3:3 + g]
    knew_ref, bias_ref, biasn_ref, s_ref = refs[3 + g:7 + g]
    p = pl.program_id(1)
    n_steps = n_pages // g
    rows = n_new

    def scores(kt):
        s = jnp.dot(q_ref[0], kt.astype(BF16), preferred_element_type=F32)
        t = jnp.maximum(s, 0.0) * w_ref[0]
        acc = t[0:rows]
        for h in range(1, N_IDX_HEADS):
            acc = acc + t[h * rows:(h + 1) * rows]
        return acc

    @pl.when(p < n_steps)
    def _past():
        for j in range(g):
            s_ref[p * g + j] = _sort_key(scores(k_refs[j][0]))

    @pl.when(p == n_steps)
    def _new():
        sc = scores(knew_ref[0])
        i = lax.broadcasted_iota(I32, (rows, LANES), 0)
        lane = lax.broadcasted_iota(I32, (rows, LANES), 1)
        s_ref[n_pages] = _sort_key(jnp.where(lane <= i, sc, -jnp.inf))

        def idx_of(c, grp, nr):
            return lane + c * page

        n_keys = (n_pages + 1) * page
        thr, cut = _select_topk(lambda c, r0, nr: s_ref[c], n_pages + 1, 1, rows, nsel,
                                int(math.log2(n_keys)) + 1, idx_of)

        def selected(c):
            k = s_ref[c]
            sel = (k > thr) | ((k == thr) & (lane + c * page <= cut))
            return jnp.where(sel, 0.0, NEG)

        def write_body(c, carry):
            bias_ref[0, c] = selected(c)
            return carry

        lax.fori_loop(0, n_pages, write_body, 0)
        biasn_ref[0] = selected(n_pages)


def _idx_decode2(page_table, q_st, w_st, kt_pool, kt_new, nsel):
    b, n_pages = page_table.shape
    page = kt_pool.shape[2]
    n_new = q_st.shape[1] // N_IDX_HEADS
    g = _pages_per_step(n_pages)
    n_steps = n_pages // g

    def req_map(bi, p, pt):
        return (bi, 0, 0)

    pool_specs = [pl.BlockSpec((1, IDX_DIM, page),
                               lambda bi, p, pt, j=j: (pt[bi, jnp.minimum(p, n_steps - 1) * g + j], 0, 0))
                  for j in range(g)]
    grid_spec = pltpu.PrefetchScalarGridSpec(
        num_scalar_prefetch=1,
        grid=(b, n_steps + 1),
        in_specs=[pl.BlockSpec((1,) + q_st.shape[1:], req_map), pl.BlockSpec((1,) + w_st.shape[1:], req_map)]
        + pool_specs + [pl.BlockSpec((1, IDX_DIM, page), req_map)],
        out_specs=[pl.BlockSpec((1, n_pages, n_new, LANES), lambda bi, p, pt: (bi, 0, 0, 0)),
                   pl.BlockSpec((1, n_new, LANES), req_map)],
        scratch_shapes=[pltpu.VMEM((n_pages + 1, n_new, LANES), I32)],
    )
    return pl.pallas_call(
        functools.partial(_idx_decode_kernel2, g=g, n_pages=n_pages, n_new=n_new, nsel=nsel, page=page),
        grid_spec=grid_spec,
        out_shape=(jax.ShapeDtypeStruct((b, n_pages, n_new, LANES), F32),
                   jax.ShapeDtypeStruct((b, n_new, LANES), F32)),
        compiler_params=_cparams(2),
        name="idx_decode",
    )(page_table, q_st, w_st, *([kt_pool] * g), kt_new)


def _attn_decode_kernel2(*refs, variant, g, n_units, n_new, n_pages):
    it = iter(refs)
    next(it)
    q_ref = next(it)
    k_refs = [next(it) for _ in range(g)]
    v_refs = [next(it) for _ in range(g)]
    knew_ref, vnew_ref = next(it), next(it)
    bias_ref, biasn_ref = (next(it), next(it)) if variant == "dsa" else (None, None)
    cq_ref, ck_ref, cn_ref = (next(it), next(it), next(it)) if variant == "fox" else (None, None, None)
    lam_ref = next(it) if variant == "diff" else None
    o_ref = next(it)
    qbd_ref, m_ref, l_ref, acc_ref = next(it), next(it), next(it), next(it)

    p = pl.program_id(1)
    n_steps = n_pages // g
    rows = n_units * n_new
    w = q_ref.shape[-1]
    wa = acc_ref.shape[-1]

    @pl.when(p == 0)
    def _init():
        q = q_ref[0].astype(F32)
        qt = jnp.concatenate([q] * n_units, axis=0)
        r = _div_pow2(lax.broadcasted_iota(I32, (rows, w), 0), n_new)
        c = _div_pow2(lax.broadcasted_iota(I32, (rows, w), 1), HEAD_DIM)
        qbd_ref[...] = jnp.where(r == c, qt, 0.0).astype(BF16)
        m_ref[...] = jnp.full(m_ref.shape, NEG, F32)
        l_ref[...] = jnp.zeros(l_ref.shape, F32)
        acc_ref[...] = jnp.zeros(acc_ref.shape, F32)

    def expand_rows(x8):
        return jnp.concatenate([x8] * n_units, axis=0)

    def expand_units(xu):
        return jnp.concatenate([jnp.broadcast_to(xu[u:u + 1, :], (n_new, xu.shape[1])) for u in range(n_units)],
                               axis=0)

    def v_head(v_ref, h, is_new):
        if is_new or len(v_ref.shape) == 3:
            return v_ref[0, :, h * LANES:(h + 1) * LANES]
        return v_ref[0, :, h, :]

    def step(kts, vs, bias, ck, is_new):
        ng = len(kts)
        parts = [jnp.dot(qbd_ref[...], kts[j][0].astype(BF16), preferred_element_type=F32) for j in range(ng)]
        s = parts[0] if ng == 1 else jnp.concatenate(parts, axis=1)
        if variant == "dsa":
            s = s + bias
        if variant == "fox":
            s = s + (jnp.tile(cq_ref[0], (1, ng)) - expand_units(ck))
        if is_new:
            i = expand_rows(lax.broadcasted_iota(I32, (n_new, LANES), 0))
            lane = lax.broadcasted_iota(I32, (rows, LANES), 1)
            s = jnp.where(lane <= i, s, NEG)
        m_prev = m_ref[...]
        m_new = jnp.maximum(m_prev, jnp.max(s, axis=-1, keepdims=True))
        alpha = jnp.exp(m_prev - m_new)
        pr = jnp.exp(s - jnp.tile(m_new, (1, ng)))
        l_ref[...] = alpha * l_ref[...] + jnp.sum(pr, axis=-1, keepdims=True)
        m_ref[...] = m_new
        p16 = pr.astype(BF16)
        acc = jnp.tile(alpha, (1, wa // LANES)) * acc_ref[...]
        for j in range(ng):
            pj = p16[:, j * LANES:(j + 1) * LANES]
            if variant == "diff":
                hr = 2 * n_new
                pv = jnp.concatenate(
                    [jnp.dot(pj[h * hr:(h + 1) * hr], v_head(vs[j], h, is_new).astype(BF16),
                             preferred_element_type=F32) for h in range(n_units // 2)], axis=0)
            else:
                pv = lax.dot_general(pj, vs[j][0].astype(BF16), NT_DIMS, preferred_element_type=F32)
            acc = acc + pv
        acc_ref[...] = acc

    @pl.when(p < n_steps)
    def _past():
        bias = ck = None
        if variant == "dsa":
            bias = jnp.concatenate([expand_rows(bias_ref[0, j]) for j in range(g)], axis=1)
        if variant == "fox":
            ck = ck_ref[0]
        step(k_refs, v_refs, bias, ck, False)

    @pl.when(p == n_steps)
    def _new():
        bias = expand_rows(biasn_ref[0]) if variant == "dsa" else None
        ck = cn_ref[0] if variant == "fox" else None
        step([knew_ref], [vnew_ref], bias, ck, True)
        accn = acc_ref[...] / jnp.tile(l_ref[...], (1, wa // LANES))
        if variant == "diff":
            lam = lam_ref[0, 0]
            o_ref[0] = jnp.concatenate(
                [accn[(2 * h) * n_new:(2 * h + 1) * n_new] - lam * accn[(2 * h + 1) * n_new:(2 * h + 2) * n_new]
                 for h in range(n_units // 2)], axis=1)
        else:
            cu = _div_pow2(lax.broadcasted_iota(I32, (n_new, wa), 1), HEAD_DIM)
            out = jnp.zeros((n_new, wa), F32)
            for u in range(n_units):
                out = out + jnp.where(cu == u, accn[u * n_new:(u + 1) * n_new], 0.0)
            o_ref[0] = out


def _attn_decode2(variant, page_table, q16, kt_pool, v_pool, kt_new, v_new,
                  bias=None, bias_new=None, cq=None, ck=None, cn=None, lam=None):
    b, n_pages = page_table.shape
    page = kt_pool.shape[2]
    n_new, w = q16.shape[1], q16.shape[2]
    n_units = w // HEAD_DIM
    rows = n_units * n_new
    g = _pages_per_step(n_pages)
    n_steps = n_pages // g
    wa = LANES if variant == "diff" else w

    def req_map(bi, p, pt):
        return (bi, 0, 0)

    def pool_spec(arr, j):
        nd = arr.ndim
        return pl.BlockSpec((1,) + arr.shape[1:],
                            lambda bi, p, pt: (pt[bi, jnp.minimum(p, n_steps - 1) * g + j],) + (0,) * (nd - 1))

    in_specs = ([pl.BlockSpec((1, n_new, w), req_map)]
                + [pool_spec(kt_pool, j) for j in range(g)] + [pool_spec(v_pool, j) for j in range(g)]
                + [pl.BlockSpec((1,) + kt_new.shape[1:], req_map), pl.BlockSpec((1,) + v_new.shape[1:], req_map)])
    args = [q16] + [kt_pool] * g + [v_pool] * g + [kt_new, v_new]
    if variant == "dsa":
        in_specs.append(pl.BlockSpec((1, g, n_new, LANES),
                                     lambda bi, p, pt: (bi, jnp.minimum(p, n_steps - 1), 0, 0)))
        in_specs.append(pl.BlockSpec((1, n_new, LANES), req_map))
        args += [bias, bias_new]
    if variant == "fox":
        in_specs.append(pl.BlockSpec((1, rows, LANES), req_map))
        in_specs.append(pl.BlockSpec((1, n_units, g * LANES), lambda bi, p, pt: (bi, 0, jnp.minimum(p, n_steps - 1))))
        in_specs.append(pl.BlockSpec((1, n_units, LANES), req_map))
        args += [cq, ck, cn]
    if variant == "diff":
        in_specs.append(pl.BlockSpec(memory_space=pltpu.SMEM))
        args.append(lam)
    wo = v_new.shape[2] if variant == "diff" else w
    grid_spec = pltpu.PrefetchScalarGridSpec(
        num_scalar_prefetch=1,
        grid=(b, n_steps + 1),
        in_specs=in_specs,
        out_specs=pl.BlockSpec((1, n_new, wo), req_map),
        scratch_shapes=[pltpu.VMEM((rows, w), BF16), pltpu.VMEM((rows, LANES), F32),
                        pltpu.VMEM((rows, LANES), F32), pltpu.VMEM((rows, wa), F32)],
    )
    return pl.pallas_call(
        functools.partial(_attn_decode_kernel2, variant=variant, g=g, n_units=n_units, n_new=n_new,
                          n_pages=n_pages),
        grid_spec=grid_spec,
        out_shape=jax.ShapeDtypeStruct((b, n_new, wo), F32),
        compiler_params=_cparams(2),
        name="attn_decode_" + variant,
    )(page_table, *args)


def _feature_major_pages(cache):
    n_pool, page = cache.shape[0], cache.shape[1]
    perm = (0,) + tuple(range(2, cache.ndim)) + (1,)
    return jnp.transpose(cache, perm).reshape(n_pool, -1, page)


def _feature_major_new(x3, page):
    xt = jnp.swapaxes(x3, 1, 2)
    return jnp.pad(xt, ((0, 0), (0, 0), (0, page - xt.shape[2])))


def _diff_lambda(p, lam_init):
    def e(a, c):
        return jnp.exp(jnp.sum(a.astype(F32) * c.astype(F32)))
    return (e(p["lam_q1"], p["lam_k1"]) - e(p["lam_q2"], p["lam_k2"]) + lam_init).reshape(1, 1).astype(F32)


def _pad_rows(x, rows):
    return jnp.pad(x, ((0, 0), (0, rows - x.shape[1]), (0, 0)))


def _tile_for(m, pref):
    t = min(m, pref)
    while m % t:
        t //= 2
    return t


def _even_prompt(x, layer, p, tiles):
    b, seq, d = x.shape
    tm, tq, tk = tiles
    pos = jnp.tile(jnp.arange(seq), b)
    (qa, ka, ka16, va, va16, ga, qi, ki, ki16, wi, qb, kb, kb16, vb, vb16, gb) = _proj0(
        x.reshape(b * seq, d), pos, p, tm)
    lam_init = 0.8 - 0.6 * math.exp(-0.3 * layer)
    lam = _diff_lambda(p, lam_init)
    nsel = min(TOPK_MAX, seq // 4)
    r3 = lambda t: t.reshape(b, seq, t.shape[-1])
    w_in = p["w_in"]
    col = lambda start, width: w_in[:, start:start + width]
    w5 = WIDTH_A
    o_i, o_b0 = 4 * w5, 5 * w5 + IDX_DIM + N_IDX_HEADS
    qat, vat, qit, qbt, vbt, wit = _proj_t(
        x.reshape(b * seq, d), p["norm"],
        [col(0, w5), col(2 * w5, w5), col(o_i, w5), col(o_b0, w5), col(o_b0 + 2 * w5, w5)],
        [(0, True, QK_SCALE * LOG2E), (None, False, 1.0), (None, True, 1.0), (1, True, QK_SCALE * LOG2E),
         (None, False, 1.0)],
        [p["qn_a"], p["qn_b"]], pos, col(o_i + w5 + IDX_DIM, N_IDX_HEADS), tm)
    bias = _idx_prompt_t(qit, wit, r3(ki16), b, nsel, tq, tk)
    o_a = _flash_t("dsa", r3(ka16), qat, vat, tq, tk, bias=bias)
    o_b = _flash_t("diff", r3(kb16), qbt, vbt, tq, tk, lam=lam)
    y = _out0(x.reshape(b * seq, d), o_a.reshape(b * seq, -1), ga, o_b.reshape(b * seq, -1), gb, p, lam_init, tm)
    new = (ka.reshape(b, seq, N_HEADS_A, HEAD_DIM), va.reshape(b, seq, N_HEADS_A, HEAD_DIM),
           ki.reshape(b, seq, IDX_DIM), kb.reshape(b, seq, N_HEADS_B, 2, HEAD_DIM),
           vb.reshape(b, seq, N_HEADS_B, 2 * HEAD_DIM))
    return y.reshape(b, seq, d), new


def _even_sample(x, layer, caches, page_table, p):
    cache_k_a, cache_v_a, cache_k_i, cache_k_b, cache_v_b = caches
    b, n_new, d = x.shape
    n_pages = page_table.shape[1]
    n_pool, page = cache_k_a.shape[0], cache_k_a.shape[1]
    past = n_pages * page
    pos = jnp.tile(past + jnp.arange(n_new), b)
    m = b * n_new
    (qa, ka, _ka16, va, _va16, ga, qi, ki, _ki16, wi, qb, kb, _kb16, vb, _vb16, gb) = _proj0(
        x.reshape(m, d), pos, p, _tile_for(m, 256))
    lam_init = 0.8 - 0.6 * math.exp(-0.3 * layer)
    lam = _diff_lambda(p, lam_init)
    nsel = min(TOPK_MAX, (past + n_new) // 4)
    r3 = lambda t: t.reshape(b, n_new, t.shape[-1])
    q_st = jnp.swapaxes(qi.reshape(b, n_new, N_IDX_HEADS, IDX_DIM), 1, 2).reshape(b, N_IDX_HEADS * n_new, IDX_DIM)
    w_st = jnp.swapaxes(wi.reshape(b, n_new, N_IDX_HEADS), 1, 2).reshape(b, N_IDX_HEADS * n_new, 1)
    w_st = jnp.broadcast_to(w_st, (b, N_IDX_HEADS * n_new, LANES))
    bias, bias_new = _idx_decode2(page_table, q_st, w_st, _feature_major_pages(cache_k_i),
                                  _feature_major_new(r3(ki), page), nsel)
    o_a = _attn_decode2("dsa", page_table, r3(qa), _feature_major_pages(cache_k_a), _feature_major_pages(cache_v_a),
                        _feature_major_new(r3(ka), page), _feature_major_new(r3(va), page),
                        bias=bias, bias_new=bias_new)
    o_b = _attn_decode2("diff", page_table, r3(qb), _feature_major_pages(cache_k_b), cache_v_b,
                        _feature_major_new(r3(kb), page), _pad_rows(r3(vb), page), lam=lam)
    y = _out0(x.reshape(m, d), o_a.reshape(m, -1), ga, o_b.reshape(m, -1), gb, p, lam_init, _tile_for(m, 256))
    new = (ka.reshape(b, n_new, N_HEADS_A, HEAD_DIM), va.reshape(b, n_new, N_HEADS_A, HEAD_DIM),
           ki.reshape(b, n_new, IDX_DIM), kb.reshape(b, n_new, N_HEADS_B, 2, HEAD_DIM),
           vb.reshape(b, n_new, N_HEADS_B, 2 * HEAD_DIM))
    return y.reshape(b, n_new, d), new


def _odd_prompt(x, p, tiles):
    b, seq, d = x.shape
    tm, tq, tk = tiles
    q16, k, k16, v, v16, gate, logf = _proj1(x.reshape(b * seq, d), p, tm)
    r3 = lambda t: t.reshape(b, seq, t.shape[-1])
    n_blk = seq // LANES
    logf_t = jnp.swapaxes(logf.reshape(b * n_blk, LANES, N_HEADS_C), 1, 2)
    ident = jnp.arange(b * n_blk, dtype=I32).reshape(b, n_blk)
    c_t, _ = _cumsum_pages2(ident, logf_t, jnp.zeros((b, N_HEADS_C, LANES), F32))
    c_tok = jnp.swapaxes(c_t, 1, 2)
    w_in = p["w_in"]
    qt, vt = _proj_t(x.reshape(b * seq, d), p["norm"], [w_in[:, 0:WIDTH_C], w_in[:, 2 * WIDTH_C:3 * WIDTH_C]],
                     [(0, False, QK_SCALE * LOG2E), (None, False, 1.0)], [p["qn"]], None, None, tm)
    o = _flash_t("fox", r3(k16), qt, vt, tq, tk, cqt=c_t, ck=c_tok)
    y = _out1(x.reshape(b * seq, d), o.reshape(b * seq, -1), gate, p, tm)
    new = (k.reshape(b, seq, N_HEADS_C, HEAD_DIM), v.reshape(b, seq, N_HEADS_C, HEAD_DIM),
           logf.reshape(b, seq, N_HEADS_C))
    return y.reshape(b, seq, d), new


def _odd_sample(x, caches, page_table, p):
    cache_k, cache_v, cache_logf = caches
    b, n_new, d = x.shape
    n_pool, page = cache_k.shape[0], cache_k.shape[1]
    m = b * n_new
    q16, k, _k16, v, _v16, gate, logf = _proj1(x.reshape(m, d), p, _tile_for(m, 256))
    r3 = lambda t: t.reshape(b, n_new, t.shape[-1])
    c_past_t, c_new_t = _cumsum_pages2(page_table, _feature_major_pages(cache_logf.astype(F32)),
                                       _feature_major_new(r3(logf), page))
    cq = jnp.broadcast_to(c_new_t[:, :, :n_new].reshape(b, N_HEADS_C * n_new, 1), (b, N_HEADS_C * n_new, LANES))
    o = _attn_decode2("fox", page_table, r3(q16), _feature_major_pages(cache_k), _feature_major_pages(cache_v),
                      _feature_major_new(r3(k), page), _feature_major_new(r3(v), page),
                      cq=cq, ck=c_past_t, cn=c_new_t)
    y = _out1(x.reshape(m, d), o.reshape(m, -1), gate, p, _tile_for(m, 256))
    new = (k.reshape(b, n_new, N_HEADS_C, HEAD_DIM), v.reshape(b, n_new, N_HEADS_C, HEAD_DIM),
           logf.reshape(b, n_new, N_HEADS_C))
    return y.reshape(b, n_new, d), new


def kernel(x_prompt, x_sample, cache_l0_k_a, cache_l0_v_a, cache_l0_k_idx, cache_l0_k_b, cache_l0_v_b,
           cache_l1_k_c, cache_l1_v_c, cache_l1_logf_c, page_table,
           l0_norm, l0_w_in, l0_qn_a, l0_kn_a, l0_kn_idx, l0_qn_b, l0_kn_b,
           l0_lam_q1, l0_lam_k1, l0_lam_q2, l0_lam_k2, l0_subln_b, l0_w_out,
           l1_norm, l1_w_in, l1_b_f, l1_qn, l1_kn, l1_w_out):
    p0 = dict(norm=l0_norm, w_in=l0_w_in, qn_a=l0_qn_a, kn_a=l0_kn_a, kn_idx=l0_kn_idx, qn_b=l0_qn_b, kn_b=l0_kn_b,
              lam_q1=l0_lam_q1, lam_k1=l0_lam_k1, lam_q2=l0_lam_q2, lam_k2=l0_lam_k2, subln_b=l0_subln_b,
              w_out=l0_w_out)
    p1 = dict(norm=l1_norm, w_in=l1_w_in, b_f=l1_b_f, qn=l1_qn, kn=l1_kn, w_out=l1_w_out)
    b, seq, _ = x_prompt.shape
    tiles = (_tile_for(b * seq, ROW_TILE), _tile_for(seq, Q_TILE), _tile_for(seq, K_TILE))
    page_table = page_table.astype(I32)
    xp, sp0 = _even_prompt(x_prompt, 0, p0, tiles)
    xs, ss0 = _even_sample(x_sample, 0, (cache_l0_k_a, cache_l0_v_a, cache_l0_k_idx, cache_l0_k_b, cache_l0_v_b),
                           page_table, p0)
    xp, sp1 = _odd_prompt(xp, p1, tiles)
    xs, ss1 = _odd_sample(xs, (cache_l1_k_c, cache_l1_v_c, cache_l1_logf_c), page_table, p1)
    (p_k_a, p_v_a, p_k_idx, p_k_b, p_v_b), (p_k_c, p_v_c, p_logf_c) = sp0, sp1
    (s_k_a, s_v_a, s_k_idx, s_k_b, s_v_b), (s_k_c, s_v_c, s_logf_c) = ss0, ss1
    return (xp, xs, p_k_a, s_k_a, p_v_a, s_v_a, p_k_idx, s_k_idx, p_k_b, s_k_b, p_v_b, s_v_b,
            p_k_c, s_k_c, p_v_c, s_v_c, p_logf_c, s_logf_c)
```

```python
import functools
import math

import jax
import jax.numpy as jnp
from jax import lax
from jax.experimental import pallas as pl
from jax.experimental.pallas import tpu as pltpu

F32 = jnp.float32
BF16 = jnp.bfloat16
I32 = jnp.int32

HEAD_DIM = 64
ROPE_THETA = 500000.0
N_HEADS_A = 8
N_IDX_HEADS = 8
IDX_DIM = 64
TOPK_MAX = 256
N_HEADS_B = 4
N_HEADS_C = 16
EPS = 1e-6
WIDTH_A = N_HEADS_A * HEAD_DIM
WIDTH_B = N_HEADS_B * 2 * HEAD_DIM
WIDTH_C = N_HEADS_C * HEAD_DIM
WIDTH_I = N_IDX_HEADS * IDX_DIM
QK_SCALE = HEAD_DIM ** -0.5
LOG2E = 1.4426950408889634

LANES = 128
SUBLANES = 8
MXU_DIM = 256
VMEM_LIMIT = 56 * 1024 * 1024
ROW_TILE = 256
Q_TILE = 512
K_TILE = 512
COUNT_ROWS = 64
COUNT_PARTS = 8
FLASH_SUB = 512
FLASH_GROUP = 8

NEG = -1e30
HIGHEST = lax.Precision.HIGHEST
NT_DIMS = (((1,), (1,)), ((), ()))

KEY_NEG_INF = -2139095041 - 0


def _cparams(n_axes):
    return pltpu.CompilerParams(
        dimension_semantics=("arbitrary",) * n_axes, vmem_limit_bytes=VMEM_LIMIT)


def _div_pow2(x, d):
    assert d & (d - 1) == 0
    return lax.shift_right_logical(x, jnp.int32(d.bit_length() - 1))


def _sort_key(x):
    bits = pltpu.bitcast(x, I32)
    return bits ^ ((bits >> 31) & jnp.int32(0x7FFFFFFF))


def _rms_rows(x, g):
    ms = jnp.mean(x * x, axis=-1, keepdims=True)
    return x * lax.rsqrt(ms + EPS) * g


def _split2(x):
    hi = x.astype(BF16)
    return hi, (x - hi.astype(F32)).astype(BF16)


def _split3(x):
    hi = x.astype(BF16)
    r = x - hi.astype(F32)
    mid = r.astype(BF16)
    return hi, mid, (r - mid.astype(F32)).astype(BF16)


def _head_norm(h, bd, gain):
    w = h.shape[-1]
    hi, lo = _split2(h * h)
    bd16 = bd.astype(BF16)
    if w >= MXU_DIM:
        cols = [jnp.dot(hi[:, c * MXU_DIM:(c + 1) * MXU_DIM], bd16, preferred_element_type=F32)
                + jnp.dot(lo[:, c * MXU_DIM:(c + 1) * MXU_DIM], bd16, preferred_element_type=F32)
                for c in range(w // MXU_DIM)]
        ms = cols[0] if len(cols) == 1 else jnp.concatenate(cols, axis=-1)
    else:
        ms = (jnp.dot(hi, bd16[:w, :w], preferred_element_type=F32)
              + jnp.dot(lo, bd16[:w, :w], preferred_element_type=F32))
    return h * lax.rsqrt(ms + EPS) * gain


def _rope(y, rope_ref):
    c = rope_ref[:, 0:LANES]
    s_lo = rope_ref[:, LANES:2 * LANES]
    s_hi = rope_ref[:, 2 * LANES:3 * LANES]
    outs = []
    for j in range(y.shape[-1] // LANES):
        yc = y[:, j * LANES:(j + 1) * LANES]
        outs.append(yc * c + pltpu.roll(yc, LANES - 8, 1) * s_lo + pltpu.roll(yc, 8, 1) * s_hi)
    return outs[0] if len(outs) == 1 else jnp.concatenate(outs, axis=-1)


def _proj0_kernel(x_ref, g_ref, w_ref, ws_ref, rope_ref, gains_ref, gki_ref, bd_ref,
                  qa_ref, ka_ref, ka16_ref, va_ref, va16_ref, ga_ref,
                  qi_ref, ki_ref, ki16_ref, wi_ref,
                  qb_ref, kb_ref, kb16_ref, vb_ref, vb16_ref, gb_ref):
    xb = _rms_rows(x_ref[...], g_ref[...]).astype(BF16)
    bd = bd_ref[...]
    w512 = WIDTH_A

    def piece(j):
        return jnp.dot(xb, w_ref[:, j * w512:(j + 1) * w512], preferred_element_type=F32)

    q_a = _rope(_head_norm(piece(0), bd, gains_ref[0:1, :]), rope_ref)
    qa_ref[...] = (q_a * QK_SCALE).astype(BF16)
    k_a = _rope(_head_norm(piece(1), bd, gains_ref[1:2, :]), rope_ref)
    ka_ref[...] = k_a
    ka16_ref[...] = k_a.astype(BF16)
    v_a = piece(2)
    va_ref[...] = v_a
    va16_ref[...] = v_a.astype(BF16)
    ga_ref[...] = piece(3)
    qi_ref[...] = _rope(piece(4), rope_ref).astype(BF16)
    q_b = _rope(_head_norm(piece(5), bd, gains_ref[2:3, :]), rope_ref)
    qb_ref[...] = (q_b * QK_SCALE).astype(BF16)
    k_b = _rope(_head_norm(piece(6), bd, gains_ref[3:4, :]), rope_ref)
    kb_ref[...] = k_b
    kb16_ref[...] = k_b.astype(BF16)
    v_b = piece(7)
    vb_ref[...] = v_b
    vb16_ref[...] = v_b.astype(BF16)
    gb_ref[...] = piece(8)
    hs = jnp.dot(xb, ws_ref[...], preferred_element_type=F32)
    k_i = _rope(_head_norm(hs[:, 0:LANES], bd, gki_ref[...]), rope_ref)
    ki_ref[...] = k_i[:, 0:IDX_DIM]
    ki16_ref[...] = k_i.astype(BF16)
    wi_ref[...] = hs[:, LANES:LANES + N_IDX_HEADS] * (WIDTH_I ** -0.5)


def _proj1_kernel(x_ref, g_ref, w_ref, wf_ref, bf_ref, gains_ref, bd_ref,
                  q_ref, k_ref, k16_ref, v_ref, v16_ref, gate_ref, logf_ref):
    xb = _rms_rows(x_ref[...], g_ref[...]).astype(BF16)
    bd = bd_ref[...]
    wc = WIDTH_C

    def piece(j):
        return jnp.dot(xb, w_ref[:, j * wc:(j + 1) * wc], preferred_element_type=F32)

    q_ref[...] = (_head_norm(piece(0), bd, gains_ref[0:1, :]) * QK_SCALE).astype(BF16)
    k = _head_norm(piece(1), bd, gains_ref[1:2, :])
    k_ref[...] = k
    k16_ref[...] = k.astype(BF16)
    v = piece(2)
    v_ref[...] = v
    v16_ref[...] = v.astype(BF16)
    gate_ref[...] = piece(3)
    f = jnp.dot(xb, wf_ref[...], preferred_element_type=F32)[:, 0:N_HEADS_C] + bf_ref[...]
    logf_ref[...] = jnp.minimum(f, 0.0) - jnp.log1p(jnp.exp(-jnp.abs(f)))


def _row_spec(tm, w):
    return pl.BlockSpec((tm, w), lambda i: (i, 0))


def _const_spec(shape):
    return pl.BlockSpec(shape, lambda i: (0,) * len(shape))


def _block_diag_mean():
    r = lax.broadcasted_iota(I32, (MXU_DIM, MXU_DIM), 0) // HEAD_DIM
    c = lax.broadcasted_iota(I32, (MXU_DIM, MXU_DIM), 1) // HEAD_DIM
    return jnp.where(r == c, 1.0 / HEAD_DIM, 0.0).astype(F32)


def _rope_table(pos):
    rot = HEAD_DIM // 4
    half = rot // 2
    inv = jnp.power(F32(ROPE_THETA), -jnp.arange(half, dtype=F32) * (2.0 / rot))
    ang = pos.astype(F32)[:, None] * inv[None, :]
    cos, sin = jnp.cos(ang), jnp.sin(ang)
    m = pos.shape[0]
    ones = jnp.ones((m, HEAD_DIM - rot), F32)
    zeros = jnp.zeros((m, HEAD_DIM - rot), F32)
    zh = jnp.zeros((m, half), F32)
    c = jnp.concatenate([cos, cos, ones], axis=-1)
    s_lo = jnp.concatenate([-sin, zh, zeros], axis=-1)
    s_hi = jnp.concatenate([zh, sin, zeros], axis=-1)
    return jnp.concatenate([c, c, s_lo, s_lo, s_hi, s_hi], axis=-1)


def _tile_gain(g, w):
    return jnp.tile(g.astype(F32), w // g.shape[0])[None, :]


def _proj0(x2d, pos, p, tm):
    m, d = x2d.shape
    sizes = (WIDTH_A,) * 4 + (WIDTH_I, IDX_DIM, N_IDX_HEADS) + (WIDTH_B,) * 4
    offs = [0]
    for s in sizes:
        offs.append(offs[-1] + s)
    w_in = p["w_in"]
    cols = [w_in[:, offs[i]:offs[i + 1]] for i in range(len(sizes))]
    w_big = jnp.concatenate([cols[0], cols[1], cols[2], cols[3], cols[4], cols[7], cols[8], cols[9], cols[10]],
                            axis=1).astype(BF16)
    w_small = jnp.concatenate(
        [cols[5], cols[5], cols[6], jnp.zeros((d, LANES - N_IDX_HEADS), w_in.dtype)], axis=1).astype(BF16)
    gains = jnp.concatenate([_tile_gain(p["qn_a"], WIDTH_A), _tile_gain(p["kn_a"], WIDTH_A),
                             _tile_gain(p["qn_b"], WIDTH_B), _tile_gain(p["kn_b"], WIDTH_B)], axis=0)
    gki = _tile_gain(p["kn_idx"], LANES)
    rope = _rope_table(pos)
    w5 = WIDTH_A
    f32o = lambda w: jax.ShapeDtypeStruct((m, w), F32)
    b16o = lambda w: jax.ShapeDtypeStruct((m, w), BF16)
    out_shape = (b16o(w5), f32o(w5), b16o(w5), f32o(w5), b16o(w5), f32o(w5),
                 b16o(w5), f32o(IDX_DIM), b16o(LANES), f32o(N_IDX_HEADS),
                 b16o(w5), f32o(w5), b16o(w5), f32o(w5), b16o(w5), f32o(w5))
    out_specs = tuple(_row_spec(tm, s.shape[1]) for s in out_shape)
    return pl.pallas_call(
        _proj0_kernel,
        grid=(m // tm,),
        in_specs=[_row_spec(tm, d), _const_spec((1, d)), _const_spec(w_big.shape), _const_spec(w_small.shape),
                  _row_spec(tm, 3 * LANES), _const_spec(gains.shape), _const_spec(gki.shape),
                  _const_spec((MXU_DIM, MXU_DIM))],
        out_specs=out_specs,
        out_shape=out_shape,
        compiler_params=_cparams(1),
        name="proj0",
    )(x2d, p["norm"].astype(F32)[None, :], w_big, w_small, rope, gains, gki, _block_diag_mean())


def _proj1(x2d, p, tm):
    m, d = x2d.shape
    wc = WIDTH_C
    w_in = p["w_in"]
    w_big = w_in[:, :4 * wc].astype(BF16)
    w_f = jnp.concatenate([w_in[:, 4 * wc:], jnp.zeros((d, LANES - N_HEADS_C), w_in.dtype)], axis=1).astype(BF16)
    gains = jnp.concatenate([_tile_gain(p["qn"], wc), _tile_gain(p["kn"], wc)], axis=0)
    f32o = lambda w: jax.ShapeDtypeStruct((m, w), F32)
    b16o = lambda w: jax.ShapeDtypeStruct((m, w), BF16)
    out_shape = (b16o(wc), f32o(wc), b16o(wc), f32o(wc), b16o(wc), f32o(wc), f32o(N_HEADS_C))
    out_specs = tuple(_row_spec(tm, s.shape[1]) for s in out_shape)
    return pl.pallas_call(
        _proj1_kernel,
        grid=(m // tm,),
        in_specs=[_row_spec(tm, d), _const_spec((1, d)), _const_spec(w_big.shape), _const_spec(w_f.shape),
                  _const_spec((1, N_HEADS_C)), _const_spec(gains.shape), _const_spec((MXU_DIM, MXU_DIM))],
        out_specs=out_specs,
        out_shape=out_shape,
        compiler_params=_cparams(1),
        name="proj1",
    )(x2d, p["norm"].astype(F32)[None, :], w_big, w_f, p["b_f"].astype(F32)[None, :], gains, _block_diag_mean())


def _head_norm_t(h, bd, gain):
    w, tm = h.shape
    hi, lo = _split2(h * h)
    bd16 = bd.astype(BF16)
    slabs = [jnp.dot(bd16, hi[c * MXU_DIM:(c + 1) * MXU_DIM, :], preferred_element_type=F32)
             + jnp.dot(bd16, lo[c * MXU_DIM:(c + 1) * MXU_DIM, :], preferred_element_type=F32)
             for c in range(w // MXU_DIM)]
    ms = slabs[0] if len(slabs) == 1 else jnp.concatenate(slabs, axis=0)
    return h * lax.rsqrt(ms + EPS) * jnp.tile(gain, (1, tm // LANES))


def _rope_t(y, rope_ref):
    half = HEAD_DIM // 8
    cos = rope_ref[0:half, :]
    sin = rope_ref[half:2 * half, :]
    parts = []
    for h in range(y.shape[0] // HEAD_DIM):
        r0 = h * HEAD_DIM
        x1 = y[r0:r0 + half, :]
        x2 = y[r0 + half:r0 + 2 * half, :]
        parts += [x1 * cos - x2 * sin, x2 * cos + x1 * sin, y[r0 + 2 * half:r0 + HEAD_DIM, :]]
    return jnp.concatenate(parts, axis=0)


def _proj_t_kernel(*refs, specs, pw, n_small):
    x_ref, g_ref, wt_ref, ws_ref, rope_ref, gains_ref, bd_ref = refs[:7]
    outs = refs[7:]
    xb = _rms_rows(x_ref[...], g_ref[...]).astype(BF16)
    bd = bd_ref[...]
    for j, (gain_row, rope, scale) in enumerate(specs):
        h = lax.dot_general(wt_ref[j * pw:(j + 1) * pw, :], xb, NT_DIMS, preferred_element_type=F32)
        if gain_row is not None:
            h = _head_norm_t(h, bd, gains_ref[gain_row])
        if rope:
            h = _rope_t(h, rope_ref)
        if scale != 1.0:
            h = h * scale
        outs[j][...] = h.astype(BF16)
    if n_small:
        hs = lax.dot_general(ws_ref[...], xb, NT_DIMS, preferred_element_type=F32)
        outs[len(specs)][...] = hs * (WIDTH_I ** -0.5)


def _rope_table_t(pos):
    rot = HEAD_DIM // 4
    half = rot // 2
    inv = jnp.power(F32(ROPE_THETA), -jnp.arange(half, dtype=F32) * (2.0 / rot))
    ang = inv[:, None] * pos.astype(F32)[None, :]
    return jnp.concatenate([jnp.cos(ang), jnp.sin(ang)], axis=0)


def _proj_t(x2d, norm, w_cols, specs, gains, pos, w_small, tm):
    m, d = x2d.shape
    pw = w_cols[0].shape[1]
    wt = jnp.concatenate([w.T for w in w_cols], axis=0).astype(BF16)
    n_small = 0 if w_small is None else w_small.shape[1]
    ws = jnp.zeros((SUBLANES, d), BF16) if w_small is None else w_small.T.astype(BF16)
    gains_arr = (jnp.zeros((1, pw, LANES), F32) if not gains else
                 jnp.stack([jnp.broadcast_to(jnp.tile(g.astype(F32), pw // g.shape[0])[:, None], (pw, LANES))
                            for g in gains]))
    rope = jnp.zeros((2 * SUBLANES, m), F32) if pos is None else _rope_table_t(pos)
    out_shape = [jax.ShapeDtypeStruct((pw, m), BF16) for _ in specs]
    out_specs = [pl.BlockSpec((pw, tm), lambda i: (0, i)) for _ in specs]
    if n_small:
        out_shape.append(jax.ShapeDtypeStruct((n_small, m), F32))
        out_specs.append(pl.BlockSpec((n_small, tm), lambda i: (0, i)))
    return pl.pallas_call(
        functools.partial(_proj_t_kernel, specs=tuple(specs), pw=pw, n_small=n_small),
        grid=(m // tm,),
        in_specs=[_row_spec(tm, d), _const_spec((1, d)), _const_spec(wt.shape), _const_spec(ws.shape),
                  pl.BlockSpec((2 * SUBLANES, tm), lambda i: (0, i)), _const_spec(gains_arr.shape),
                  _const_spec((MXU_DIM, MXU_DIM))],
        out_specs=out_specs,
        out_shape=out_shape,
        compiler_params=_cparams(1),
        name="proj_t",
    )(x2d, norm.astype(F32)[None, :], wt, ws, rope, gains_arr, _block_diag_mean())


def _silu(g):
    return g * jax.nn.sigmoid(g)


def _out0_kernel(x_ref, oa_ref, ga_ref, ob_ref, gb_ref, sub_ref, w_ref, y_ref, *, post_scale):
    ya = (oa_ref[...] * _silu(ga_ref[...])).astype(BF16)
    ob = ob_ref[...]
    cols = []
    for h in range(N_HEADS_B):
        oc = ob[:, h * LANES:(h + 1) * LANES]
        ms = jnp.mean(oc * oc, axis=-1, keepdims=True)
        cols.append(oc * lax.rsqrt(ms + EPS))
    obn = jnp.concatenate(cols, axis=-1) * sub_ref[...] * post_scale
    yb = (obn * _silu(gb_ref[...])).astype(BF16)
    y = jnp.dot(ya, w_ref[0:WIDTH_A, :], preferred_element_type=F32)
    y = y + jnp.dot(yb, w_ref[WIDTH_A:WIDTH_A + WIDTH_B, :], preferred_element_type=F32)
    y_ref[...] = x_ref[...] + y


def _out1_kernel(x_ref, o_ref, g_ref, w_ref, y_ref):
    yo = (o_ref[...] * _silu(g_ref[...])).astype(BF16)
    y_ref[...] = x_ref[...] + jnp.dot(yo, w_ref[...], preferred_element_type=F32)


def _out0(x2d, o_a, g_a, o_b, g_b, p, lam_init, tm):
    m, d = x2d.shape
    sub = _tile_gain(p["subln_b"], WIDTH_B)
    w = p["w_out"].astype(BF16)
    return pl.pallas_call(
        functools.partial(_out0_kernel, post_scale=1.0 - lam_init),
        grid=(m // tm,),
        in_specs=[_row_spec(tm, d), _row_spec(tm, WIDTH_A), _row_spec(tm, WIDTH_A), _row_spec(tm, WIDTH_B),
                  _row_spec(tm, WIDTH_B), _const_spec(sub.shape), _const_spec(w.shape)],
        out_specs=_row_spec(tm, d),
        out_shape=jax.ShapeDtypeStruct((m, d), F32),
        compiler_params=_cparams(1),
        name="out0",
    )(x2d, o_a, g_a, o_b, g_b, sub, w)


def _out1(x2d, o, g, p, tm):
    m, d = x2d.shape
    w = p["w_out"].astype(BF16)
    return pl.pallas_call(
        _out1_kernel,
        grid=(m // tm,),
        in_specs=[_row_spec(tm, d), _row_spec(tm, WIDTH_C), _row_spec(tm, WIDTH_C), _const_spec(w.shape)],
        out_specs=_row_spec(tm, d),
        out_shape=jax.ShapeDtypeStruct((m, d), F32),
        compiler_params=_cparams(1),
        name="out1",
    )(x2d, o, g, w)


def _select_topk(load_keys, n_chunks, groups, rows, nsel, idx_bits, idx_of):
    shape = (rows, LANES)
    rb = min(rows, COUNT_ROWS)
    n_blk = rows // rb

    def count(pred_of):
        accs = []
        for b in range(n_blk):
            pred = pred_of(slice(b * rb, (b + 1) * rb))

            def body(c, acc, b=b, pred=pred):
                blk = load_keys(c, b * rb, rb)
                for g in range(groups):
                    acc = jnp.where(pred(blk[:, g * LANES:(g + 1) * LANES], c, g), acc + 1.0, acc)
                return acc
            accs.append(lax.fori_loop(0, n_chunks, body, jnp.zeros((rb, LANES), F32),
                                      unroll=isinstance(n_chunks, int)))
        outs = [jnp.broadcast_to(jnp.sum(a, axis=-1, keepdims=True), (rb, LANES)) for a in accs]
        return outs[0] if n_blk == 1 else jnp.concatenate(outs, axis=0)

    kf = float(nsel)

    def bit_step(i, thr):
        bit = lax.shift_left(jnp.int32(1), jnp.int32(31) - i)
        cand = thr ^ bit

        def pred_of(sl):
            cb = cand[sl]
            return lambda k, c, g: k >= cb
        return jnp.where(count(pred_of) >= kf, cand, thr)

    thr = lax.fori_loop(0, 32, bit_step, jnp.full(shape, jnp.iinfo(jnp.int32).min, I32))
    thr = jnp.maximum(thr, KEY_NEG_INF + 1)

    def gt_of(sl):
        tb = thr[sl]
        return lambda k, c, g: k > tb

    def ge_of(sl):
        tb = thr[sl]
        return lambda k, c, g: k >= tb

    n_gt = count(gt_of)
    n_ge = count(ge_of)
    need = kf - n_gt
    excess = n_ge - kf

    def tie_cut():
        def idx_step(i, cut):
            bit = lax.shift_left(jnp.int32(1), jnp.int32(idx_bits - 1) - i)
            cand = cut | bit

            def pred_of(sl):
                tb, cb = thr[sl], cand[sl]
                return lambda k, c, g: (k == tb) & (idx_of(c, g, rb) < cb)
            return jnp.where(count(pred_of) < need, cand, cut)
        return lax.fori_loop(0, idx_bits, idx_step, jnp.zeros(shape, I32))

    big = jnp.full(shape, jnp.iinfo(jnp.int32).max, I32)
    any_excess = jnp.max(excess) > 0.0
    cut = lax.cond(any_excess, lambda: jnp.where(excess > 0.0, tie_cut(), big), lambda: big)
    return thr, cut


def _pair_masks(rows):
    lane = lax.broadcasted_iota(I32, (rows, LANES), 1)
    return lane < HEAD_DIM


def _idx_prompt_kernel(qi_ref, wi_ref, ki_ref, bias_ref, qm_ref, wb_ref, s_ref, *, tq, tk, nsel, seq):
    qi = pl.program_id(1)
    q0 = qi * tq
    n_chunks = seq // tk
    nc = (q0 + tq + tk - 1) // tk
    lo = _pair_masks(tq)
    groups = tk // LANES
    for j in range(N_IDX_HEADS // 2):
        pair = qi_ref[0, :, j * LANES:(j + 1) * LANES]
        zero = jnp.zeros_like(pair)
        qm_ref[2 * j] = jnp.where(lo, pair, zero)
        qm_ref[2 * j + 1] = jnp.where(lo, zero, pair)
    w = wi_ref[0]
    for h in range(N_IDX_HEADS):
        wb_ref[h] = jnp.broadcast_to(w[:, h:h + 1], (tq, LANES))
    row = q0 + lax.broadcasted_iota(I32, (tq, tk), 0)
    col0 = lax.broadcasted_iota(I32, (tq, tk), 1)

    def score_body(c, carry):
        start = pl.multiple_of(c * tk, tk)
        kblk = ki_ref[0, pl.ds(start, tk), :]
        acc = jnp.zeros((tq, tk), F32)
        for h in range(N_IDX_HEADS):
            s = lax.dot_general(qm_ref[h], kblk, NT_DIMS, preferred_element_type=F32)
            acc = acc + jnp.maximum(s, 0.0) * jnp.tile(wb_ref[h], (1, groups))
        acc = jnp.where(col0 + c * tk <= row, acc, -jnp.inf)
        s_ref[c] = _sort_key(acc)
        return carry

    lax.fori_loop(0, nc, score_body, 0)

    def idx_of(c, g, nr):
        return lax.broadcasted_iota(I32, (nr, LANES), 1) + (c * tk + g * LANES)

    thr, cut = _select_topk(lambda c, r0, nr: s_ref[c, r0:r0 + nr, :], nc, groups, tq, nsel,
                            int(math.log2(seq)) + 1, idx_of)
    thr_t = jnp.tile(thr, (1, groups))
    cut_t = jnp.tile(cut, (1, groups))

    def write_body(c, carry):
        k = s_ref[c]
        sel = (k > thr_t) | ((k == thr_t) & (col0 + c * tk <= cut_t))
        bias_ref[0, c] = jnp.where(sel, 0.0, NEG).astype(BF16)
        return carry

    lax.fori_loop(0, nc, write_body, 0)

    def fill_body(c, carry):
        bias_ref[0, c] = jnp.full((tq, tk), NEG, BF16)
        return carry

    lax.fori_loop(nc, n_chunks, fill_body, 0)


def _idx_prompt(qi16, wi, ki16, nsel, tq, tk):
    b, seq, _ = qi16.shape
    n_chunks = seq // tk
    return pl.pallas_call(
        functools.partial(_idx_prompt_kernel, tq=tq, tk=tk, nsel=nsel, seq=seq),
        grid=(b, seq // tq),
        in_specs=[pl.BlockSpec((1, tq, WIDTH_I), lambda bi, qi: (bi, qi, 0)),
                  pl.BlockSpec((1, tq, N_IDX_HEADS), lambda bi, qi: (bi, qi, 0)),
                  pl.BlockSpec((1, seq, LANES), lambda bi, qi: (bi, 0, 0))],
        out_specs=pl.BlockSpec((1, n_chunks, tq, tk), lambda bi, qi: (bi, 0, qi, 0)),
        out_shape=jax.ShapeDtypeStruct((b, n_chunks, seq, tk), BF16),
        scratch_shapes=[pltpu.VMEM((N_IDX_HEADS, tq, LANES), BF16),
                        pltpu.VMEM((N_IDX_HEADS, tq, LANES), F32),
                        pltpu.VMEM((n_chunks, tq, tk), I32)],
        compiler_params=_cparams(2),
        name="idx_prompt",
    )(qi16, wi, ki16)


def _flash_prompt_kernel(*refs, variant, n_pairs, tq, tk):
    it = iter(refs)
    q_ref, k_ref, v_ref = next(it), next(it), next(it)
    bias_ref = next(it) if variant == "dsa" else None
    cq_ref = next(it) if variant == "fox" else None
    ck_ref = next(it) if variant == "fox" else None
    lam_ref = next(it) if variant == "diff" else None
    o_ref = next(it)
    qm_ref, m_ref, l_ref, acc_ref = next(it), next(it), next(it), next(it)
    cqb_ref = next(it) if variant == "fox" else None

    qi = pl.program_id(1)
    kc = pl.program_id(2)
    nk = pl.num_programs(2)
    last = ((qi + 1) * tq - 1) // tk
    n_units = 2 * n_pairs
    groups = tk // LANES
    lo = _pair_masks(tq)

    @pl.when(kc == 0)
    def _init():
        for j in range(n_pairs):
            pair = q_ref[0, :, j * LANES:(j + 1) * LANES]
            zero = jnp.zeros_like(pair)
            qm_ref[2 * j] = jnp.where(lo, pair, zero)
            qm_ref[2 * j + 1] = jnp.where(lo, zero, pair)
        m_ref[...] = jnp.full(m_ref.shape, NEG, F32)
        l_ref[...] = jnp.zeros(l_ref.shape, F32)
        acc_ref[...] = jnp.zeros(acc_ref.shape, F32)
        if variant == "fox":
            cq = cq_ref[0]
            for u in range(n_units):
                cqb_ref[u] = jnp.broadcast_to(cq[:, u:u + 1], (tq, LANES))

    def compute(masked):
        if variant == "dsa":
            bias = bias_ref[0, 0].astype(F32)
            masked = False
        if masked:
            row = qi * tq + lax.broadcasted_iota(I32, (tq, tk), 0)
            col = kc * tk + lax.broadcasted_iota(I32, (tq, tk), 1)
            causal = col <= row
        for j in range(n_pairs):
            kp = k_ref[0, :, j * LANES:(j + 1) * LANES]
            vp = v_ref[0, :, j * LANES:(j + 1) * LANES]
            alphas, pvs = [], []
            for e in range(2):
                u = 2 * j + e
                s = lax.dot_general(qm_ref[u], kp, NT_DIMS, preferred_element_type=F32)
                if variant == "dsa":
                    s = s + bias
                if variant == "fox":
                    s = s + (jnp.tile(cqb_ref[u], (1, groups)) - ck_ref[0, u:u + 1, :])
                if masked:
                    s = jnp.where(causal, s, NEG)
                m_prev = m_ref[u]
                m_new = jnp.maximum(m_prev, jnp.max(s, axis=-1, keepdims=True))
                alpha = jnp.exp(m_prev - m_new)
                p = jnp.exp(s - jnp.tile(m_new, (1, groups)))
                l_ref[u] = alpha * l_ref[u] + jnp.sum(p, axis=-1, keepdims=True)
                m_ref[u] = m_new
                pv = jnp.dot(p.astype(BF16), vp, preferred_element_type=F32)
                if variant == "diff":
                    acc_ref[u] = alpha * acc_ref[u] + pv
                else:
                    alphas.append(alpha)
                    pvs.append(pv)
            if variant != "diff":
                acc_ref[j] = jnp.where(lo, alphas[0], alphas[1]) * acc_ref[j] + jnp.where(lo, pvs[0], pvs[1])

    needs_mask = (kc + 1) * tk - 1 > qi * tq

    @pl.when((kc <= last) & needs_mask)
    def _diag():
        compute(True)

    @pl.when((kc <= last) & jnp.logical_not(needs_mask))
    def _full():
        compute(False)

    @pl.when(kc == nk - 1)
    def _fin():
        for j in range(n_pairs):
            if variant == "diff":
                lam = lam_ref[0, 0]
                o_ref[0, :, j * LANES:(j + 1) * LANES] = (
                    acc_ref[2 * j] / l_ref[2 * j] - lam * (acc_ref[2 * j + 1] / l_ref[2 * j + 1]))
            else:
                o_ref[0, :, j * LANES:(j + 1) * LANES] = acc_ref[j] / jnp.where(lo, l_ref[2 * j], l_ref[2 * j + 1])


def _flash_prompt(variant, q16, k16, v16, tq, tk, bias=None, cq=None, ck=None, lam=None):
    b, seq, w = q16.shape
    n_pairs = w // LANES
    n_units = 2 * n_pairs

    def last_of(qi):
        return ((qi + 1) * tq - 1) // tk

    q_spec = pl.BlockSpec((1, tq, w), lambda bi, qi, kc: (bi, qi, 0))
    kv_spec = pl.BlockSpec((1, tk, w), lambda bi, qi, kc: (bi, jnp.minimum(kc, last_of(qi)), 0))
    in_specs = [q_spec, kv_spec, kv_spec]
    args = [q16, k16, v16]
    scratch = [pltpu.VMEM((n_units, tq, LANES), BF16), pltpu.VMEM((n_units, tq, LANES), F32),
               pltpu.VMEM((n_units, tq, LANES), F32),
               pltpu.VMEM((n_units if variant == "diff" else n_pairs, tq, LANES), F32)]
    if variant == "dsa":
        in_specs.append(pl.BlockSpec((1, 1, tq, tk), lambda bi, qi, kc: (bi, jnp.minimum(kc, last_of(qi)), qi, 0)))
        args.append(bias)
    if variant == "fox":
        in_specs.append(pl.BlockSpec((1, tq, n_units), lambda bi, qi, kc: (bi, qi, 0)))
        in_specs.append(pl.BlockSpec((1, n_units, tk), lambda bi, qi, kc: (bi, 0, jnp.minimum(kc, last_of(qi)))))
        args += [cq, ck]
        scratch.append(pltpu.VMEM((n_units, tq, LANES), F32))
    if variant == "diff":
        in_specs.append(pl.BlockSpec(memory_space=pltpu.SMEM))
        args.append(lam)
    return pl.pallas_call(
        functools.partial(_flash_prompt_kernel, variant=variant, n_pairs=n_pairs, tq=tq, tk=tk),
        grid=(b, seq // tq, seq // tk),
        in_specs=in_specs,
        out_specs=pl.BlockSpec((1, tq, w), lambda bi, qi, kc: (bi, qi, 0)),
        out_shape=jax.ShapeDtypeStruct((b, seq, w), F32),
        scratch_shapes=scratch,
        compiler_params=_cparams(3),
        name="flash_" + variant,
    )(*args)


def _cumsum_kernel(pt_ref, x_ref, xn_ref, c_ref, cn_ref, carry_ref):
    del pt_ref
    p = pl.program_id(1)
    n_pages = pl.num_programs(1)
    r = lax.broadcasted_iota(I32, (LANES, LANES), 0)
    c = lax.broadcasted_iota(I32, (LANES, LANES), 1)
    upper = jnp.where(r <= c, 1.0, 0.0).astype(F32)

    @pl.when(p == 0)
    def _():
        carry_ref[...] = jnp.zeros(carry_ref.shape, F32)

    cs = jnp.dot(x_ref[0], upper, precision=HIGHEST, preferred_element_type=F32) + carry_ref[...]
    c_ref[0] = cs
    total = jnp.broadcast_to(cs[:, LANES - 1:LANES], cs.shape)
    carry_ref[...] = total

    @pl.when(p == n_pages - 1)
    def _():
        cn_ref[0] = jnp.dot(xn_ref[0], upper, precision=HIGHEST, preferred_element_type=F32) + total


def _cumsum_pages(page_table, pool_t, new_t):
    b, n_pages = page_table.shape
    h = pool_t.shape[1]
    grid_spec = pltpu.PrefetchScalarGridSpec(
        num_scalar_prefetch=1,
        grid=(b, n_pages),
        in_specs=[pl.BlockSpec((1, h, LANES), lambda bi, p, pt: (pt[bi, p], 0, 0)),
                  pl.BlockSpec((1, h, LANES), lambda bi, p, pt: (bi, 0, 0))],
        out_specs=[pl.BlockSpec((1, h, LANES), lambda bi, p, pt: (bi, 0, p)),
                   pl.BlockSpec((1, h, LANES), lambda bi, p, pt: (bi, 0, 0))],
        scratch_shapes=[pltpu.VMEM((h, LANES), F32)],
    )
    return pl.pallas_call(
        _cumsum_kernel,
        grid_spec=grid_spec,
        out_shape=(jax.ShapeDtypeStruct((b, h, n_pages * LANES), F32), jax.ShapeDtypeStruct((b, h, LANES), F32)),
        compiler_params=_cparams(2),
        name="cumsum_pages",
    )(page_table, pool_t, new_t)


def _idx_decode_kernel(pt_ref, q_ref, w_ref, kpool_ref, knew_ref, bias_ref, s_ref, *, n_pages, n_new, nsel, page):
    del pt_ref
    p = pl.program_id(1)
    rows = n_new

    def scores(kblk):
        s = lax.dot_general(q_ref[0], kblk.astype(BF16), NT_DIMS, preferred_element_type=F32)
        t = jnp.maximum(s, 0.0) * w_ref[0]
        acc = t[0:rows]
        for h in range(1, N_IDX_HEADS):
            acc = acc + t[h * rows:(h + 1) * rows]
        return acc

    @pl.when(p < n_pages)
    def _past():
        s_ref[p] = _sort_key(scores(kpool_ref[0]))

    @pl.when(p == n_pages)
    def _new():
        sc = scores(knew_ref[0])
        i = lax.broadcasted_iota(I32, (rows, LANES), 0)
        lane = lax.broadcasted_iota(I32, (rows, LANES), 1)
        s_ref[n_pages] = _sort_key(jnp.where(lane <= i, sc, -jnp.inf))

        def idx_of(c, g):
            return lane + c * page

        n_keys = (n_pages + 1) * page
        thr, cut = _select_topk(lambda c: s_ref[c], n_pages + 1, 1, rows, nsel, int(math.log2(n_keys)) + 1, idx_of)

        def write_body(c, carry):
            k = s_ref[c]
            sel = (k > thr) | ((k == thr) & (lane + c * page <= cut))
            bias_ref[0, c] = jnp.where(sel, 0.0, NEG)
            return carry

        lax.fori_loop(0, n_pages + 1, write_body, 0)


def _idx_decode(page_table, q_st, w_st, kpool, knew, nsel):
    b, n_pages = page_table.shape
    page = kpool.shape[1]
    n_new = q_st.shape[1] // N_IDX_HEADS
    grid_spec = pltpu.PrefetchScalarGridSpec(
        num_scalar_prefetch=1,
        grid=(b, n_pages + 1),
        in_specs=[pl.BlockSpec((1,) + q_st.shape[1:], lambda bi, p, pt: (bi, 0, 0)),
                  pl.BlockSpec((1,) + w_st.shape[1:], lambda bi, p, pt: (bi, 0, 0)),
                  pl.BlockSpec((1, page, IDX_DIM), lambda bi, p, pt: (pt[bi, jnp.minimum(p, n_pages - 1)], 0, 0)),
                  pl.BlockSpec((1, page, IDX_DIM), lambda bi, p, pt: (bi, 0, 0))],
        out_specs=pl.BlockSpec((1, n_pages + 1, n_new, LANES), lambda bi, p, pt: (bi, 0, 0, 0)),
        scratch_shapes=[pltpu.VMEM((n_pages + 1, n_new, LANES), I32)],
    )
    return pl.pallas_call(
        functools.partial(_idx_decode_kernel, n_pages=n_pages, n_new=n_new, nsel=nsel, page=page),
        grid_spec=grid_spec,
        out_shape=jax.ShapeDtypeStruct((b, n_pages + 1, n_new, LANES), F32),
        compiler_params=_cparams(2),
        name="idx_decode",
    )(page_table, q_st, w_st, kpool, knew)


def _attn_decode_kernel(*refs, variant, n_units, n_new, n_pages, v_unit):
    it = iter(refs)
    _pt_ref = next(it)
    q_ref, kpool_ref, vpool_ref, knew_ref, vnew_ref = next(it), next(it), next(it), next(it), next(it)
    bias_ref = next(it) if variant == "dsa" else None
    cq_ref = next(it) if variant == "fox" else None
    ck_ref = next(it) if variant == "fox" else None
    cn_ref = next(it) if variant == "fox" else None
    lam_ref = next(it) if variant == "diff" else None
    o_ref = next(it)
    qbd_ref, m_ref, l_ref, acc_ref = next(it), next(it), next(it), next(it)

    p = pl.program_id(1)
    rows = n_units * n_new
    w = q_ref.shape[-1]
    wv = acc_ref.shape[-1]

    @pl.when(p == 0)
    def _init():
        q = q_ref[0].astype(F32)
        qt = jnp.concatenate([q] * n_units, axis=0)
        r = _div_pow2(lax.broadcasted_iota(I32, (rows, w), 0), n_new)
        c = _div_pow2(lax.broadcasted_iota(I32, (rows, w), 1), HEAD_DIM)
        qbd_ref[...] = jnp.where(r == c, qt, 0.0).astype(BF16)
        m_ref[...] = jnp.full(m_ref.shape, NEG, F32)
        l_ref[...] = jnp.zeros(l_ref.shape, F32)
        acc_ref[...] = jnp.zeros(acc_ref.shape, F32)

    def expand_rows(x8):
        return jnp.concatenate([x8] * n_units, axis=0)

    def expand_units(xu):
        return jnp.concatenate([jnp.broadcast_to(xu[u:u + 1, :], (n_new, LANES)) for u in range(n_units)], axis=0)

    def step(k, v, is_new):
        s = lax.dot_general(qbd_ref[...], k.astype(BF16), NT_DIMS, preferred_element_type=F32)
        if variant == "dsa":
            s = s + expand_rows(bias_ref[0, 0])
        if variant == "fox":
            ck = cn_ref[0] if is_new else ck_ref[0]
            s = s + (cq_ref[0] - expand_units(ck))
        if is_new:
            i = expand_rows(lax.broadcasted_iota(I32, (n_new, LANES), 0))
            lane = lax.broadcasted_iota(I32, (rows, LANES), 1)
            s = jnp.where(lane <= i, s, NEG)
        m_prev = m_ref[...]
        m_new = jnp.maximum(m_prev, jnp.max(s, axis=-1, keepdims=True))
        alpha = jnp.exp(m_prev - m_new)
        pr = jnp.exp(s - m_new)
        l_ref[...] = alpha * l_ref[...] + jnp.sum(pr, axis=-1, keepdims=True)
        m_ref[...] = m_new
        pv = jnp.dot(pr.astype(BF16), v.astype(BF16), preferred_element_type=F32)
        acc_ref[...] = jnp.tile(alpha, (1, wv // LANES)) * acc_ref[...] + pv

    @pl.when(p < n_pages)
    def _past():
        step(kpool_ref[0], vpool_ref[0], False)

    @pl.when(p == n_pages)
    def _new():
        step(knew_ref[0], vnew_ref[0], True)
        accn = acc_ref[...] / jnp.tile(l_ref[...], (1, wv // LANES))
        cu = _div_pow2(lax.broadcasted_iota(I32, (n_new, wv), 1), v_unit)
        out = jnp.zeros((n_new, wv), F32)
        if variant == "diff":
            lam = lam_ref[0, 0]
            for h in range(n_units // 2):
                a0 = accn[(2 * h) * n_new:(2 * h + 1) * n_new]
                a1 = accn[(2 * h + 1) * n_new:(2 * h + 2) * n_new]
                out = out + jnp.where(cu == h, a0 - lam * a1, 0.0)
        else:
            for u in range(n_units):
                out = out + jnp.where(cu == u, accn[u * n_new:(u + 1) * n_new], 0.0)
        o_ref[0] = out


def _attn_decode(variant, page_table, q16, kpool, vpool, knew, vnew, bias=None, cq=None, ck=None, cn=None, lam=None):
    b, n_pages = page_table.shape
    page = kpool.shape[1]
    n_new, w = q16.shape[1], q16.shape[2]
    wv = vpool.shape[2]
    n_units = w // HEAD_DIM
    rows = n_units * n_new
    v_unit = wv // (n_units // 2) if variant == "diff" else HEAD_DIM

    def pool_map(bi, p, pt):
        return (pt[bi, jnp.minimum(p, n_pages - 1)], 0, 0)

    def req_map(bi, p, pt):
        return (bi, 0, 0)

    in_specs = [pl.BlockSpec((1, n_new, w), req_map),
                pl.BlockSpec((1, page, w), pool_map), pl.BlockSpec((1, page, wv), pool_map),
                pl.BlockSpec((1, page, w), req_map), pl.BlockSpec((1, page, wv), req_map)]
    args = [q16, kpool, vpool, knew, vnew]
    if variant == "dsa":
        in_specs.append(pl.BlockSpec((1, 1, n_new, LANES), lambda bi, p, pt: (bi, p, 0, 0)))
        args.append(bias)
    if variant == "fox":
        in_specs.append(pl.BlockSpec((1, rows, LANES), req_map))
        in_specs.append(pl.BlockSpec((1, n_units, LANES), lambda bi, p, pt: (bi, 0, jnp.minimum(p, n_pages - 1))))
        in_specs.append(pl.BlockSpec((1, n_units, LANES), req_map))
        args += [cq, ck, cn]
    if variant == "diff":
        in_specs.append(pl.BlockSpec(memory_space=pltpu.SMEM))
        args.append(lam)
    grid_spec = pltpu.PrefetchScalarGridSpec(
        num_scalar_prefetch=1,
        grid=(b, n_pages + 1),
        in_specs=in_specs,
        out_specs=pl.BlockSpec((1, n_new, wv), req_map),
        scratch_shapes=[pltpu.VMEM((rows, w), BF16), pltpu.VMEM((rows, LANES), F32),
                        pltpu.VMEM((rows, LANES), F32), pltpu.VMEM((rows, wv), F32)],
    )
    return pl.pallas_call(
        functools.partial(_attn_decode_kernel, variant=variant, n_units=n_units, n_new=n_new, n_pages=n_pages,
                          v_unit=v_unit),
        grid_spec=grid_spec,
        out_shape=jax.ShapeDtypeStruct((b, n_new, wv), F32),
        compiler_params=_cparams(2),
        name="attn_decode_" + variant,
    )(page_table, *args)


def _select_topk_t(load_keys, n_chunks, rpc, cols, nsel, idx_bits):
    shape = (SUBLANES, cols)
    reps = rpc // SUBLANES
    n_part = math.gcd(reps, COUNT_PARTS)

    def count(pred):
        def body(c, acc):
            x = jnp.where(pred(load_keys(c).reshape(reps, SUBLANES, cols), c), 1.0, 0.0)
            part = jnp.sum(x.reshape(reps // n_part, n_part, SUBLANES, cols), axis=0)
            return acc + jnp.sum(part, axis=0)
        acc = lax.fori_loop(0, n_chunks, body, jnp.zeros(shape, F32))
        return jnp.broadcast_to(jnp.sum(acc, axis=0, keepdims=True), shape)

    kf = float(nsel)

    def bit_step(i, thr):
        bit = lax.shift_left(jnp.int32(1), jnp.int32(31) - i)
        cand = thr ^ bit
        return jnp.where(count(lambda k, c: k >= cand[None]) >= kf, cand, thr)

    thr = lax.fori_loop(0, 32, bit_step, jnp.full(shape, jnp.iinfo(jnp.int32).min, I32))
    thr = jnp.maximum(thr, KEY_NEG_INF + 1)
    n_gt = count(lambda k, c: k > thr[None])
    n_ge = count(lambda k, c: k >= thr[None])
    need = kf - n_gt
    excess = n_ge - kf
    row = lax.broadcasted_iota(I32, (rpc, cols), 0).reshape(reps, SUBLANES, cols)

    def tie_cut():
        def idx_step(i, cut):
            bit = lax.shift_left(jnp.int32(1), jnp.int32(idx_bits - 1) - i)
            cand = cut | bit
            n_lt = count(lambda k, c: (k == thr[None]) & (row + c * rpc < cand[None]))
            return jnp.where(n_lt < need, cand, cut)
        return lax.fori_loop(0, idx_bits, idx_step, jnp.zeros(shape, I32))

    big = jnp.full(shape, jnp.iinfo(jnp.int32).max, I32)
    any_excess = jnp.max(excess) > 0.0
    cut = lax.cond(any_excess, lambda: jnp.where(excess > 0.0, tie_cut(), big), lambda: big)
    return thr, cut


def _masked_pairs_t(qt_ref, qm_ref, n_pairs, tq):
    top = lax.broadcasted_iota(I32, (LANES, tq), 0) < HEAD_DIM
    for j in range(n_pairs):
        pair = qt_ref[j * LANES:(j + 1) * LANES, :]
        zero = jnp.zeros_like(pair)
        qm_ref[2 * j] = jnp.where(top, pair, zero)
        qm_ref[2 * j + 1] = jnp.where(top, zero, pair)


def _idx_prompt_t_kernel(qit_ref, wit_ref, ki_ref, bias_ref, qm_ref, s_ref, *, tq, tk, nsel, seq):
    qi = pl.program_id(1)
    q0 = qi * tq
    n_chunks = seq // tk
    nc = (q0 + tq + tk - 1) // tk
    _masked_pairs_t(qit_ref, qm_ref, N_IDX_HEADS // 2, tq)
    wt = wit_ref[...]
    col = q0 + lax.broadcasted_iota(I32, (tk, tq), 1)
    row0 = lax.broadcasted_iota(I32, (tk, tq), 0)

    def score_body(c, carry):
        start = pl.multiple_of(c * tk, tk)
        kblk = ki_ref[0, pl.ds(start, tk), :]
        acc = jnp.zeros((tk, tq), F32)
        for h in range(N_IDX_HEADS):
            s = jnp.dot(kblk, qm_ref[h], preferred_element_type=F32)
            acc = acc + jnp.maximum(s, 0.0) * wt[h:h + 1, :]
        acc = jnp.where(row0 + c * tk <= col, acc, -jnp.inf)
        s_ref[c] = _sort_key(acc)
        return carry

    lax.fori_loop(0, nc, score_body, 0)
    thr, cut = _select_topk_t(lambda c: s_ref[c], nc, tk, tq, nsel, int(math.log2(seq)) + 1)
    thr_t = jnp.tile(thr, (tk // SUBLANES, 1))
    cut_t = jnp.tile(cut, (tk // SUBLANES, 1))

    def write_body(c, carry):
        k = s_ref[c]
        sel = (k > thr_t) | ((k == thr_t) & (row0 + c * tk <= cut_t))
        start = pl.multiple_of(c * tk, tk)
        bias_ref[0, 0, pl.ds(start, tk), :] = jnp.where(sel, 0.0, NEG).astype(BF16)
        return carry

    lax.fori_loop(0, nc, write_body, 0)

    def fill_body(c, carry):
        start = pl.multiple_of(c * tk, tk)
        bias_ref[0, 0, pl.ds(start, tk), :] = jnp.full((tk, tq), NEG, BF16)
        return carry

    lax.fori_loop(nc, n_chunks, fill_body, 0)


def _idx_prompt_t(qit, wit, ki16, b, nsel, tq, tk):
    seq = ki16.shape[1]
    nq = seq // tq
    return pl.pallas_call(
        functools.partial(_idx_prompt_t_kernel, tq=tq, tk=tk, nsel=nsel, seq=seq),
        grid=(b, nq),
        in_specs=[pl.BlockSpec((WIDTH_I, tq), lambda bi, qi: (0, bi * nq + qi)),
                  pl.BlockSpec((N_IDX_HEADS, tq), lambda bi, qi: (0, bi * nq + qi)),
                  pl.BlockSpec((1, seq, LANES), lambda bi, qi: (bi, 0, 0))],
        out_specs=pl.BlockSpec((1, 1, seq, tq), lambda bi, qi: (bi, qi, 0, 0)),
        out_shape=jax.ShapeDtypeStruct((b, nq, seq, tq), BF16),
        scratch_shapes=[pltpu.VMEM((N_IDX_HEADS, LANES, tq), BF16),
                        pltpu.VMEM((seq // tk, tk, tq), I32)],
        compiler_params=_cparams(2),
        name="idx_prompt",
    )(qit, wit, ki16)


def _flash_t_kernel(*refs, variant, n_pairs, tq, tk, sub):
    it = iter(refs)
    k_ref, qt_ref, vt_ref = next(it), next(it), next(it)
    bias_ref = next(it) if variant == "dsa" else None
    cqt_ref, ck_ref = (next(it), next(it)) if variant == "fox" else (None, None)
    lam_ref = next(it) if variant == "diff" else None
    o_ref = next(it)
    qm_ref, m_ref, l_ref, acc_ref = next(it), next(it), next(it), next(it)
    bias32_ref = next(it) if variant == "dsa" else None

    qi = pl.program_id(1)
    kc = pl.program_id(2)
    nk = pl.num_programs(2)
    last = ((qi + 1) * tq - 1) // tk
    n_units = 2 * n_pairs
    vr = acc_ref.shape[1]

    @pl.when(kc == 0)
    def _init():
        _masked_pairs_t(qt_ref, qm_ref, n_pairs, tq)
        m_ref[...] = jnp.full(m_ref.shape, NEG, F32)
        l_ref[...] = jnp.zeros(l_ref.shape, F32)
        acc_ref[...] = jnp.zeros(acc_ref.shape, F32)

    def compute(masked):
        if variant == "dsa":
            bias32_ref[...] = bias_ref[0, 0].astype(F32)
            masked = False
        if masked:
            col = qi * tq + lax.broadcasted_iota(I32, (sub, tq), 1)
            row0 = kc * tk + lax.broadcasted_iota(I32, (sub, tq), 0)
        ones_rows = jnp.ones((2 * SUBLANES, sub), BF16)
        if variant == "fox":
            ck_all = ck_ref[0] * LOG2E
        for g0 in range(0, n_units, FLASH_GROUP):
            units = list(range(g0, min(g0 + FLASH_GROUP, n_units)))
            state = [[m_ref[u], l_ref[u], acc_ref[u]] for u in units]
            for r0 in range(0, tk, sub):
                ss = [jnp.dot(k_ref[0, r0:r0 + sub, (u // 2) * LANES:(u // 2 + 1) * LANES], qm_ref[u],
                              preferred_element_type=F32) for u in units]
                ps = []
                for e, u in enumerate(units):
                    s = ss[e]
                    if variant == "dsa":
                        s = s + bias32_ref[r0:r0 + sub, :]
                    if variant == "fox":
                        s = s + (cqt_ref[0, u:u + 1, :] * LOG2E
                                 - jnp.broadcast_to(ck_all[r0:r0 + sub, u:u + 1], (sub, tq)))
                    if masked:
                        s = jnp.where(row0 + r0 <= col, s, NEG)
                    m_prev, l_prev, acc = state[e]
                    m_new = jnp.maximum(m_prev, jnp.max(s, axis=0, keepdims=True))
                    alpha = jnp.exp2(m_prev - m_new)
                    p = jnp.exp2(s - jnp.tile(m_new, (sub // SUBLANES, 1)))
                    state[e][0] = m_new
                    state[e][1] = alpha * l_prev
                    state[e][2] = jnp.tile(alpha, (vr // SUBLANES, 1)) * acc
                    ps.append(p.astype(BF16))
                for e, u in enumerate(units):
                    v0 = (u // 2) * vr if variant == "diff" else u * vr
                    v_aug = jnp.concatenate([vt_ref[v0:v0 + vr, r0:r0 + sub], ones_rows], axis=0)
                    pv = jnp.dot(v_aug, ps[e], preferred_element_type=F32)
                    state[e][1] = state[e][1] + pv[vr:vr + SUBLANES, :]
                    state[e][2] = state[e][2] + pv[0:vr, :]
            for e, u in enumerate(units):
                m_ref[u], l_ref[u], acc_ref[u] = state[e]

    needs_mask = (kc + 1) * tk - 1 > qi * tq

    @pl.when((kc <= last) & needs_mask)
    def _diag():
        compute(True)

    @pl.when((kc <= last) & jnp.logical_not(needs_mask))
    def _full():
        compute(False)

    @pl.when(kc == nk - 1)
    def _fin():
        def norm(u):
            return acc_ref[u] / jnp.tile(l_ref[u], (vr // SUBLANES, 1))
        for j in range(n_pairs):
            if variant == "diff":
                ot = norm(2 * j) - lam_ref[0, 0] * norm(2 * j + 1)
            else:
                ot = jnp.concatenate([norm(2 * j), norm(2 * j + 1)], axis=0)
            o_ref[0, :, j * LANES:(j + 1) * LANES] = ot.T


def _flash_t(variant, k16, qt, vt, tq, tk, bias=None, cqt=None, ck=None, lam=None):
    b, seq, w = k16.shape
    n_pairs = w // LANES
    n_units = 2 * n_pairs
    nq, nk = seq // tq, seq // tk
    vr = LANES if variant == "diff" else HEAD_DIM
    sub = min(tk, FLASH_SUB)

    def last_of(qi):
        return ((qi + 1) * tq - 1) // tk

    def kc_of(qi, kc):
        return jnp.minimum(kc, last_of(qi))

    in_specs = [pl.BlockSpec((1, tk, w), lambda bi, qi, kc: (bi, kc_of(qi, kc), 0)),
                pl.BlockSpec((w, tq), lambda bi, qi, kc: (0, bi * nq + qi)),
                pl.BlockSpec((vt.shape[0], tk), lambda bi, qi, kc: (0, bi * nk + kc_of(qi, kc)))]
    args = [k16, qt, vt]
    scratch = [pltpu.VMEM((n_units, LANES, tq), BF16), pltpu.VMEM((n_units, SUBLANES, tq), F32),
               pltpu.VMEM((n_units, SUBLANES, tq), F32), pltpu.VMEM((n_units, vr, tq), F32)]
    if variant == "dsa":
        in_specs.append(pl.BlockSpec((1, 1, tk, tq), lambda bi, qi, kc: (bi, qi, kc_of(qi, kc), 0)))
        args.append(bias)
        scratch.append(pltpu.VMEM((tk, tq), F32))
    if variant == "fox":
        in_specs.append(pl.BlockSpec((1, n_units, tq), lambda bi, qi, kc: (bi, 0, qi)))
        in_specs.append(pl.BlockSpec((1, tk, n_units), lambda bi, qi, kc: (bi, kc_of(qi, kc), 0)))
        args += [cqt, ck]
    if variant == "diff":
        in_specs.append(pl.BlockSpec(memory_space=pltpu.SMEM))
        args.append(lam)
    return pl.pallas_call(
        functools.partial(_flash_t_kernel, variant=variant, n_pairs=n_pairs, tq=tq, tk=tk, sub=sub),
        grid=(b, nq, nk),
        in_specs=in_specs,
        out_specs=pl.BlockSpec((1, tq, w), lambda bi, qi, kc: (bi, qi, 0)),
        out_shape=jax.ShapeDtypeStruct((b, seq, w), F32),
        scratch_shapes=scratch,
        compiler_params=_cparams(3),
        name="flash_" + variant,
    )(*args)


PAGES_PER_STEP = 8


def _pages_per_step(n_pages):
    g = PAGES_PER_STEP
    while n_pages % g:
        g //= 2
    return g


def _upper_ones():
    r = lax.broadcasted_iota(I32, (LANES, LANES), 0)
    c = lax.broadcasted_iota(I32, (LANES, LANES), 1)
    return jnp.where(r <= c, 1.0, 0.0).astype(BF16)


def _cumsum_kernel2(*refs, g):
    x_refs = refs[1:1 + g]
    xn_ref, c_ref, cn_ref, carry_ref = refs[1 + g:5 + g]
    p = pl.program_id(1)
    upper = _upper_ones()

    @pl.when(p == 0)
    def _():
        carry_ref[...] = jnp.zeros(carry_ref.shape, F32)

    def page_cumsum(x):
        return sum(jnp.dot(t, upper, preferred_element_type=F32) for t in _split3(x))

    local = [page_cumsum(x_refs[j][0]) for j in range(g)]
    total = carry_ref[...]
    for j in range(g):
        c_ref[0, :, j * LANES:(j + 1) * LANES] = local[j] + total
        total = total + jnp.broadcast_to(local[j][:, LANES - 1:LANES], total.shape)
    carry_ref[...] = total

    @pl.when(p == pl.num_programs(1) - 1)
    def _():
        cn_ref[0] = page_cumsum(xn_ref[0]) + total


def _cumsum_pages2(page_table, pool_t, new_t):
    b, n_pages = page_table.shape
    h = pool_t.shape[1]
    g = _pages_per_step(n_pages)
    pool_specs = [pl.BlockSpec((1, h, LANES), lambda bi, p, pt, j=j: (pt[bi, p * g + j], 0, 0)) for j in range(g)]
    grid_spec = pltpu.PrefetchScalarGridSpec(
        num_scalar_prefetch=1,
        grid=(b, n_pages // g),
        in_specs=pool_specs + [pl.BlockSpec((1, h, LANES), lambda bi, p, pt: (bi, 0, 0))],
        out_specs=[pl.BlockSpec((1, h, g * LANES), lambda bi, p, pt: (bi, 0, p)),
                   pl.BlockSpec((1, h, LANES), lambda bi, p, pt: (bi, 0, 0))],
        scratch_shapes=[pltpu.VMEM((h, LANES), F32)],
    )
    return pl.pallas_call(
        functools.partial(_cumsum_kernel2, g=g),
        grid_spec=grid_spec,
        out_shape=(jax.ShapeDtypeStruct((b, h, n_pages * LANES), F32), jax.ShapeDtypeStruct((b, h, LANES), F32)),
        compiler_params=_cparams(2),
        name="cumsum_pages",
    )(page_table, *([pool_t] * g), new_t)


def _idx_decode_kernel2(*refs, g, n_pages, n_new, nsel, page):
    q_ref, w_ref = refs[1:3]
    k_refs = refs[3:3 + g]
    knew_ref, bias_ref, biasn_ref, s_ref = refs[3 + g:7 + g]
    p = pl.program_id(1)
    n_steps = n_pages // g
    rows = n_new

    def scores(kt):
        n = kt.shape[1] // LANES
        s = jnp.dot(q_ref[0], kt.astype(BF16), preferred_element_type=F32)
        t = jnp.maximum(s, 0.0) * jnp.tile(w_ref[0], (1, n))
        acc = t[0:rows]
        for h in range(1, N_IDX_HEADS):
            acc = acc + t[h * rows:(h + 1) * rows]
        return acc

    @pl.when(p < n_steps)
    def _past():
        keys = _sort_key(scores(jnp.concatenate([k_refs[j][0] for j in range(g)], axis=1)))
        for j in range(g):
            s_ref[p * g + j] = keys[:, j * LANES:(j + 1) * LANES]

    @pl.when(p == n_steps)
    def _new():
        sc = scores(knew_ref[0])
        i = lax.broadcasted_iota(I32, (rows, LANES), 0)
        lane = lax.broadcasted_iota(I32, (rows, LANES), 1)
        s_ref[n_pages] = _sort_key(jnp.where(lane <= i, sc, -jnp.inf))

        def idx_of(c, grp, nr):
            return lane + c * page

        n_keys = (n_pages + 1) * page
        thr, cut = _select_topk(lambda c, r0, nr: s_ref[c], n_pages + 1, 1, rows, nsel,
                                int(math.log2(n_keys)) + 1, idx_of)

        def selected(c):
            k = s_ref[c]
            sel = (k > thr) | ((k == thr) & (lane + c * page <= cut))
            return jnp.where(sel, 0.0, NEG)

        def write_body(c, carry):
            bias_ref[0, c] = selected(c)
            return carry

        lax.fori_loop(0, n_pages, write_body, 0)
        biasn_ref[0] = selected(n_pages)


def _idx_decode2(page_table, q_st, w_st, kt_pool, kt_new, nsel):
    b, n_pages = page_table.shape
    page = kt_pool.shape[2]
    n_new = q_st.shape[1] // N_IDX_HEADS
    g = _pages_per_step(n_pages)
    n_steps = n_pages // g

    def req_map(bi, p, pt):
        return (bi, 0, 0)

    pool_specs = [pl.BlockSpec((1, IDX_DIM, page),
                               lambda bi, p, pt, j=j: (pt[bi, jnp.minimum(p, n_steps - 1) * g + j], 0, 0))
                  for j in range(g)]
    grid_spec = pltpu.PrefetchScalarGridSpec(
        num_scalar_prefetch=1,
        grid=(b, n_steps + 1),
        in_specs=[pl.BlockSpec((1,) + q_st.shape[1:], req_map), pl.BlockSpec((1,) + w_st.shape[1:], req_map)]
        + pool_specs + [pl.BlockSpec((1, IDX_DIM, page), req_map)],
        out_specs=[pl.BlockSpec((1, n_pages, n_new, LANES), lambda bi, p, pt: (bi, 0, 0, 0)),
                   pl.BlockSpec((1, n_new, LANES), req_map)],
        scratch_shapes=[pltpu.VMEM((n_pages + 1, n_new, LANES), I32)],
    )
    return pl.pallas_call(
        functools.partial(_idx_decode_kernel2, g=g, n_pages=n_pages, n_new=n_new, nsel=nsel, page=page),
        grid_spec=grid_spec,
        out_shape=(jax.ShapeDtypeStruct((b, n_pages, n_new, LANES), F32),
                   jax.ShapeDtypeStruct((b, n_new, LANES), F32)),
        compiler_params=_cparams(2),
        name="idx_decode",
    )(page_table, q_st, w_st, *([kt_pool] * g), kt_new)


def _attn_decode_kernel2(*refs, variant, g, n_units, n_new, n_pages):
    it = iter(refs)
    next(it)
    q_ref = next(it)
    k_refs = [next(it) for _ in range(g)]
    v_refs = [next(it) for _ in range(g)]
    knew_ref, vnew_ref = next(it), next(it)
    bias_ref, biasn_ref = (next(it), next(it)) if variant == "dsa" else (None, None)
    cq_ref, ck_ref, cn_ref = (next(it), next(it), next(it)) if variant == "fox" else (None, None, None)
    lam_ref = next(it) if variant == "diff" else None
    o_ref = next(it)
    qbd_ref, m_ref, l_ref, acc_ref = next(it), next(it), next(it), next(it)

    p = pl.program_id(1)
    n_steps = n_pages // g
    rows = n_units * n_new
    w = q_ref.shape[-1]
    wa = acc_ref.shape[-1]

    @pl.when(p == 0)
    def _init():
        q = q_ref[0].astype(F32)
        qt = jnp.concatenate([q] * n_units, axis=0)
        r = _div_pow2(lax.broadcasted_iota(I32, (rows, w), 0), n_new)
        c = _div_pow2(lax.broadcasted_iota(I32, (rows, w), 1), HEAD_DIM)
        qbd_ref[...] = jnp.where(r == c, qt, 0.0).astype(BF16)
        m_ref[...] = jnp.full(m_ref.shape, NEG, F32)
        l_ref[...] = jnp.zeros(l_ref.shape, F32)
        acc_ref[...] = jnp.zeros(acc_ref.shape, F32)

    def expand_rows(x8):
        return jnp.concatenate([x8] * n_units, axis=0)

    def expand_units(xu):
        return jnp.concatenate([jnp.broadcast_to(xu[u:u + 1, :], (n_new, xu.shape[1])) for u in range(n_units)],
                               axis=0)

    def v_head(v_ref, h, is_new):
        if is_new or len(v_ref.shape) == 3:
            return v_ref[0, :, h * LANES:(h + 1) * LANES]
        return v_ref[0, :, h, :]

    def step(kts, vs, bias, ck, is_new):
        ng = len(kts)
        kt_all = kts[0][0] if ng == 1 else jnp.concatenate([kts[j][0] for j in range(ng)], axis=1)
        s = jnp.dot(qbd_ref[...], kt_all.astype(BF16), preferred_element_type=F32)
        if variant == "dsa":
            s = s + bias
        if variant == "fox":
            s = s + (jnp.tile(cq_ref[0], (1, ng)) - expand_units(ck))
        if is_new:
            i = expand_rows(lax.broadcasted_iota(I32, (n_new, LANES), 0))
            lane = lax.broadcasted_iota(I32, (rows, LANES), 1)
            s = jnp.where(lane <= i, s, NEG)
        m_prev = m_ref[...]
        m_new = jnp.maximum(m_prev, jnp.max(s, axis=-1, keepdims=True))
        alpha = jnp.exp(m_prev - m_new)
        pr = jnp.exp(s - jnp.tile(m_new, (1, ng)))
        l_ref[...] = alpha * l_ref[...] + jnp.sum(pr, axis=-1, keepdims=True)
        m_ref[...] = m_new
        p16 = pr.astype(BF16)
        if variant == "diff":
            hr = 2 * n_new
            pv = jnp.concatenate(
                [jnp.dot(p16[h * hr:(h + 1) * hr],
                         jnp.concatenate([v_head(vs[j], h, is_new) for j in range(ng)], axis=0).astype(BF16),
                         preferred_element_type=F32) for h in range(n_units // 2)], axis=0)
        else:
            vt_all = vs[0][0] if ng == 1 else jnp.concatenate([vs[j][0] for j in range(ng)], axis=1)
            pv = lax.dot_general(p16, vt_all.astype(BF16), NT_DIMS, preferred_element_type=F32)
        acc_ref[...] = jnp.tile(alpha, (1, wa // LANES)) * acc_ref[...] + pv

    @pl.when(p < n_steps)
    def _past():
        bias = ck = None
        if variant == "dsa":
            bias = jnp.concatenate([expand_rows(bias_ref[0, j]) for j in range(g)], axis=1)
        if variant == "fox":
            ck = ck_ref[0]
        step(k_refs, v_refs, bias, ck, False)

    @pl.when(p == n_steps)
    def _new():
        bias = expand_rows(biasn_ref[0]) if variant == "dsa" else None
        ck = cn_ref[0] if variant == "fox" else None
        step([knew_ref], [vnew_ref], bias, ck, True)
        accn = acc_ref[...] / jnp.tile(l_ref[...], (1, wa // LANES))
        if variant == "diff":
            lam = lam_ref[0, 0]
            o_ref[0] = jnp.concatenate(
                [accn[(2 * h) * n_new:(2 * h + 1) * n_new] - lam * accn[(2 * h + 1) * n_new:(2 * h + 2) * n_new]
                 for h in range(n_units // 2)], axis=1)
        else:
            cu = _div_pow2(lax.broadcasted_iota(I32, (n_new, wa), 1), HEAD_DIM)
            out = jnp.zeros((n_new, wa), F32)
            for u in range(n_units):
                out = out + jnp.where(cu == u, accn[u * n_new:(u + 1) * n_new], 0.0)
            o_ref[0] = out


def _attn_decode2(variant, page_table, q16, kt_pool, v_pool, kt_new, v_new,
                  bias=None, bias_new=None, cq=None, ck=None, cn=None, lam=None):
    b, n_pages = page_table.shape
    page = kt_pool.shape[2]
    n_new, w = q16.shape[1], q16.shape[2]
    n_units = w // HEAD_DIM
    rows = n_units * n_new
    g = _pages_per_step(n_pages)
    n_steps = n_pages // g
    wa = LANES if variant == "diff" else w

    def req_map(bi, p, pt):
        return (bi, 0, 0)

    def pool_spec(arr, j):
        nd = arr.ndim
        return pl.BlockSpec((1,) + arr.shape[1:],
                            lambda bi, p, pt: (pt[bi, jnp.minimum(p, n_steps - 1) * g + j],) + (0,) * (nd - 1))

    in_specs = ([pl.BlockSpec((1, n_new, w), req_map)]
                + [pool_spec(kt_pool, j) for j in range(g)] + [pool_spec(v_pool, j) for j in range(g)]
                + [pl.BlockSpec((1,) + kt_new.shape[1:], req_map), pl.BlockSpec((1,) + v_new.shape[1:], req_map)])
    args = [q16] + [kt_pool] * g + [v_pool] * g + [kt_new, v_new]
    if variant == "dsa":
        in_specs.append(pl.BlockSpec((1, g, n_new, LANES),
                                     lambda bi, p, pt: (bi, jnp.minimum(p, n_steps - 1), 0, 0)))
        in_specs.append(pl.BlockSpec((1, n_new, LANES), req_map))
        args += [bias, bias_new]
    if variant == "fox":
        in_specs.append(pl.BlockSpec((1, rows, LANES), req_map))
        in_specs.append(pl.BlockSpec((1, n_units, g * LANES), lambda bi, p, pt: (bi, 0, jnp.minimum(p, n_steps - 1))))
        in_specs.append(pl.BlockSpec((1, n_units, LANES), req_map))
        args += [cq, ck, cn]
    if variant == "diff":
        in_specs.append(pl.BlockSpec(memory_space=pltpu.SMEM))
        args.append(lam)
    wo = v_new.shape[2] if variant == "diff" else w
    grid_spec = pltpu.PrefetchScalarGridSpec(
        num_scalar_prefetch=1,
        grid=(b, n_steps + 1),
        in_specs=in_specs,
        out_specs=pl.BlockSpec((1, n_new, wo), req_map),
        scratch_shapes=[pltpu.VMEM((rows, w), BF16), pltpu.VMEM((rows, LANES), F32),
                        pltpu.VMEM((rows, LANES), F32), pltpu.VMEM((rows, wa), F32)],
    )
    return pl.pallas_call(
        functools.partial(_attn_decode_kernel2, variant=variant, g=g, n_units=n_units, n_new=n_new,
                          n_pages=n_pages),
        grid_spec=grid_spec,
        out_shape=jax.ShapeDtypeStruct((b, n_new, wo), F32),
        compiler_params=_cparams(2),
        name="attn_decode_" + variant,
    )(page_table, *args)


def _feature_major_pages(cache):
    n_pool, page = cache.shape[0], cache.shape[1]
    perm = (0,) + tuple(range(2, cache.ndim)) + (1,)
    return jnp.transpose(cache, perm).reshape(n_pool, -1, page)


def _feature_major_new(x3, page):
    xt = jnp.swapaxes(x3, 1, 2)
    return jnp.pad(xt, ((0, 0), (0, 0), (0, page - xt.shape[2])))


def _diff_lambda(p, lam_init):
    def e(a, c):
        return jnp.exp(jnp.sum(a.astype(F32) * c.astype(F32)))
    return (e(p["lam_q1"], p["lam_k1"]) - e(p["lam_q2"], p["lam_k2"]) + lam_init).reshape(1, 1).astype(F32)


def _pad_rows(x, rows):
    return jnp.pad(x, ((0, 0), (0, rows - x.shape[1]), (0, 0)))


def _tile_for(m, pref):
    t = min(m, pref)
    while m % t:
        t //= 2
    return t


def _even_prompt(x, layer, p, tiles):
    b, seq, d = x.shape
    tm, tq, tk = tiles
    pos = jnp.tile(jnp.arange(seq), b)
    (qa, ka, ka16, va, va16, ga, qi, ki, ki16, wi, qb, kb, kb16, vb, vb16, gb) = _proj0(
        x.reshape(b * seq, d), pos, p, tm)
    lam_init = 0.8 - 0.6 * math.exp(-0.3 * layer)
    lam = _diff_lambda(p, lam_init)
    nsel = min(TOPK_MAX, seq // 4)
    r3 = lambda t: t.reshape(b, seq, t.shape[-1])
    w_in = p["w_in"]
    col = lambda start, width: w_in[:, start:start + width]
    w5 = WIDTH_A
    o_i, o_b0 = 4 * w5, 5 * w5 + IDX_DIM + N_IDX_HEADS
    qat, vat, qit, qbt, vbt, wit = _proj_t(
        x.reshape(b * seq, d), p["norm"],
        [col(0, w5), col(2 * w5, w5), col(o_i, w5), col(o_b0, w5), col(o_b0 + 2 * w5, w5)],
        [(0, True, QK_SCALE * LOG2E), (None, False, 1.0), (None, True, 1.0), (1, True, QK_SCALE * LOG2E),
         (None, False, 1.0)],
        [p["qn_a"], p["qn_b"]], pos, col(o_i + w5 + IDX_DIM, N_IDX_HEADS), tm)
    bias = _idx_prompt_t(qit, wit, r3(ki16), b, nsel, tq, tk)
    o_a = _flash_t("dsa", r3(ka16), qat, vat, tq, tk, bias=bias)
    o_b = _flash_t("diff", r3(kb16), qbt, vbt, tq, tk, lam=lam)
    y = _out0(x.reshape(b * seq, d), o_a.reshape(b * seq, -1), ga, o_b.reshape(b * seq, -1), gb, p, lam_init, tm)
    new = (ka.reshape(b, seq, N_HEADS_A, HEAD_DIM), va.reshape(b, seq, N_HEADS_A, HEAD_DIM),
           ki.reshape(b, seq, IDX_DIM), kb.reshape(b, seq, N_HEADS_B, 2, HEAD_DIM),
           vb.reshape(b, seq, N_HEADS_B, 2 * HEAD_DIM))
    return y.reshape(b, seq, d), new


def _even_sample(x, layer, caches, page_table, p):
    cache_k_a, cache_v_a, cache_k_i, cache_k_b, cache_v_b = caches
    b, n_new, d = x.shape
    n_pages = page_table.shape[1]
    n_pool, page = cache_k_a.shape[0], cache_k_a.shape[1]
    past = n_pages * page
    pos = jnp.tile(past + jnp.arange(n_new), b)
    m = b * n_new
    (qa, ka, _ka16, va, _va16, ga, qi, ki, _ki16, wi, qb, kb, _kb16, vb, _vb16, gb) = _proj0(
        x.reshape(m, d), pos, p, _tile_for(m, 256))
    lam_init = 0.8 - 0.6 * math.exp(-0.3 * layer)
    lam = _diff_lambda(p, lam_init)
    nsel = min(TOPK_MAX, (past + n_new) // 4)
    r3 = lambda t: t.reshape(b, n_new, t.shape[-1])
    q_st = jnp.swapaxes(qi.reshape(b, n_new, N_IDX_HEADS, IDX_DIM), 1, 2).reshape(b, N_IDX_HEADS * n_new, IDX_DIM)
    w_st = jnp.swapaxes(wi.reshape(b, n_new, N_IDX_HEADS), 1, 2).reshape(b, N_IDX_HEADS * n_new, 1)
    w_st = jnp.broadcast_to(w_st, (b, N_IDX_HEADS * n_new, LANES))
    bias, bias_new = _idx_decode2(page_table, q_st, w_st, _feature_major_pages(cache_k_i),
                                  _feature_major_new(r3(ki), page), nsel)
    o_a = _attn_decode2("dsa", page_table, r3(qa), _feature_major_pages(cache_k_a), _feature_major_pages(cache_v_a),
                        _feature_major_new(r3(ka), page), _feature_major_new(r3(va), page),
                        bias=bias, bias_new=bias_new)
    o_b = _attn_decode2("diff", page_table, r3(qb), _feature_major_pages(cache_k_b), cache_v_b,
                        _feature_major_new(r3(kb), page), _pad_rows(r3(vb), page), lam=lam)
    y = _out0(x.reshape(m, d), o_a.reshape(m, -1), ga, o_b.reshape(m, -1), gb, p, lam_init, _tile_for(m, 256))
    new = (ka.reshape(b, n_new, N_HEADS_A, HEAD_DIM), va.reshape(b, n_new, N_HEADS_A, HEAD_DIM),
           ki.reshape(b, n_new, IDX_DIM), kb.reshape(b, n_new, N_HEADS_B, 2, HEAD_DIM),
           vb.reshape(b, n_new, N_HEADS_B, 2 * HEAD_DIM))
    return y.reshape(b, n_new, d), new


def _odd_prompt(x, p, tiles):
    b, seq, d = x.shape
    tm, tq, tk = tiles
    q16, k, k16, v, v16, gate, logf = _proj1(x.reshape(b * seq, d), p, tm)
    r3 = lambda t: t.reshape(b, seq, t.shape[-1])
    n_blk = seq // LANES
    logf_t = jnp.swapaxes(logf.reshape(b * n_blk, LANES, N_HEADS_C), 1, 2)
    ident = jnp.arange(b * n_blk, dtype=I32).reshape(b, n_blk)
    c_t, _ = _cumsum_pages2(ident, logf_t, jnp.zeros((b, N_HEADS_C, LANES), F32))
    c_tok = jnp.swapaxes(c_t, 1, 2)
    w_in = p["w_in"]
    qt, vt = _proj_t(x.reshape(b * seq, d), p["norm"], [w_in[:, 0:WIDTH_C], w_in[:, 2 * WIDTH_C:3 * WIDTH_C]],
                     [(0, False, QK_SCALE * LOG2E), (None, False, 1.0)], [p["qn"]], None, None, tm)
    o = _flash_t("fox", r3(k16), qt, vt, tq, tk, cqt=c_t, ck=c_tok)
    y = _out1(x.reshape(b * seq, d), o.reshape(b * seq, -1), gate, p, tm)
    new = (k.reshape(b, seq, N_HEADS_C, HEAD_DIM), v.reshape(b, seq, N_HEADS_C, HEAD_DIM),
           logf.reshape(b, seq, N_HEADS_C))
    return y.reshape(b, seq, d), new


def _odd_sample(x, caches, page_table, p):
    cache_k, cache_v, cache_logf = caches
    b, n_new, d = x.shape
    n_pool, page = cache_k.shape[0], cache_k.shape[1]
    m = b * n_new
    q16, k, _k16, v, _v16, gate, logf = _proj1(x.reshape(m, d), p, _tile_for(m, 256))
    r3 = lambda t: t.reshape(b, n_new, t.shape[-1])
    c_past_t, c_new_t = _cumsum_pages2(page_table, _feature_major_pages(cache_logf.astype(F32)),
                                       _feature_major_new(r3(logf), page))
    cq = jnp.broadcast_to(c_new_t[:, :, :n_new].reshape(b, N_HEADS_C * n_new, 1), (b, N_HEADS_C * n_new, LANES))
    o = _attn_decode2("fox", page_table, r3(q16), _feature_major_pages(cache_k), _feature_major_pages(cache_v),
                      _feature_major_new(r3(k), page), _feature_major_new(r3(v), page),
                      cq=cq, ck=c_past_t, cn=c_new_t)
    y = _out1(x.reshape(m, d), o.reshape(m, -1), gate, p, _tile_for(m, 256))
    new = (k.reshape(b, n_new, N_HEADS_C, HEAD_DIM), v.reshape(b, n_new, N_HEADS_C, HEAD_DIM),
           logf.reshape(b, n_new, N_HEADS_C))
    return y.reshape(b, n_new, d), new


def kernel(x_prompt, x_sample, cache_l0_k_a, cache_l0_v_a, cache_l0_k_idx, cache_l0_k_b, cache_l0_v_b,
           cache_l1_k_c, cache_l1_v_c, cache_l1_logf_c, page_table,
           l0_norm, l0_w_in, l0_qn_a, l0_kn_a, l0_kn_idx, l0_qn_b, l0_kn_b,
           l0_lam_q1, l0_lam_k1, l0_lam_q2, l0_lam_k2, l0_subln_b, l0_w_out,
           l1_norm, l1_w_in, l1_b_f, l1_qn, l1_kn, l1_w_out):
    p0 = dict(norm=l0_norm, w_in=l0_w_in, qn_a=l0_qn_a, kn_a=l0_kn_a, kn_idx=l0_kn_idx, qn_b=l0_qn_b, kn_b=l0_kn_b,
              lam_q1=l0_lam_q1, lam_k1=l0_lam_k1, lam_q2=l0_lam_q2, lam_k2=l0_lam_k2, subln_b=l0_subln_b,
              w_out=l0_w_out)
    p1 = dict(norm=l1_norm, w_in=l1_w_in, b_f=l1_b_f, qn=l1_qn, kn=l1_kn, w_out=l1_w_out)
    b, seq, _ = x_prompt.shape
    tiles = (_tile_for(b * seq, ROW_TILE), _tile_for(seq, Q_TILE), _tile_for(seq, K_TILE))
    page_table = page_table.astype(I32)
    xp, sp0 = _even_prompt(x_prompt, 0, p0, tiles)
    xs, ss0 = _even_sample(x_sample, 0, (cache_l0_k_a, cache_l0_v_a, cache_l0_k_idx, cache_l0_k_b, cache_l0_v_b),
                           page_table, p0)
    xp, sp1 = _odd_prompt(xp, p1, tiles)
    xs, ss1 = _odd_sample(xs, (cache_l1_k_c, cache_l1_v_c, cache_l1_logf_c), page_table, p1)
    (p_k_a, p_v_a, p_k_idx, p_k_b, p_v_b), (p_k_c, p_v_c, p_logf_c) = sp0, sp1
    (s_k_a, s_v_a, s_k_idx, s_k_b, s_v_b), (s_k_c, s_v_c, s_logf_c) = ss0, ss1
    return (xp, xs, p_k_a, s_k_a, p_v_a, s_v_a, p_k_idx, s_k_idx, p_k_b, s_k_b, p_v_b, s_v_b,
            p_k_c, s_k_c, p_v_c, s_v_c, p_logf_c, s_logf_c)
```

```python
import functools
import math

import jax
import jax.numpy as jnp
from jax import lax
from jax.experimental import pallas as pl
from jax.experimental.pallas import tpu as pltpu

F32 = jnp.float32
BF16 = jnp.bfloat16
I32 = jnp.int32

HEAD_DIM = 64
ROPE_THETA = 500000.0
N_HEADS_A = 8
N_IDX_HEADS = 8
IDX_DIM = 64
TOPK_MAX = 256
N_HEADS_B = 4
N_HEADS_C = 16
EPS = 1e-6
WIDTH_A = N_HEADS_A * HEAD_DIM
WIDTH_B = N_HEADS_B * 2 * HEAD_DIM
WIDTH_C = N_HEADS_C * HEAD_DIM
WIDTH_I = N_IDX_HEADS * IDX_DIM
QK_SCALE = HEAD_DIM ** -0.5
LOG2E = 1.4426950408889634

LANES = 128
SUBLANES = 8
MXU_DIM = 256
VMEM_LIMIT = 56 * 1024 * 1024
ROW_TILE = 256
Q_TILE = 512
IDX_Q_TILE = 256
K_TILE = 512
COUNT_ROWS = 64
COUNT_PARTS = 8
FLASH_SUB = 512
FLASH_GROUP = 8

NEG = -1e30
NT_DIMS = (((1,), (1,)), ((), ()))

KEY_NEG_INF = -2139095041


def _cparams(n_axes):
    return pltpu.CompilerParams(
        dimension_semantics=("arbitrary",) * n_axes, vmem_limit_bytes=VMEM_LIMIT)


def _div_pow2(x, d):
    assert d & (d - 1) == 0
    return lax.shift_right_logical(x, jnp.int32(d.bit_length() - 1))


def _sort_key(x):
    bits = pltpu.bitcast(x, I32)
    return bits ^ ((bits >> 31) & jnp.int32(0x7FFFFFFF))


def _rms_rows(x, g):
    ms = jnp.mean(x * x, axis=-1, keepdims=True)
    return x * lax.rsqrt(ms + EPS) * g


def _split2(x):
    hi = x.astype(BF16)
    return hi, (x - hi.astype(F32)).astype(BF16)


def _split3(x):
    hi = x.astype(BF16)
    r = x - hi.astype(F32)
    mid = r.astype(BF16)
    return hi, mid, (r - mid.astype(F32)).astype(BF16)


def _head_norm(h, bd, gain):
    w = h.shape[-1]
    hi, lo = _split2(h * h)
    bd16 = bd.astype(BF16)
    if w >= MXU_DIM:
        cols = [jnp.dot(hi[:, c * MXU_DIM:(c + 1) * MXU_DIM], bd16, preferred_element_type=F32)
                + jnp.dot(lo[:, c * MXU_DIM:(c + 1) * MXU_DIM], bd16, preferred_element_type=F32)
                for c in range(w // MXU_DIM)]
        ms = cols[0] if len(cols) == 1 else jnp.concatenate(cols, axis=-1)
    else:
        ms = (jnp.dot(hi, bd16[:w, :w], preferred_element_type=F32)
              + jnp.dot(lo, bd16[:w, :w], preferred_element_type=F32))
    return h * lax.rsqrt(ms + EPS) * gain


def _rope(y, rope_ref):
    c = rope_ref[:, 0:LANES]
    s_lo = rope_ref[:, LANES:2 * LANES]
    s_hi = rope_ref[:, 2 * LANES:3 * LANES]
    outs = []
    for j in range(y.shape[-1] // LANES):
        yc = y[:, j * LANES:(j + 1) * LANES]
        outs.append(yc * c + pltpu.roll(yc, LANES - 8, 1) * s_lo + pltpu.roll(yc, 8, 1) * s_hi)
    return outs[0] if len(outs) == 1 else jnp.concatenate(outs, axis=-1)


def _proj0_kernel(*refs, want):
    x_ref, g_ref, w_ref, ws_ref, rope_ref, gains_ref, gki_ref, bd_ref = refs[:8]
    o = dict(zip(want, refs[8:]))
    xb = _rms_rows(x_ref[...], g_ref[...]).astype(BF16)
    bd = bd_ref[...]
    w512 = WIDTH_A

    def piece(j):
        return jnp.dot(xb, w_ref[:, j * w512:(j + 1) * w512], preferred_element_type=F32)

    def put(name, val):
        if name in o:
            o[name][...] = val.astype(o[name].dtype)

    def needs(*names):
        return any(n in o for n in names)

    if needs("qa"):
        put("qa", _rope(_head_norm(piece(0), bd, gains_ref[0:1, :]), rope_ref) * QK_SCALE)
    if needs("ka", "ka16"):
        k_a = _rope(_head_norm(piece(1), bd, gains_ref[1:2, :]), rope_ref)
        put("ka", k_a)
        put("ka16", k_a)
    if needs("va"):
        put("va", piece(2))
    if needs("ga"):
        put("ga", piece(3))
    if needs("qi"):
        put("qi", _rope(piece(4), rope_ref))
    if needs("qb"):
        put("qb", _rope(_head_norm(piece(5), bd, gains_ref[2:3, :]), rope_ref) * QK_SCALE)
    if needs("kb", "kb16"):
        k_b = _rope(_head_norm(piece(6), bd, gains_ref[3:4, :]), rope_ref)
        put("kb", k_b)
        put("kb16", k_b)
    if needs("vb"):
        put("vb", piece(7))
    if needs("gb"):
        put("gb", piece(8))
    if needs("ki", "ki16", "wi"):
        hs = jnp.dot(xb, ws_ref[...], preferred_element_type=F32)
        k_i = _rope(_head_norm(hs[:, 0:LANES], bd, gki_ref[...]), rope_ref)
        put("ki", k_i[:, 0:IDX_DIM])
        put("ki16", k_i)
        put("wi", hs[:, LANES:LANES + N_IDX_HEADS] * (WIDTH_I ** -0.5))


def _proj1_kernel(*refs, want):
    x_ref, g_ref, w_ref, wf_ref, bf_ref, gains_ref, bd_ref = refs[:7]
    o = dict(zip(want, refs[7:]))
    xb = _rms_rows(x_ref[...], g_ref[...]).astype(BF16)
    bd = bd_ref[...]
    wc = WIDTH_C

    def piece(j):
        return jnp.dot(xb, w_ref[:, j * wc:(j + 1) * wc], preferred_element_type=F32)

    def put(name, val):
        if name in o:
            o[name][...] = val.astype(o[name].dtype)

    if "q16" in o:
        put("q16", _head_norm(piece(0), bd, gains_ref[0:1, :]) * QK_SCALE)
    if "k" in o or "k16" in o:
        k = _head_norm(piece(1), bd, gains_ref[1:2, :])
        put("k", k)
        put("k16", k)
    if "v" in o:
        put("v", piece(2))
    if "gate" in o:
        put("gate", piece(3))
    if "logf" in o:
        f = jnp.dot(xb, wf_ref[...], preferred_element_type=F32)[:, 0:N_HEADS_C] + bf_ref[...]
        put("logf", _log_sigmoid(f))


def _row_spec(tm, w):
    return pl.BlockSpec((tm, w), lambda i: (i, 0))


def _const_spec(shape):
    return pl.BlockSpec(shape, lambda i: (0,) * len(shape))


def _block_diag_mean():
    r = lax.broadcasted_iota(I32, (MXU_DIM, MXU_DIM), 0) // HEAD_DIM
    c = lax.broadcasted_iota(I32, (MXU_DIM, MXU_DIM), 1) // HEAD_DIM
    return jnp.where(r == c, 1.0 / HEAD_DIM, 0.0).astype(F32)


def _rope_table(pos):
    rot = HEAD_DIM // 4
    half = rot // 2
    inv = jnp.power(F32(ROPE_THETA), -jnp.arange(half, dtype=F32) * (2.0 / rot))
    ang = pos.astype(F32)[:, None] * inv[None, :]
    cos, sin = jnp.cos(ang), jnp.sin(ang)
    m = pos.shape[0]
    ones = jnp.ones((m, HEAD_DIM - rot), F32)
    zeros = jnp.zeros((m, HEAD_DIM - rot), F32)
    zh = jnp.zeros((m, half), F32)
    c = jnp.concatenate([cos, cos, ones], axis=-1)
    s_lo = jnp.concatenate([-sin, zh, zeros], axis=-1)
    s_hi = jnp.concatenate([zh, sin, zeros], axis=-1)
    return jnp.concatenate([c, c, s_lo, s_lo, s_hi, s_hi], axis=-1)


def _tile_gain(g, w):
    return jnp.tile(g.astype(F32), w // g.shape[0])[None, :]


P0_OUTS = dict(qa=(WIDTH_A, BF16), ka=(WIDTH_A, F32), ka16=(WIDTH_A, BF16), va=(WIDTH_A, F32), ga=(WIDTH_A, F32),
               qi=(WIDTH_I, BF16), ki=(IDX_DIM, F32), ki16=(LANES, BF16), wi=(N_IDX_HEADS, F32),
               qb=(WIDTH_B, BF16), kb=(WIDTH_B, F32), kb16=(WIDTH_B, BF16), vb=(WIDTH_B, F32), gb=(WIDTH_B, F32))
P1_OUTS = dict(q16=(WIDTH_C, BF16), k=(WIDTH_C, F32), k16=(WIDTH_C, BF16), v=(WIDTH_C, F32), gate=(WIDTH_C, F32),
               logf=(N_HEADS_C, F32))


def _even_splits():
    sizes = (WIDTH_A,) * 4 + (WIDTH_I, IDX_DIM, N_IDX_HEADS) + (WIDTH_B,) * 4
    offs = [0]
    for s in sizes:
        offs.append(offs[-1] + s)
    return offs


def _proj0(x2d, pos, p, tm, want):
    m, d = x2d.shape
    offs = _even_splits()
    w_in = p["w_in"]
    cols = [w_in[:, offs[i]:offs[i + 1]] for i in range(len(offs) - 1)]
    w_big = jnp.concatenate([cols[0], cols[1], cols[2], cols[3], cols[4], cols[7], cols[8], cols[9], cols[10]],
                            axis=1).astype(BF16)
    w_small = jnp.concatenate(
        [cols[5], cols[5], cols[6], jnp.zeros((d, LANES - N_IDX_HEADS), w_in.dtype)], axis=1).astype(BF16)
    gains = jnp.concatenate([_tile_gain(p["qn_a"], WIDTH_A), _tile_gain(p["kn_a"], WIDTH_A),
                             _tile_gain(p["qn_b"], WIDTH_B), _tile_gain(p["kn_b"], WIDTH_B)], axis=0)
    gki = _tile_gain(p["kn_idx"], LANES)
    rope = _rope_table(pos)
    out_shape = tuple(jax.ShapeDtypeStruct((m, P0_OUTS[n][0]), P0_OUTS[n][1]) for n in want)
    out_specs = tuple(_row_spec(tm, s.shape[1]) for s in out_shape)
    outs = pl.pallas_call(
        functools.partial(_proj0_kernel, want=tuple(want)),
        grid=(m // tm,),
        in_specs=[_row_spec(tm, d), _const_spec((1, d)), _const_spec(w_big.shape), _const_spec(w_small.shape),
                  _row_spec(tm, 3 * LANES), _const_spec(gains.shape), _const_spec(gki.shape),
                  _const_spec((MXU_DIM, MXU_DIM))],
        out_specs=out_specs,
        out_shape=out_shape,
        compiler_params=_cparams(1),
        name="proj0",
    )(x2d, p["norm"].astype(F32)[None, :], w_big, w_small, rope, gains, gki, _block_diag_mean())
    return dict(zip(want, outs))


def _proj1(x2d, p, tm, want):
    m, d = x2d.shape
    wc = WIDTH_C
    w_in = p["w_in"]
    w_big = w_in[:, :4 * wc].astype(BF16)
    w_f = jnp.concatenate([w_in[:, 4 * wc:], jnp.zeros((d, LANES - N_HEADS_C), w_in.dtype)], axis=1).astype(BF16)
    gains = jnp.concatenate([_tile_gain(p["qn"], wc), _tile_gain(p["kn"], wc)], axis=0)
    out_shape = tuple(jax.ShapeDtypeStruct((m, P1_OUTS[n][0]), P1_OUTS[n][1]) for n in want)
    out_specs = tuple(_row_spec(tm, s.shape[1]) for s in out_shape)
    outs = pl.pallas_call(
        functools.partial(_proj1_kernel, want=tuple(want)),
        grid=(m // tm,),
        in_specs=[_row_spec(tm, d), _const_spec((1, d)), _const_spec(w_big.shape), _const_spec(w_f.shape),
                  _const_spec((1, N_HEADS_C)), _const_spec(gains.shape), _const_spec((MXU_DIM, MXU_DIM))],
        out_specs=out_specs,
        out_shape=out_shape,
        compiler_params=_cparams(1),
        name="proj1",
    )(x2d, p["norm"].astype(F32)[None, :], w_big, w_f, p["b_f"].astype(F32)[None, :], gains, _block_diag_mean())
    return dict(zip(want, outs))


def _log_sigmoid(f):
    return jnp.minimum(f, 0.0) - jnp.log1p(jnp.exp(-jnp.abs(f)))


def _head_norm_t(h, bd, gain):
    w, tm = h.shape
    hi, lo = _split2(h * h)
    bd16 = bd.astype(BF16)
    if w >= MXU_DIM:
        slabs = [jnp.dot(bd16, hi[c * MXU_DIM:(c + 1) * MXU_DIM, :], preferred_element_type=F32)
                 + jnp.dot(bd16, lo[c * MXU_DIM:(c + 1) * MXU_DIM, :], preferred_element_type=F32)
                 for c in range(w // MXU_DIM)]
        ms = slabs[0] if len(slabs) == 1 else jnp.concatenate(slabs, axis=0)
    else:
        ms = (jnp.dot(bd16[:w, :w], hi, preferred_element_type=F32)
              + jnp.dot(bd16[:w, :w], lo, preferred_element_type=F32))
    return h * lax.rsqrt(ms + EPS) * jnp.tile(gain, (1, tm // LANES))


def _rope_t(y, rope_ref):
    half = HEAD_DIM // 8
    cos = rope_ref[0:half, :]
    sin = rope_ref[half:2 * half, :]
    parts = []
    for h in range(y.shape[0] // HEAD_DIM):
        r0 = h * HEAD_DIM
        x1 = y[r0:r0 + half, :]
        x2 = y[r0 + half:r0 + 2 * half, :]
        parts += [x1 * cos - x2 * sin, x2 * cos + x1 * sin, y[r0 + 2 * half:r0 + HEAD_DIM, :]]
    return jnp.concatenate(parts, axis=0)


def _proj_t_kernel(*refs, specs, pw, small):
    x_ref, g_ref, wt_ref, ws_ref, rope_ref, gains_ref, gs_ref, bd_ref = refs[:8]
    outs = list(refs[8:])
    xb = _rms_rows(x_ref[...], g_ref[...]).astype(BF16)
    bd = bd_ref[...]
    tm = xb.shape[0]
    k = 0
    for j, (gain_row, rope, scale, kinds) in enumerate(specs):
        h = lax.dot_general(wt_ref[j * pw:(j + 1) * pw, :], xb, NT_DIMS, preferred_element_type=F32)
        if gain_row is not None:
            h = _head_norm_t(h, bd, gains_ref[gain_row])
        if rope:
            h = _rope_t(h, rope_ref)
        if scale != 1.0:
            h = h * scale
        for dt in kinds:
            outs[k][0] = h.astype(dt)
            k += 1
    if small == "idx":
        hs = lax.dot_general(ws_ref[...], xb, NT_DIMS, preferred_element_type=F32)
        outs[k][0] = _rope_t(_head_norm_t(hs[0:IDX_DIM, :], bd, gs_ref[...]), rope_ref)
        outs[k + 1][0] = hs[IDX_DIM:IDX_DIM + N_IDX_HEADS, :] * (WIDTH_I ** -0.5)
    if small == "logf":
        f = lax.dot_general(ws_ref[...], xb, NT_DIMS, preferred_element_type=F32)
        outs[k][0] = _log_sigmoid(f + jnp.tile(gs_ref[...], (1, tm // LANES)))


def _rope_table_t(pos):
    rot = HEAD_DIM // 4
    half = rot // 2
    inv = jnp.power(F32(ROPE_THETA), -jnp.arange(half, dtype=F32) * (2.0 / rot))
    ang = inv[:, None] * pos.astype(F32)[None, :]
    return jnp.concatenate([jnp.cos(ang), jnp.sin(ang)], axis=0)


def _lane_replicated(col):
    return jnp.broadcast_to(col.astype(F32)[:, None], (col.shape[0], LANES))


def _proj_t(x3, norm, blocks, gains, pos, small, tm):
    b, seq, d = x3.shape
    m = b * seq
    nt = seq // tm
    pw = blocks[0][0].shape[1]
    wt = jnp.concatenate([blk[0].T for blk in blocks], axis=0).astype(BF16)
    specs = tuple(tuple(blk[1:]) for blk in blocks)
    gains_arr = (jnp.zeros((1, pw, LANES), F32) if not gains else
                 jnp.stack([_lane_replicated(jnp.tile(g, pw // g.shape[0])) for g in gains]))
    rope = jnp.zeros((2 * SUBLANES, m), F32) if pos is None else _rope_table_t(pos)
    out_rows = [(pw, dt) for blk in blocks for dt in blk[4]]
    kind = None
    ws = jnp.zeros((SUBLANES, d), BF16)
    gs = jnp.zeros((SUBLANES, LANES), F32)
    if small is not None:
        kind = small[0]
        if kind == "idx":
            ws = jnp.concatenate([small[1].T, small[2].T], axis=0).astype(BF16)
            gs = _lane_replicated(small[3])
            out_rows += [(IDX_DIM, F32), (N_IDX_HEADS, F32)]
        else:
            ws = small[1].T.astype(BF16)
            gs = _lane_replicated(small[2])
            out_rows += [(N_HEADS_C, F32)]
    out_shape = [jax.ShapeDtypeStruct((b, r, seq), dt) for r, dt in out_rows]
    out_specs = [pl.BlockSpec((1, r, tm), lambda i: (i // nt, 0, i % nt)) for r, _ in out_rows]
    return pl.pallas_call(
        functools.partial(_proj_t_kernel, specs=specs, pw=pw, small=kind),
        grid=(m // tm,),
        in_specs=[_row_spec(tm, d), _const_spec((1, d)), _const_spec(wt.shape), _const_spec(ws.shape),
                  pl.BlockSpec((2 * SUBLANES, tm), lambda i: (0, i)), _const_spec(gains_arr.shape),
                  _const_spec(gs.shape), _const_spec((MXU_DIM, MXU_DIM))],
        out_specs=out_specs,
        out_shape=out_shape,
        compiler_params=_cparams(1),
        name="proj_t",
    )(x3.reshape(m, d), norm.astype(F32)[None, :], wt, ws, rope, gains_arr, gs, _block_diag_mean())


def _silu(g):
    return g * jax.nn.sigmoid(g)


def _out0_kernel(x_ref, oa_ref, ga_ref, ob_ref, gb_ref, sub_ref, w_ref, y_ref, *, post_scale):
    ya = (oa_ref[...] * _silu(ga_ref[...])).astype(BF16)
    ob = ob_ref[...]
    cols = []
    for h in range(N_HEADS_B):
        oc = ob[:, h * LANES:(h + 1) * LANES]
        ms = jnp.mean(oc * oc, axis=-1, keepdims=True)
        cols.append(oc * lax.rsqrt(ms + EPS))
    obn = jnp.concatenate(cols, axis=-1) * sub_ref[...] * post_scale
    yb = (obn * _silu(gb_ref[...])).astype(BF16)
    y = jnp.dot(ya, w_ref[0:WIDTH_A, :], preferred_element_type=F32)
    y = y + jnp.dot(yb, w_ref[WIDTH_A:WIDTH_A + WIDTH_B, :], preferred_element_type=F32)
    y_ref[...] = x_ref[...] + y


def _out1_kernel(x_ref, o_ref, g_ref, w_ref, y_ref):
    yo = (o_ref[...] * _silu(g_ref[...])).astype(BF16)
    y_ref[...] = x_ref[...] + jnp.dot(yo, w_ref[...], preferred_element_type=F32)


def _out0(x2d, o_a, g_a, o_b, g_b, p, lam_init, tm):
    m, d = x2d.shape
    sub = _tile_gain(p["subln_b"], WIDTH_B)
    w = p["w_out"].astype(BF16)
    return pl.pallas_call(
        functools.partial(_out0_kernel, post_scale=1.0 - lam_init),
        grid=(m // tm,),
        in_specs=[_row_spec(tm, d), _row_spec(tm, WIDTH_A), _row_spec(tm, WIDTH_A), _row_spec(tm, WIDTH_B),
                  _row_spec(tm, WIDTH_B), _const_spec(sub.shape), _const_spec(w.shape)],
        out_specs=_row_spec(tm, d),
        out_shape=jax.ShapeDtypeStruct((m, d), F32),
        compiler_params=_cparams(1),
        name="out0",
    )(x2d, o_a, g_a, o_b, g_b, sub, w)


def _out1(x2d, o, g, p, tm):
    m, d = x2d.shape
    w = p["w_out"].astype(BF16)
    return pl.pallas_call(
        _out1_kernel,
        grid=(m // tm,),
        in_specs=[_row_spec(tm, d), _row_spec(tm, WIDTH_C), _row_spec(tm, WIDTH_C), _const_spec(w.shape)],
        out_specs=_row_spec(tm, d),
        out_shape=jax.ShapeDtypeStruct((m, d), F32),
        compiler_params=_cparams(1),
        name="out1",
    )(x2d, o, g, w)


def _select_topk(load_keys, n_chunks, groups, rows, nsel, idx_bits, idx_of):
    shape = (rows, LANES)
    rb = min(rows, COUNT_ROWS)
    n_blk = rows // rb

    def count(pred_of):
        accs = []
        for b in range(n_blk):
            pred = pred_of(slice(b * rb, (b + 1) * rb))

            def body(c, acc, b=b, pred=pred):
                blk = load_keys(c, b * rb, rb)
                for g in range(groups):
                    acc = jnp.where(pred(blk[:, g * LANES:(g + 1) * LANES], c, g), acc + 1.0, acc)
                return acc
            accs.append(lax.fori_loop(0, n_chunks, body, jnp.zeros((rb, LANES), F32),
                                      unroll=isinstance(n_chunks, int)))
        outs = [jnp.broadcast_to(jnp.sum(a, axis=-1, keepdims=True), (rb, LANES)) for a in accs]
        return outs[0] if n_blk == 1 else jnp.concatenate(outs, axis=0)

    kf = float(nsel)

    def bit_step(i, thr):
        bit = lax.shift_left(jnp.int32(1), jnp.int32(31) - i)
        cand = thr ^ bit

        def pred_of(sl):
            cb = cand[sl]
            return lambda k, c, g: k >= cb
        return jnp.where(count(pred_of) >= kf, cand, thr)

    thr = lax.fori_loop(0, 32, bit_step, jnp.full(shape, jnp.iinfo(jnp.int32).min, I32))
    thr = jnp.maximum(thr, KEY_NEG_INF + 1)

    def gt_of(sl):
        tb = thr[sl]
        return lambda k, c, g: k > tb

    def ge_of(sl):
        tb = thr[sl]
        return lambda k, c, g: k >= tb

    n_gt = count(gt_of)
    n_ge = count(ge_of)
    need = kf - n_gt
    excess = n_ge - kf

    def tie_cut():
        def idx_step(i, cut):
            bit = lax.shift_left(jnp.int32(1), jnp.int32(idx_bits - 1) - i)
            cand = cut | bit

            def pred_of(sl):
                tb, cb = thr[sl], cand[sl]
                return lambda k, c, g: (k == tb) & (idx_of(c, g, rb) < cb)
            return jnp.where(count(pred_of) < need, cand, cut)
        return lax.fori_loop(0, idx_bits, idx_step, jnp.zeros(shape, I32))

    big = jnp.full(shape, jnp.iinfo(jnp.int32).max, I32)
    any_excess = jnp.max(excess) > 0.0
    cut = lax.cond(any_excess, lambda: jnp.where(excess > 0.0, tie_cut(), big), lambda: big)
    return thr, cut


def _select_topk_t(load_keys, n_chunks, rpc, cols, nsel, idx_bits):
    shape = (SUBLANES, cols)
    reps = rpc // SUBLANES
    n_part = math.gcd(reps, COUNT_PARTS)

    def count(pred):
        def body(c, acc):
            x = jnp.where(pred(load_keys(c).reshape(reps, SUBLANES, cols), c), 1.0, 0.0)
            part = jnp.sum(x.reshape(reps // n_part, n_part, SUBLANES, cols), axis=0)
            return acc + jnp.sum(part, axis=0)
        acc = lax.fori_loop(0, n_chunks, body, jnp.zeros(shape, F32))
        return jnp.broadcast_to(jnp.sum(acc, axis=0, keepdims=True), shape)

    kf = float(nsel)

    def bit_step(i, thr):
        bit = lax.shift_left(jnp.int32(1), jnp.int32(31) - i)
        cand = thr ^ bit
        return jnp.where(count(lambda k, c: k >= cand[None]) >= kf, cand, thr)

    thr = lax.fori_loop(0, 32, bit_step, jnp.full(shape, jnp.iinfo(jnp.int32).min, I32))
    thr = jnp.maximum(thr, KEY_NEG_INF + 1)
    n_gt = count(lambda k, c: k > thr[None])
    n_ge = count(lambda k, c: k >= thr[None])
    need = kf - n_gt
    excess = n_ge - kf
    row = lax.broadcasted_iota(I32, (rpc, cols), 0).reshape(reps, SUBLANES, cols)

    def tie_cut():
        def idx_step(i, cut):
            bit = lax.shift_left(jnp.int32(1), jnp.int32(idx_bits - 1) - i)
            cand = cut | bit
            n_lt = count(lambda k, c: (k == thr[None]) & (row + c * rpc < cand[None]))
            return jnp.where(n_lt < need, cand, cut)
        return lax.fori_loop(0, idx_bits, idx_step, jnp.zeros(shape, I32))

    big = jnp.full(shape, jnp.iinfo(jnp.int32).max, I32)
    any_excess = jnp.max(excess) > 0.0
    cut = lax.cond(any_excess, lambda: jnp.where(excess > 0.0, tie_cut(), big), lambda: big)
    return thr, cut


def _masked_pairs_t(qt_ref, qm_ref, n_pairs, tq):
    top = lax.broadcasted_iota(I32, (LANES, tq), 0) < HEAD_DIM
    for j in range(n_pairs):
        pair = qt_ref[0, j * LANES:(j + 1) * LANES, :]
        zero = jnp.zeros_like(pair)
        qm_ref[2 * j] = jnp.where(top, pair, zero)
        qm_ref[2 * j + 1] = jnp.where(top, zero, pair)


def _idx_prompt_t_kernel(qit_ref, wit_ref, ki_ref, bias_ref, qm_ref, s_ref, *, tq, tk, nsel, seq):
    qi = pl.program_id(1)
    q0 = qi * tq
    n_chunks = seq // tk
    nc = (q0 + tq + tk - 1) // tk
    _masked_pairs_t(qit_ref, qm_ref, N_IDX_HEADS // 2, tq)
    wt = wit_ref[0]
    col = q0 + lax.broadcasted_iota(I32, (tk, tq), 1)
    row0 = lax.broadcasted_iota(I32, (tk, tq), 0)

    def score_body(c, carry):
        start = pl.multiple_of(c * tk, tk)
        kblk = ki_ref[0, pl.ds(start, tk), :]
        acc = jnp.zeros((tk, tq), F32)
        for h in range(N_IDX_HEADS):
            s = jnp.dot(kblk, qm_ref[h], preferred_element_type=F32)
            acc = acc + jnp.maximum(s, 0.0) * wt[h:h + 1, :]
        acc = jnp.where(row0 + c * tk <= col, acc, -jnp.inf)
        s_ref[c] = _sort_key(acc)
        return carry

    lax.fori_loop(0, nc, score_body, 0)
    thr, cut = _select_topk_t(lambda c: s_ref[c], nc, tk, tq, nsel, int(math.log2(seq)) + 1)
    thr_t = jnp.tile(thr, (tk // SUBLANES, 1))
    cut_t = jnp.tile(cut, (tk // SUBLANES, 1))

    def write_body(c, carry):
        k = s_ref[c]
        sel = (k > thr_t) | ((k == thr_t) & (row0 + c * tk <= cut_t))
        start = pl.multiple_of(c * tk, tk)
        bias_ref[0, 0, pl.ds(start, tk), :] = jnp.where(sel, 0.0, NEG).astype(BF16)
        return carry

    lax.fori_loop(0, nc, write_body, 0)

    def fill_body(c, carry):
        start = pl.multiple_of(c * tk, tk)
        bias_ref[0, 0, pl.ds(start, tk), :] = jnp.full((tk, tq), NEG, BF16)
        return carry

    lax.fori_loop(nc, n_chunks, fill_body, 0)


def _idx_prompt_t(qit, wit, ki16, nsel, tq, tq_out, tk):
    b, seq, _ = ki16.shape
    nq = seq // tq
    per = tq_out // tq
    return pl.pallas_call(
        functools.partial(_idx_prompt_t_kernel, tq=tq, tk=tk, nsel=nsel, seq=seq),
        grid=(b, nq),
        in_specs=[pl.BlockSpec((1, WIDTH_I, tq), lambda bi, qi: (bi, 0, qi)),
                  pl.BlockSpec((1, N_IDX_HEADS, tq), lambda bi, qi: (bi, 0, qi)),
                  pl.BlockSpec((1, seq, LANES), lambda bi, qi: (bi, 0, 0))],
        out_specs=pl.BlockSpec((1, 1, seq, tq), lambda bi, qi: (bi, qi // per, 0, qi % per)),
        out_shape=jax.ShapeDtypeStruct((b, seq // tq_out, seq, tq_out), BF16),
        scratch_shapes=[pltpu.VMEM((N_IDX_HEADS, LANES, tq), BF16),
                        pltpu.VMEM((seq // tk, tk, tq), I32)],
        compiler_params=_cparams(2),
        name="idx_prompt",
    )(qit, wit, ki16)


def _flash_t_kernel(*refs, variant, n_pairs, tq, tk, sub):
    it = iter(refs)
    k_ref, qt_ref, vt_ref = next(it), next(it), next(it)
    bias_ref = next(it) if variant == "dsa" else None
    cqt_ref, ck_ref = (next(it), next(it)) if variant == "fox" else (None, None)
    lam_ref = next(it) if variant == "diff" else None
    o_ref = next(it)
    qm_ref, m_ref, l_ref, acc_ref = next(it), next(it), next(it), next(it)
    bias32_ref = next(it) if variant == "dsa" else None

    qi = pl.program_id(1)
    kc = pl.program_id(2)
    nk = pl.num_programs(2)
    last = ((qi + 1) * tq - 1) // tk
    n_units = 2 * n_pairs
    vr = acc_ref.shape[1]

    @pl.when(kc == 0)
    def _init():
        _masked_pairs_t(qt_ref, qm_ref, n_pairs, tq)
        m_ref[...] = jnp.full(m_ref.shape, NEG, F32)
        l_ref[...] = jnp.zeros(l_ref.shape, F32)
        acc_ref[...] = jnp.zeros(acc_ref.shape, F32)

    def compute(masked):
        if variant == "dsa":
            bias32_ref[...] = bias_ref[0, 0].astype(F32)
            masked = False
        if masked:
            col = qi * tq + lax.broadcasted_iota(I32, (sub, tq), 1)
            row0 = kc * tk + lax.broadcasted_iota(I32, (sub, tq), 0)
        ones_rows = jnp.ones((2 * SUBLANES, sub), BF16)
        if variant == "fox":
            ck_all = ck_ref[0] * LOG2E
        for g0 in range(0, n_units, FLASH_GROUP):
            units = list(range(g0, min(g0 + FLASH_GROUP, n_units)))
            state = [[m_ref[u], l_ref[u], acc_ref[u]] for u in units]
            for r0 in range(0, tk, sub):
                ss = [jnp.dot(k_ref[0, r0:r0 + sub, (u // 2) * LANES:(u // 2 + 1) * LANES], qm_ref[u],
                              preferred_element_type=F32) for u in units]
                ps = []
                for e, u in enumerate(units):
                    s = ss[e]
                    if variant == "dsa":
                        s = s + bias32_ref[r0:r0 + sub, :]
                    cq_row = 0.0
                    if variant == "fox":
                        s = s - jnp.broadcast_to(ck_all[r0:r0 + sub, u:u + 1], (sub, tq))
                        cq_row = cqt_ref[0, u:u + 1, :] * LOG2E
                    if masked:
                        s = jnp.where(row0 + r0 <= col, s, NEG)
                    m_prev, l_prev, acc = state[e]
                    m_new = jnp.maximum(m_prev, jnp.max(s, axis=0, keepdims=True) + cq_row)
                    alpha = jnp.exp2(m_prev - m_new)
                    p = jnp.exp2(s - jnp.tile(m_new - cq_row, (sub // SUBLANES, 1)))
                    state[e][0] = m_new
                    state[e][1] = alpha * l_prev
                    state[e][2] = jnp.tile(alpha, (vr // SUBLANES, 1)) * acc
                    ps.append(p.astype(BF16))
                for e, u in enumerate(units):
                    v0 = (u // 2) * vr if variant == "diff" else u * vr
                    v_aug = jnp.concatenate([vt_ref[0, v0:v0 + vr, r0:r0 + sub], ones_rows], axis=0)
                    pv = jnp.dot(v_aug, ps[e], preferred_element_type=F32)
                    state[e][1] = state[e][1] + pv[vr:vr + SUBLANES, :]
                    state[e][2] = state[e][2] + pv[0:vr, :]
            for e, u in enumerate(units):
                m_ref[u], l_ref[u], acc_ref[u] = state[e]

    needs_mask = (kc + 1) * tk - 1 > qi * tq

    @pl.when((kc <= last) & needs_mask)
    def _diag():
        compute(True)

    @pl.when((kc <= last) & jnp.logical_not(needs_mask))
    def _full():
        compute(False)

    @pl.when(kc == nk - 1)
    def _fin():
        def norm(u):
            return acc_ref[u] / jnp.tile(l_ref[u], (vr // SUBLANES, 1))
        for j in range(n_pairs):
            if variant == "diff":
                ot = norm(2 * j) - lam_ref[0, 0] * norm(2 * j + 1)
            else:
                ot = jnp.concatenate([norm(2 * j), norm(2 * j + 1)], axis=0)
            o_ref[0, :, j * LANES:(j + 1) * LANES] = ot.T


def _flash_t(variant, k16, qt, vt, tq, tk, bias=None, cqt=None, ck=None, lam=None):
    b, seq, w = k16.shape
    n_pairs = w // LANES
    n_units = 2 * n_pairs
    nq, nk = seq // tq, seq // tk
    vr = LANES if variant == "diff" else HEAD_DIM
    sub = min(tk, FLASH_SUB)

    def kc_of(qi, kc):
        return jnp.minimum(kc, ((qi + 1) * tq - 1) // tk)

    in_specs = [pl.BlockSpec((1, tk, w), lambda bi, qi, kc: (bi, kc_of(qi, kc), 0)),
                pl.BlockSpec((1, w, tq), lambda bi, qi, kc: (bi, 0, qi)),
                pl.BlockSpec((1, vt.shape[1], tk), lambda bi, qi, kc: (bi, 0, kc_of(qi, kc)))]
    args = [k16, qt, vt]
    scratch = [pltpu.VMEM((n_units, LANES, tq), BF16), pltpu.VMEM((n_units, SUBLANES, tq), F32),
               pltpu.VMEM((n_units, SUBLANES, tq), F32), pltpu.VMEM((n_units, vr, tq), F32)]
    if variant == "dsa":
        in_specs.append(pl.BlockSpec((1, 1, tk, tq), lambda bi, qi, kc: (bi, qi, kc_of(qi, kc), 0)))
        args.append(bias)
        scratch.append(pltpu.VMEM((tk, tq), F32))
    if variant == "fox":
        in_specs.append(pl.BlockSpec((1, n_units, tq), lambda bi, qi, kc: (bi, 0, qi)))
        in_specs.append(pl.BlockSpec((1, tk, n_units), lambda bi, qi, kc: (bi, kc_of(qi, kc), 0)))
        args += [cqt, ck]
    if variant == "diff":
        in_specs.append(pl.BlockSpec(memory_space=pltpu.SMEM))
        args.append(lam)
    return pl.pallas_call(
        functools.partial(_flash_t_kernel, variant=variant, n_pairs=n_pairs, tq=tq, tk=tk, sub=sub),
        grid=(b, nq, nk),
        in_specs=in_specs,
        out_specs=pl.BlockSpec((1, tq, w), lambda bi, qi, kc: (bi, qi, 0)),
        out_shape=jax.ShapeDtypeStruct((b, seq, w), F32),
        scratch_shapes=scratch,
        compiler_params=_cparams(3),
        name="flash_" + variant,
    )(*args)


PAGES_PER_STEP = 8


def _pages_per_step(n_pages):
    g = PAGES_PER_STEP
    while n_pages % g:
        g //= 2
    return g


def _upper_ones():
    r = lax.broadcasted_iota(I32, (LANES, LANES), 0)
    c = lax.broadcasted_iota(I32, (LANES, LANES), 1)
    return jnp.where(r <= c, 1.0, 0.0).astype(BF16)


def _cumsum_kernel(*refs, g):
    x_refs = refs[1:1 + g]
    xn_ref, c_ref, cn_ref, carry_ref = refs[1 + g:5 + g]
    p = pl.program_id(1)
    upper = _upper_ones()

    @pl.when(p == 0)
    def _():
        carry_ref[...] = jnp.zeros(carry_ref.shape, F32)

    def page_cumsum(x):
        return sum(jnp.dot(t, upper, preferred_element_type=F32) for t in _split3(x))

    local = [page_cumsum(x_refs[j][0]) for j in range(g)]
    total = carry_ref[...]
    for j in range(g):
        c_ref[0, :, j * LANES:(j + 1) * LANES] = local[j] + total
        total = total + jnp.broadcast_to(local[j][:, LANES - 1:LANES], total.shape)
    carry_ref[...] = total

    @pl.when(p == pl.num_programs(1) - 1)
    def _():
        cn_ref[0] = page_cumsum(xn_ref[0]) + total


def _cumsum_pages(page_table, pool_t, new_t):
    b, n_pages = page_table.shape
    h = pool_t.shape[1]
    g = _pages_per_step(n_pages)
    pool_specs = [pl.BlockSpec((1, h, LANES), lambda bi, p, pt, j=j: (pt[bi, p * g + j], 0, 0)) for j in range(g)]
    grid_spec = pltpu.PrefetchScalarGridSpec(
        num_scalar_prefetch=1,
        grid=(b, n_pages // g),
        in_specs=pool_specs + [pl.BlockSpec((1, h, LANES), lambda bi, p, pt: (bi, 0, 0))],
        out_specs=[pl.BlockSpec((1, h, g * LANES), lambda bi, p, pt: (bi, 0, p)),
                   pl.BlockSpec((1, h, LANES), lambda bi, p, pt: (bi, 0, 0))],
        scratch_shapes=[pltpu.VMEM((h, LANES), F32)],
    )
    return pl.pallas_call(
        functools.partial(_cumsum_kernel, g=g),
        grid_spec=grid_spec,
        out_shape=(jax.ShapeDtypeStruct((b, h, n_pages * LANES), F32), jax.ShapeDtypeStruct((b, h, LANES), F32)),
        compiler_params=_cparams(2),
        name="cumsum_pages",
    )(page_table, *([pool_t] * g), new_t)


def _idx_decode_kernel(*refs, g, n_pages, n_new, nsel, page):
    q_ref, w_ref = refs[1:3]
    k_refs = refs[3:3 + g]
    knew_ref, bias_ref, biasn_ref, s_ref = refs[3 + g:7 + g]
    p = pl.program_id(1)
    n_steps = n_pages // g
    rows = n_new

    def scores(kt):
        n = kt.shape[1] // LANES
        s = jnp.dot(q_ref[0], kt.astype(BF16), preferred_element_type=F32)
        t = jnp.maximum(s, 0.0) * jnp.tile(w_ref[0], (1, n))
        acc = t[0:rows]
        for h in range(1, N_IDX_HEADS):
            acc = acc + t[h * rows:(h + 1) * rows]
        return acc

    @pl.when(p < n_steps)
    def _past():
        keys = _sort_key(scores(jnp.concatenate([k_refs[j][0] for j in range(g)], axis=1)))
        for j in range(g):
            s_ref[p * g + j] = keys[:, j * LANES:(j + 1) * LANES]

    @pl.when(p == n_steps)
    def _new():
        sc = scores(knew_ref[0])
        i = lax.broadcasted_iota(I32, (rows, LANES), 0)
        lane = lax.broadcasted_iota(I32, (rows, LANES), 1)
        s_ref[n_pages] = _sort_key(jnp.where(lane <= i, sc, -jnp.inf))

        def idx_of(c, grp, nr):
            return lane + c * page

        n_keys = (n_pages + 1) * page
        thr, cut = _select_topk(lambda c, r0, nr: s_ref[c], n_pages + 1, 1, rows, nsel,
                                int(math.log2(n_keys)) + 1, idx_of)

        def selected(c):
            k = s_ref[c]
            sel = (k > thr) | ((k == thr) & (lane + c * page <= cut))
            return jnp.where(sel, 0.0, NEG)

        def write_body(c, carry):
            bias_ref[0, c] = selected(c)
            return carry

        lax.fori_loop(0, n_pages, write_body, 0)
        biasn_ref[0] = selected(n_pages)


def _idx_decode(page_table, q_st, w_st, kt_pool, kt_new, nsel):
    b, n_pages = page_table.shape
    page = kt_pool.shape[2]
    n_new = q_st.shape[1] // N_IDX_HEADS
    g = _pages_per_step(n_pages)
    n_steps = n_pages // g

    def req_map(bi, p, pt):
        return (bi, 0, 0)

    pool_specs = [pl.BlockSpec((1, IDX_DIM, page),
                               lambda bi, p, pt, j=j: (pt[bi, jnp.minimum(p, n_steps - 1) * g + j], 0, 0))
                  for j in range(g)]
    grid_spec = pltpu.PrefetchScalarGridSpec(
        num_scalar_prefetch=1,
        grid=(b, n_steps + 1),
        in_specs=[pl.BlockSpec((1,) + q_st.shape[1:], req_map), pl.BlockSpec((1,) + w_st.shape[1:], req_map)]
        + pool_specs + [pl.BlockSpec((1, IDX_DIM, page), req_map)],
        out_specs=[pl.BlockSpec((1, n_pages, n_new, LANES), lambda bi, p, pt: (bi, 0, 0, 0)),
                   pl.BlockSpec((1, n_new, LANES), req_map)],
        scratch_shapes=[pltpu.VMEM((n_pages + 1, n_new, LANES), I32)],
    )
    return pl.pallas_call(
        functools.partial(_idx_decode_kernel, g=g, n_pages=n_pages, n_new=n_new, nsel=nsel, page=page),
        grid_spec=grid_spec,
        out_shape=(jax.ShapeDtypeStruct((b, n_pages, n_new, LANES), F32),
                   jax.ShapeDtypeStruct((b, n_new, LANES), F32)),
        compiler_params=_cparams(2),
        name="idx_decode",
    )(page_table, q_st, w_st, *([kt_pool] * g), kt_new)


def _attn_decode_kernel(*refs, variant, g, n_units, n_new, n_pages):
    it = iter(refs)
    next(it)
    q_ref = next(it)
    k_refs = [next(it) for _ in range(g)]
    v_refs = [next(it) for _ in range(g)]
    knew_ref, vnew_ref = next(it), next(it)
    bias_ref, biasn_ref = (next(it), next(it)) if variant == "dsa" else (None, None)
    cq_ref, ck_ref, cn_ref = (next(it), next(it), next(it)) if variant == "fox" else (None, None, None)
    lam_ref = next(it) if variant == "diff" else None
    o_ref = next(it)
    qbd_ref, m_ref, l_ref, acc_ref = next(it), next(it), next(it), next(it)

    p = pl.program_id(1)
    n_steps = n_pages // g
    rows = n_units * n_new
    w = q_ref.shape[-1]
    wa = acc_ref.shape[-1]

    @pl.when(p == 0)
    def _init():
        q = q_ref[0].astype(F32)
        qt = jnp.concatenate([q] * n_units, axis=0)
        r = _div_pow2(lax.broadcasted_iota(I32, (rows, w), 0), n_new)
        c = _div_pow2(lax.broadcasted_iota(I32, (rows, w), 1), HEAD_DIM)
        qbd_ref[...] = jnp.where(r == c, qt, 0.0).astype(BF16)
        m_ref[...] = jnp.full(m_ref.shape, NEG, F32)
        l_ref[...] = jnp.zeros(l_ref.shape, F32)
        acc_ref[...] = jnp.zeros(acc_ref.shape, F32)

    def expand_rows(x8):
        return jnp.concatenate([x8] * n_units, axis=0)

    def expand_units(xu):
        return jnp.concatenate([jnp.broadcast_to(xu[u:u + 1, :], (n_new, xu.shape[1])) for u in range(n_units)],
                               axis=0)

    def v_head(v_ref, h, is_new):
        if is_new or len(v_ref.shape) == 3:
            return v_ref[0, :, h * LANES:(h + 1) * LANES]
        return v_ref[0, :, h, :]

    def step(kts, vs, bias, ck, is_new):
        ng = len(kts)
        kt_all = kts[0][0] if ng == 1 else jnp.concatenate([kts[j][0] for j in range(ng)], axis=1)
        s = jnp.dot(qbd_ref[...], kt_all.astype(BF16), preferred_element_type=F32)
        if variant == "dsa":
            s = s + bias
        if variant == "fox":
            s = s + (jnp.tile(cq_ref[0], (1, ng)) - expand_units(ck))
        if is_new:
            i = expand_rows(lax.broadcasted_iota(I32, (n_new, LANES), 0))
            lane = lax.broadcasted_iota(I32, (rows, LANES), 1)
            s = jnp.where(lane <= i, s, NEG)
        m_prev = m_ref[...]
        m_new = jnp.maximum(m_prev, jnp.max(s, axis=-1, keepdims=True))
        alpha = jnp.exp(m_prev - m_new)
        pr = jnp.exp(s - jnp.tile(m_new, (1, ng)))
        l_ref[...] = alpha * l_ref[...] + jnp.sum(pr, axis=-1, keepdims=True)
        m_ref[...] = m_new
        p16 = pr.astype(BF16)
        if variant == "diff":
            hr = 2 * n_new
            pv = jnp.concatenate(
                [jnp.dot(p16[h * hr:(h + 1) * hr],
                         jnp.concatenate([v_head(vs[j], h, is_new) for j in range(ng)], axis=0).astype(BF16),
                         preferred_element_type=F32) for h in range(n_units // 2)], axis=0)
        else:
            vt_all = vs[0][0] if ng == 1 else jnp.concatenate([vs[j][0] for j in range(ng)], axis=1)
            pv = lax.dot_general(p16, vt_all.astype(BF16), NT_DIMS, preferred_element_type=F32)
        acc_ref[...] = jnp.tile(alpha, (1, wa // LANES)) * acc_ref[...] + pv

    @pl.when(p < n_steps)
    def _past():
        bias = ck = None
        if variant == "dsa":
            bias = jnp.concatenate([expand_rows(bias_ref[0, j]) for j in range(g)], axis=1)
        if variant == "fox":
            ck = ck_ref[0]
        step(k_refs, v_refs, bias, ck, False)

    @pl.when(p == n_steps)
    def _new():
        bias = expand_rows(biasn_ref[0]) if variant == "dsa" else None
        ck = cn_ref[0] if variant == "fox" else None
        step([knew_ref], [vnew_ref], bias, ck, True)
        accn = acc_ref[...] / jnp.tile(l_ref[...], (1, wa // LANES))
        if variant == "diff":
            lam = lam_ref[0, 0]
            o_ref[0] = jnp.concatenate(
                [accn[(2 * h) * n_new:(2 * h + 1) * n_new] - lam * accn[(2 * h + 1) * n_new:(2 * h + 2) * n_new]
                 for h in range(n_units // 2)], axis=1)
        else:
            cu = _div_pow2(lax.broadcasted_iota(I32, (n_new, wa), 1), HEAD_DIM)
            out = jnp.zeros((n_new, wa), F32)
            for u in range(n_units):
                out = out + jnp.where(cu == u, accn[u * n_new:(u + 1) * n_new], 0.0)
            o_ref[0] = out


def _attn_decode(variant, page_table, q16, kt_pool, v_pool, kt_new, v_new,
                  bias=None, bias_new=None, cq=None, ck=None, cn=None, lam=None):
    b, n_pages = page_table.shape
    page = kt_pool.shape[2]
    n_new, w = q16.shape[1], q16.shape[2]
    n_units = w // HEAD_DIM
    rows = n_units * n_new
    g = _pages_per_step(n_pages)
    n_steps = n_pages // g
    wa = LANES if variant == "diff" else w

    def req_map(bi, p, pt):
        return (bi, 0, 0)

    def pool_spec(arr, j):
        nd = arr.ndim
        return pl.BlockSpec((1,) + arr.shape[1:],
                            lambda bi, p, pt: (pt[bi, jnp.minimum(p, n_steps - 1) * g + j],) + (0,) * (nd - 1))

    in_specs = ([pl.BlockSpec((1, n_new, w), req_map)]
                + [pool_spec(kt_pool, j) for j in range(g)] + [pool_spec(v_pool, j) for j in range(g)]
                + [pl.BlockSpec((1,) + kt_new.shape[1:], req_map), pl.BlockSpec((1,) + v_new.shape[1:], req_map)])
    args = [q16] + [kt_pool] * g + [v_pool] * g + [kt_new, v_new]
    if variant == "dsa":
        in_specs.append(pl.BlockSpec((1, g, n_new, LANES),
                                     lambda bi, p, pt: (bi, jnp.minimum(p, n_steps - 1), 0, 0)))
        in_specs.append(pl.BlockSpec((1, n_new, LANES), req_map))
        args += [bias, bias_new]
    if variant == "fox":
        in_specs.append(pl.BlockSpec((1, rows, LANES), req_map))
        in_specs.append(pl.BlockSpec((1, n_units, g * LANES), lambda bi, p, pt: (bi, 0, jnp.minimum(p, n_steps - 1))))
        in_specs.append(pl.BlockSpec((1, n_units, LANES), req_map))
        args += [cq, ck, cn]
    if variant == "diff":
        in_specs.append(pl.BlockSpec(memory_space=pltpu.SMEM))
        args.append(lam)
    wo = v_new.shape[2] if variant == "diff" else w
    grid_spec = pltpu.PrefetchScalarGridSpec(
        num_scalar_prefetch=1,
        grid=(b, n_steps + 1),
        in_specs=in_specs,
        out_specs=pl.BlockSpec((1, n_new, wo), req_map),
        scratch_shapes=[pltpu.VMEM((rows, w), BF16), pltpu.VMEM((rows, LANES), F32),
                        pltpu.VMEM((rows, LANES), F32), pltpu.VMEM((rows, wa), F32)],
    )
    return pl.pallas_call(
        functools.partial(_attn_decode_kernel, variant=variant, g=g, n_units=n_units, n_new=n_new,
                          n_pages=n_pages),
        grid_spec=grid_spec,
        out_shape=jax.ShapeDtypeStruct((b, n_new, wo), F32),
        compiler_params=_cparams(2),
        name="attn_decode_" + variant,
    )(page_table, *args)


def _feature_major_pages(cache):
    n_pool, page = cache.shape[0], cache.shape[1]
    perm = (0,) + tuple(range(2, cache.ndim)) + (1,)
    return jnp.transpose(cache, perm).reshape(n_pool, -1, page)


def _feature_major_new(x3, page):
    xt = jnp.swapaxes(x3, 1, 2)
    return jnp.pad(xt, ((0, 0), (0, 0), (0, page - xt.shape[2])))


def _diff_lambda(p, lam_init):
    def e(a, c):
        return jnp.exp(jnp.sum(a.astype(F32) * c.astype(F32)))
    return (e(p["lam_q1"], p["lam_k1"]) - e(p["lam_q2"], p["lam_k2"]) + lam_init).reshape(1, 1).astype(F32)


def _pad_rows(x, rows):
    return jnp.pad(x, ((0, 0), (0, rows - x.shape[1]), (0, 0)))


def _tile_for(m, pref):
    t = min(m, pref)
    while m % t:
        t //= 2
    return t


def _token_major(xt, *tail):
    b, _, seq = xt.shape
    return jnp.swapaxes(xt, 1, 2).reshape((b, seq) + tail)


def _even_prompt(x, layer, p, tiles):
    b, seq, d = x.shape
    tm, tq, tk = tiles
    pos = jnp.tile(jnp.arange(seq), b)
    t = _proj0(x.reshape(b * seq, d), pos, p, tm, ("ka16", "ki16", "kb16", "vb", "ga", "gb"))
    lam_init = 0.8 - 0.6 * math.exp(-0.3 * layer)
    lam = _diff_lambda(p, lam_init)
    nsel = min(TOPK_MAX, seq // 4)
    r3 = lambda a: a.reshape(b, seq, a.shape[-1])
    offs = _even_splits()
    col = lambda i: p["w_in"][:, offs[i]:offs[i + 1]]
    log2_scale = QK_SCALE * LOG2E
    qat, kat, vat16, vat, qit, qbt, kbt, vbt16, kit, wit = _proj_t(
        x, p["norm"],
        [(col(0), 0, True, log2_scale, (BF16,)), (col(1), 1, True, 1.0, (F32,)),
         (col(2), None, False, 1.0, (BF16, F32)), (col(4), None, True, 1.0, (BF16,)),
         (col(7), 2, True, log2_scale, (BF16,)), (col(8), 3, True, 1.0, (F32,)),
         (col(9), None, False, 1.0, (BF16,))],
        [p["qn_a"], p["kn_a"], p["qn_b"], p["kn_b"]], pos, ("idx", col(5), col(6), p["kn_idx"]), tm)
    bias = _idx_prompt_t(qit, wit, r3(t["ki16"]), nsel, min(tq, IDX_Q_TILE), tq, tk)
    o_a = _flash_t("dsa", r3(t["ka16"]), qat, vat16, tq, tk, bias=bias)
    o_b = _flash_t("diff", r3(t["kb16"]), qbt, vbt16, tq, tk, lam=lam)
    y = _out0(x.reshape(b * seq, d), o_a.reshape(b * seq, -1), t["ga"], o_b.reshape(b * seq, -1), t["gb"], p,
              lam_init, tm)
    new = (_token_major(kat, N_HEADS_A, HEAD_DIM), _token_major(vat, N_HEADS_A, HEAD_DIM), _token_major(kit, IDX_DIM),
           _token_major(kbt, N_HEADS_B, 2, HEAD_DIM), t["vb"].reshape(b, seq, N_HEADS_B, 2 * HEAD_DIM))
    return y.reshape(b, seq, d), new


def _even_sample(x, layer, caches, page_table, p):
    cache_k_a, cache_v_a, cache_k_i, cache_k_b, cache_v_b = caches
    b, n_new, d = x.shape
    n_pages = page_table.shape[1]
    page = cache_k_a.shape[1]
    past = n_pages * page
    pos = jnp.tile(past + jnp.arange(n_new), b)
    m = b * n_new
    t = _proj0(x.reshape(m, d), pos, p, _tile_for(m, ROW_TILE),
               ("qa", "ka", "va", "ga", "qi", "ki", "wi", "qb", "kb", "vb", "gb"))
    lam_init = 0.8 - 0.6 * math.exp(-0.3 * layer)
    lam = _diff_lambda(p, lam_init)
    nsel = min(TOPK_MAX, (past + n_new) // 4)
    r3 = lambda a: a.reshape(b, n_new, a.shape[-1])
    q_st = jnp.swapaxes(t["qi"].reshape(b, n_new, N_IDX_HEADS, IDX_DIM), 1, 2).reshape(
        b, N_IDX_HEADS * n_new, IDX_DIM)
    w_st = jnp.swapaxes(t["wi"].reshape(b, n_new, N_IDX_HEADS), 1, 2).reshape(b, N_IDX_HEADS * n_new, 1)
    w_st = jnp.broadcast_to(w_st, (b, N_IDX_HEADS * n_new, LANES))
    bias, bias_new = _idx_decode(page_table, q_st, w_st, _feature_major_pages(cache_k_i),
                                 _feature_major_new(r3(t["ki"]), page), nsel)
    o_a = _attn_decode("dsa", page_table, r3(t["qa"]), _feature_major_pages(cache_k_a),
                       _feature_major_pages(cache_v_a), _feature_major_new(r3(t["ka"]), page),
                       _feature_major_new(r3(t["va"]), page), bias=bias, bias_new=bias_new)
    o_b = _attn_decode("diff", page_table, r3(t["qb"]), _feature_major_pages(cache_k_b), cache_v_b,
                       _feature_major_new(r3(t["kb"]), page), _pad_rows(r3(t["vb"]), page), lam=lam)
    y = _out0(x.reshape(m, d), o_a.reshape(m, -1), t["ga"], o_b.reshape(m, -1), t["gb"], p, lam_init,
              _tile_for(m, ROW_TILE))
    new = (t["ka"].reshape(b, n_new, N_HEADS_A, HEAD_DIM), t["va"].reshape(b, n_new, N_HEADS_A, HEAD_DIM),
           t["ki"].reshape(b, n_new, IDX_DIM), t["kb"].reshape(b, n_new, N_HEADS_B, 2, HEAD_DIM),
           t["vb"].reshape(b, n_new, N_HEADS_B, 2 * HEAD_DIM))
    return y.reshape(b, n_new, d), new


def _odd_prompt(x, p, tiles):
    b, seq, d = x.shape
    tm, tq, tk = tiles
    t = _proj1(x.reshape(b * seq, d), p, tm, ("k16", "gate"))
    wc = WIDTH_C
    w_in = p["w_in"]
    qt, kt, vt16, vt, logf_t = _proj_t(
        x, p["norm"],
        [(w_in[:, 0:wc], 0, False, QK_SCALE * LOG2E, (BF16,)), (w_in[:, wc:2 * wc], 1, False, 1.0, (F32,)),
         (w_in[:, 2 * wc:3 * wc], None, False, 1.0, (BF16, F32))],
        [p["qn"], p["kn"]], None, ("logf", w_in[:, 4 * wc:], p["b_f"]), tm)
    n_blk = seq // LANES
    pages = jnp.swapaxes(logf_t.reshape(b, N_HEADS_C, n_blk, LANES), 1, 2).reshape(b * n_blk, N_HEADS_C, LANES)
    ident = jnp.arange(b * n_blk, dtype=I32).reshape(b, n_blk)
    c_t, _ = _cumsum_pages(ident, pages, jnp.zeros((b, N_HEADS_C, LANES), F32))
    o = _flash_t("fox", t["k16"].reshape(b, seq, wc), qt, vt16, tq, tk, cqt=c_t, ck=jnp.swapaxes(c_t, 1, 2))
    y = _out1(x.reshape(b * seq, d), o.reshape(b * seq, -1), t["gate"], p, tm)
    new = (_token_major(kt, N_HEADS_C, HEAD_DIM), _token_major(vt, N_HEADS_C, HEAD_DIM), _token_major(logf_t, N_HEADS_C))
    return y.reshape(b, seq, d), new


def _odd_sample(x, caches, page_table, p):
    cache_k, cache_v, cache_logf = caches
    b, n_new, d = x.shape
    page = cache_k.shape[1]
    m = b * n_new
    t = _proj1(x.reshape(m, d), p, _tile_for(m, ROW_TILE), ("q16", "k", "v", "gate", "logf"))
    r3 = lambda a: a.reshape(b, n_new, a.shape[-1])
    c_past_t, c_new_t = _cumsum_pages(page_table, _feature_major_pages(cache_logf.astype(F32)),
                                      _feature_major_new(r3(t["logf"]), page))
    cq = jnp.broadcast_to(c_new_t[:, :, :n_new].reshape(b, N_HEADS_C * n_new, 1), (b, N_HEADS_C * n_new, LANES))
    o = _attn_decode("fox", page_table, r3(t["q16"]), _feature_major_pages(cache_k), _feature_major_pages(cache_v),
                     _feature_major_new(r3(t["k"]), page), _feature_major_new(r3(t["v"]), page),
                     cq=cq, ck=c_past_t, cn=c_new_t)
    y = _out1(x.reshape(m, d), o.reshape(m, -1), t["gate"], p, _tile_for(m, ROW_TILE))
    new = (t["k"].reshape(b, n_new, N_HEADS_C, HEAD_DIM), t["v"].reshape(b, n_new, N_HEADS_C, HEAD_DIM),
           t["logf"].reshape(b, n_new, N_HEADS_C))
    return y.reshape(b, n_new, d), new


def kernel(x_prompt, x_sample, cache_l0_k_a, cache_l0_v_a, cache_l0_k_idx, cache_l0_k_b, cache_l0_v_b,
           cache_l1_k_c, cache_l1_v_c, cache_l1_logf_c, page_table,
           l0_norm, l0_w_in, l0_qn_a, l0_kn_a, l0_kn_idx, l0_qn_b, l0_kn_b,
           l0_lam_q1, l0_lam_k1, l0_lam_q2, l0_lam_k2, l0_subln_b, l0_w_out,
           l1_norm, l1_w_in, l1_b_f, l1_qn, l1_kn, l1_w_out):
    p0 = dict(norm=l0_norm, w_in=l0_w_in, qn_a=l0_qn_a, kn_a=l0_kn_a, kn_idx=l0_kn_idx, qn_b=l0_qn_b, kn_b=l0_kn_b,
              lam_q1=l0_lam_q1, lam_k1=l0_lam_k1, lam_q2=l0_lam_q2, lam_k2=l0_lam_k2, subln_b=l0_subln_b,
              w_out=l0_w_out)
    p1 = dict(norm=l1_norm, w_in=l1_w_in, b_f=l1_b_f, qn=l1_qn, kn=l1_kn, w_out=l1_w_out)
    b, seq, _ = x_prompt.shape
    tiles = (_tile_for(b * seq, ROW_TILE), _tile_for(seq, Q_TILE), _tile_for(seq, K_TILE))
    page_table = page_table.astype(I32)
    xp, sp0 = _even_prompt(x_prompt, 0, p0, tiles)
    xs, ss0 = _even_sample(x_sample, 0, (cache_l0_k_a, cache_l0_v_a, cache_l0_k_idx, cache_l0_k_b, cache_l0_v_b),
                           page_table, p0)
    xp, sp1 = _odd_prompt(xp, p1, tiles)
    xs, ss1 = _odd_sample(xs, (cache_l1_k_c, cache_l1_v_c, cache_l1_logf_c), page_table, p1)
    (p_k_a, p_v_a, p_k_idx, p_k_b, p_v_b), (p_k_c, p_v_c, p_logf_c) = sp0, sp1
    (s_k_a, s_v_a, s_k_idx, s_k_b, s_v_b), (s_k_c, s_v_c, s_logf_c) = ss0, ss1
    return (xp, xs, p_k_a, s_k_a, p_v_a, s_v_a, p_k_idx, s_k_idx, p_k_b, s_k_b, p_v_b, s_v_b,
            p_k_c, s_k_c, p_v_c, s_v_c, p_logf_c, s_logf_c)
```

```python
import functools
import math

import jax
import jax.numpy as jnp
from jax import lax
from jax.experimental import pallas as pl
from jax.experimental.pallas import tpu as pltpu

F32 = jnp.float32
BF16 = jnp.bfloat16
I32 = jnp.int32

HEAD_DIM = 64
ROPE_THETA = 500000.0
N_HEADS_A = 8
N_IDX_HEADS = 8
IDX_DIM = 64
TOPK_MAX = 256
N_HEADS_B = 4
N_HEADS_C = 16
EPS = 1e-6
WIDTH_A = N_HEADS_A * HEAD_DIM
WIDTH_B = N_HEADS_B * 2 * HEAD_DIM
WIDTH_C = N_HEADS_C * HEAD_DIM
WIDTH_I = N_IDX_HEADS * IDX_DIM
QK_SCALE = HEAD_DIM ** -0.5
LOG2E = 1.4426950408889634

LANES = 128
SUBLANES = 8
MXU_DIM = 256
VMEM_LIMIT = 56 * 1024 * 1024
ROW_TILE = 256
Q_TILE = 512
IDX_Q_TILE = 256
K_TILE = 512
COUNT_ROWS = 64
COUNT_PARTS = 8
FLASH_SUB = 512
FLASH_GROUP = 8

NEG = -1e30
NT_DIMS = (((1,), (1,)), ((), ()))

KEY_NEG_INF = -2139095041


def _cparams(n_axes):
    return pltpu.CompilerParams(
        dimension_semantics=("arbitrary",) * n_axes, vmem_limit_bytes=VMEM_LIMIT)


def _div_pow2(x, d):
    assert d & (d - 1) == 0
    return lax.shift_right_logical(x, jnp.int32(d.bit_length() - 1))


def _sort_key(x):
    bits = pltpu.bitcast(x, I32)
    return bits ^ ((bits >> 31) & jnp.int32(0x7FFFFFFF))


def _rms_rows(x, g):
    ms = jnp.mean(x * x, axis=-1, keepdims=True)
    return x * lax.rsqrt(ms + EPS) * g


def _split2(x):
    hi = x.astype(BF16)
    return hi, (x - hi.astype(F32)).astype(BF16)


def _split3(x):
    hi = x.astype(BF16)
    r = x - hi.astype(F32)
    mid = r.astype(BF16)
    return hi, mid, (r - mid.astype(F32)).astype(BF16)


def _head_norm(h, bd, gain):
    w = h.shape[-1]
    hi, lo = _split2(h * h)
    bd16 = bd.astype(BF16)
    if w >= MXU_DIM:
        cols = [jnp.dot(hi[:, c * MXU_DIM:(c + 1) * MXU_DIM], bd16, preferred_element_type=F32)
                + jnp.dot(lo[:, c * MXU_DIM:(c + 1) * MXU_DIM], bd16, preferred_element_type=F32)
                for c in range(w // MXU_DIM)]
        ms = cols[0] if len(cols) == 1 else jnp.concatenate(cols, axis=-1)
    else:
        ms = (jnp.dot(hi, bd16[:w, :w], preferred_element_type=F32)
              + jnp.dot(lo, bd16[:w, :w], preferred_element_type=F32))
    return h * lax.rsqrt(ms + EPS) * gain


def _rope(y, rope_ref):
    c = rope_ref[:, 0:LANES]
    s_lo = rope_ref[:, LANES:2 * LANES]
    s_hi = rope_ref[:, 2 * LANES:3 * LANES]
    outs = []
    for j in range(y.shape[-1] // LANES):
        yc = y[:, j * LANES:(j + 1) * LANES]
        outs.append(yc * c + pltpu.roll(yc, LANES - 8, 1) * s_lo + pltpu.roll(yc, 8, 1) * s_hi)
    return outs[0] if len(outs) == 1 else jnp.concatenate(outs, axis=-1)


def _proj0_kernel(*refs, want):
    x_ref, g_ref, w_ref, ws_ref, rope_ref, gains_ref, gki_ref, bd_ref = refs[:8]
    o = dict(zip(want, refs[8:]))
    xb = _rms_rows(x_ref[...], g_ref[...]).astype(BF16)
    bd = bd_ref[...]
    w512 = WIDTH_A

    def piece(j):
        return jnp.dot(xb, w_ref[:, j * w512:(j + 1) * w512], preferred_element_type=F32)

    def put(name, val):
        if name in o:
            o[name][...] = val.astype(o[name].dtype)

    def needs(*names):
        return any(n in o for n in names)

    if needs("qa"):
        put("qa", _rope(_head_norm(piece(0), bd, gains_ref[0:1, :]), rope_ref) * QK_SCALE)
    if needs("ka", "ka16"):
        k_a = _rope(_head_norm(piece(1), bd, gains_ref[1:2, :]), rope_ref)
        put("ka", k_a)
        put("ka16", k_a)
    if needs("va"):
        put("va", piece(2))
    if needs("ga"):
        put("ga", piece(3))
    if needs("qi"):
        put("qi", _rope(piece(4), rope_ref))
    if needs("qb"):
        put("qb", _rope(_head_norm(piece(5), bd, gains_ref[2:3, :]), rope_ref) * QK_SCALE)
    if needs("kb", "kb16"):
        k_b = _rope(_head_norm(piece(6), bd, gains_ref[3:4, :]), rope_ref)
        put("kb", k_b)
        put("kb16", k_b)
    if needs("vb"):
        put("vb", piece(7))
    if needs("gb"):
        put("gb", piece(8))
    if needs("ki", "ki16", "wi"):
        hs = jnp.dot(xb, ws_ref[...], preferred_element_type=F32)
        k_i = _rope(_head_norm(hs[:, 0:LANES], bd, gki_ref[...]), rope_ref)
        put("ki", k_i[:, 0:IDX_DIM])
        put("ki16", k_i)
        put("wi", hs[:, LANES:LANES + N_IDX_HEADS] * (WIDTH_I ** -0.5))


def _proj1_kernel(*refs, want):
    x_ref, g_ref, w_ref, wf_ref, bf_ref, gains_ref, bd_ref = refs[:7]
    o = dict(zip(want, refs[7:]))
    xb = _rms_rows(x_ref[...], g_ref[...]).astype(BF16)
    bd = bd_ref[...]
    wc = WIDTH_C

    def piece(j):
        return jnp.dot(xb, w_ref[:, j * wc:(j + 1) * wc], preferred_element_type=F32)

    def put(name, val):
        if name in o:
            o[name][...] = val.astype(o[name].dtype)

    if "q16" in o:
        put("q16", _head_norm(piece(0), bd, gains_ref[0:1, :]) * QK_SCALE)
    if "k" in o or "k16" in o:
        k = _head_norm(piece(1), bd, gains_ref[1:2, :])
        put("k", k)
        put("k16", k)
    if "v" in o:
        put("v", piece(2))
    if "gate" in o:
        put("gate", piece(3))
    if "logf" in o:
        f = jnp.dot(xb, wf_ref[...], preferred_element_type=F32)[:, 0:N_HEADS_C] + bf_ref[...]
        put("logf", _log_sigmoid(f))


def _row_spec(tm, w):
    return pl.BlockSpec((tm, w), lambda i: (i, 0))


def _const_spec(shape):
    return pl.BlockSpec(shape, lambda i: (0,) * len(shape))


def _block_diag_mean():
    r = lax.broadcasted_iota(I32, (MXU_DIM, MXU_DIM), 0) // HEAD_DIM
    c = lax.broadcasted_iota(I32, (MXU_DIM, MXU_DIM), 1) // HEAD_DIM
    return jnp.where(r == c, 1.0 / HEAD_DIM, 0.0).astype(F32)


def _rope_table(pos):
    rot = HEAD_DIM // 4
    half = rot // 2
    inv = jnp.power(F32(ROPE_THETA), -jnp.arange(half, dtype=F32) * (2.0 / rot))
    ang = pos.astype(F32)[:, None] * inv[None, :]
    cos, sin = jnp.cos(ang), jnp.sin(ang)
    m = pos.shape[0]
    ones = jnp.ones((m, HEAD_DIM - rot), F32)
    zeros = jnp.zeros((m, HEAD_DIM - rot), F32)
    zh = jnp.zeros((m, half), F32)
    c = jnp.concatenate([cos, cos, ones], axis=-1)
    s_lo = jnp.concatenate([-sin, zh, zeros], axis=-1)
    s_hi = jnp.concatenate([zh, sin, zeros], axis=-1)
    return jnp.concatenate([c, c, s_lo, s_lo, s_hi, s_hi], axis=-1)


def _tile_gain(g, w):
    return jnp.tile(g.astype(F32), w // g.shape[0])[None, :]


P0_OUTS = dict(qa=(WIDTH_A, BF16), ka=(WIDTH_A, F32), ka16=(WIDTH_A, BF16), va=(WIDTH_A, F32), ga=(WIDTH_A, F32),
               qi=(WIDTH_I, BF16), ki=(IDX_DIM, F32), ki16=(LANES, BF16), wi=(N_IDX_HEADS, F32),
               qb=(WIDTH_B, BF16), kb=(WIDTH_B, F32), kb16=(WIDTH_B, BF16), vb=(WIDTH_B, F32), gb=(WIDTH_B, F32))
P1_OUTS = dict(q16=(WIDTH_C, BF16), k=(WIDTH_C, F32), k16=(WIDTH_C, BF16), v=(WIDTH_C, F32), gate=(WIDTH_C, F32),
               logf=(N_HEADS_C, F32))


def _even_splits():
    sizes = (WIDTH_A,) * 4 + (WIDTH_I, IDX_DIM, N_IDX_HEADS) + (WIDTH_B,) * 4
    offs = [0]
    for s in sizes:
        offs.append(offs[-1] + s)
    return offs


def _proj0(x2d, pos, p, tm, want):
    m, d = x2d.shape
    offs = _even_splits()
    w_in = p["w_in"]
    cols = [w_in[:, offs[i]:offs[i + 1]] for i in range(len(offs) - 1)]
    w_big = jnp.concatenate([cols[0], cols[1], cols[2], cols[3], cols[4], cols[7], cols[8], cols[9], cols[10]],
                            axis=1).astype(BF16)
    w_small = jnp.concatenate(
        [cols[5], cols[5], cols[6], jnp.zeros((d, LANES - N_IDX_HEADS), w_in.dtype)], axis=1).astype(BF16)
    gains = jnp.concatenate([_tile_gain(p["qn_a"], WIDTH_A), _tile_gain(p["kn_a"], WIDTH_A),
                             _tile_gain(p["qn_b"], WIDTH_B), _tile_gain(p["kn_b"], WIDTH_B)], axis=0)
    gki = _tile_gain(p["kn_idx"], LANES)
    rope = _rope_table(pos)
    out_shape = tuple(jax.ShapeDtypeStruct((m, P0_OUTS[n][0]), P0_OUTS[n][1]) for n in want)
    out_specs = tuple(_row_spec(tm, s.shape[1]) for s in out_shape)
    outs = pl.pallas_call(
        functools.partial(_proj0_kernel, want=tuple(want)),
        grid=(m // tm,),
        in_specs=[_row_spec(tm, d), _const_spec((1, d)), _const_spec(w_big.shape), _const_spec(w_small.shape),
                  _row_spec(tm, 3 * LANES), _const_spec(gains.shape), _const_spec(gki.shape),
                  _const_spec((MXU_DIM, MXU_DIM))],
        out_specs=out_specs,
        out_shape=out_shape,
        compiler_params=_cparams(1),
        name="proj0",
    )(x2d, p["norm"].astype(F32)[None, :], w_big, w_small, rope, gains, gki, _block_diag_mean())
    return dict(zip(want, outs))


def _proj1(x2d, p, tm, want):
    m, d = x2d.shape
    wc = WIDTH_C
    w_in = p["w_in"]
    w_big = w_in[:, :4 * wc].astype(BF16)
    w_f = jnp.concatenate([w_in[:, 4 * wc:], jnp.zeros((d, LANES - N_HEADS_C), w_in.dtype)], axis=1).astype(BF16)
    gains = jnp.concatenate([_tile_gain(p["qn"], wc), _tile_gain(p["kn"], wc)], axis=0)
    out_shape = tuple(jax.ShapeDtypeStruct((m, P1_OUTS[n][0]), P1_OUTS[n][1]) for n in want)
    out_specs = tuple(_row_spec(tm, s.shape[1]) for s in out_shape)
    outs = pl.pallas_call(
        functools.partial(_proj1_kernel, want=tuple(want)),
        grid=(m // tm,),
        in_specs=[_row_spec(tm, d), _const_spec((1, d)), _const_spec(w_big.shape), _const_spec(w_f.shape),
                  _const_spec((1, N_HEADS_C)), _const_spec(gains.shape), _const_spec((MXU_DIM, MXU_DIM))],
        out_specs=out_specs,
        out_shape=out_shape,
        compiler_params=_cparams(1),
        name="proj1",
    )(x2d, p["norm"].astype(F32)[None, :], w_big, w_f, p["b_f"].astype(F32)[None, :], gains, _block_diag_mean())
    return dict(zip(want, outs))


def _log_sigmoid(f):
    return jnp.minimum(f, 0.0) - jnp.log1p(jnp.exp(-jnp.abs(f)))


def _head_norm_t(h, bd, gain):
    w, tm = h.shape
    hi, lo = _split2(h * h)
    bd16 = bd.astype(BF16)
    if w >= MXU_DIM:
        slabs = [jnp.dot(bd16, hi[c * MXU_DIM:(c + 1) * MXU_DIM, :], preferred_element_type=F32)
                 + jnp.dot(bd16, lo[c * MXU_DIM:(c + 1) * MXU_DIM, :], preferred_element_type=F32)
                 for c in range(w // MXU_DIM)]
        ms = slabs[0] if len(slabs) == 1 else jnp.concatenate(slabs, axis=0)
    else:
        ms = (jnp.dot(bd16[:w, :w], hi, preferred_element_type=F32)
              + jnp.dot(bd16[:w, :w], lo, preferred_element_type=F32))
    return h * lax.rsqrt(ms + EPS) * jnp.tile(gain, (1, tm // LANES))


def _rope_t(y, rope_ref):
    half = HEAD_DIM // 8
    cos = rope_ref[0:half, :]
    sin = rope_ref[half:2 * half, :]
    parts = []
    for h in range(y.shape[0] // HEAD_DIM):
        r0 = h * HEAD_DIM
        x1 = y[r0:r0 + half, :]
        x2 = y[r0 + half:r0 + 2 * half, :]
        parts += [x1 * cos - x2 * sin, x2 * cos + x1 * sin, y[r0 + 2 * half:r0 + HEAD_DIM, :]]
    return jnp.concatenate(parts, axis=0)


def _proj_t_kernel(*refs, specs, pw, small):
    x_ref, g_ref, wt_ref, ws_ref, rope_ref, gains_ref, gs_ref, bd_ref = refs[:8]
    outs = list(refs[8:])
    xb = _rms_rows(x_ref[...], g_ref[...]).astype(BF16)
    bd = bd_ref[...]
    tm = xb.shape[0]
    k = 0
    for j, (gain_row, rope, scale, kinds) in enumerate(specs):
        h = lax.dot_general(wt_ref[j * pw:(j + 1) * pw, :], xb, NT_DIMS, preferred_element_type=F32)
        if gain_row is not None:
            h = _head_norm_t(h, bd, gains_ref[gain_row])
        if rope:
            h = _rope_t(h, rope_ref)
        if scale != 1.0:
            h = h * scale
        for dt in kinds:
            outs[k][0] = h.astype(dt)
            k += 1
    if small == "idx":
        hs = lax.dot_general(ws_ref[...], xb, NT_DIMS, preferred_element_type=F32)
        outs[k][0] = _rope_t(_head_norm_t(hs[0:IDX_DIM, :], bd, gs_ref[...]), rope_ref)
        outs[k + 1][0] = hs[IDX_DIM:IDX_DIM + N_IDX_HEADS, :] * (WIDTH_I ** -0.5)
    if small == "logf":
        f = lax.dot_general(ws_ref[...], xb, NT_DIMS, preferred_element_type=F32)
        outs[k][0] = _log_sigmoid(f + jnp.tile(gs_ref[...], (1, tm // LANES)))


def _rope_table_t(pos):
    rot = HEAD_DIM // 4
    half = rot // 2
    inv = jnp.power(F32(ROPE_THETA), -jnp.arange(half, dtype=F32) * (2.0 / rot))
    ang = inv[:, None] * pos.astype(F32)[None, :]
    return jnp.concatenate([jnp.cos(ang), jnp.sin(ang)], axis=0)


def _lane_replicated(col):
    return jnp.broadcast_to(col.astype(F32)[:, None], (col.shape[0], LANES))


def _proj_t(x3, norm, blocks, gains, pos, small, tm):
    b, seq, d = x3.shape
    m = b * seq
    nt = seq // tm
    pw = blocks[0][0].shape[1]
    wt = jnp.concatenate([blk[0].T for blk in blocks], axis=0).astype(BF16)
    specs = tuple(tuple(blk[1:]) for blk in blocks)
    gains_arr = (jnp.zeros((1, pw, LANES), F32) if not gains else
                 jnp.stack([_lane_replicated(jnp.tile(g, pw // g.shape[0])) for g in gains]))
    rope = jnp.zeros((2 * SUBLANES, m), F32) if pos is None else _rope_table_t(pos)
    out_rows = [(pw, dt) for blk in blocks for dt in blk[4]]
    kind = None
    ws = jnp.zeros((SUBLANES, d), BF16)
    gs = jnp.zeros((SUBLANES, LANES), F32)
    if small is not None:
        kind = small[0]
        if kind == "idx":
            ws = jnp.concatenate([small[1].T, small[2].T], axis=0).astype(BF16)
            gs = _lane_replicated(small[3])
            out_rows += [(IDX_DIM, F32), (N_IDX_HEADS, F32)]
        else:
            ws = small[1].T.astype(BF16)
            gs = _lane_replicated(small[2])
            out_rows += [(N_HEADS_C, F32)]
    out_shape = [jax.ShapeDtypeStruct((b, r, seq), dt) for r, dt in out_rows]
    out_specs = [pl.BlockSpec((1, r, tm), lambda i: (i // nt, 0, i % nt)) for r, _ in out_rows]
    return pl.pallas_call(
        functools.partial(_proj_t_kernel, specs=specs, pw=pw, small=kind),
        grid=(m // tm,),
        in_specs=[_row_spec(tm, d), _const_spec((1, d)), _const_spec(wt.shape), _const_spec(ws.shape),
                  pl.BlockSpec((2 * SUBLANES, tm), lambda i: (0, i)), _const_spec(gains_arr.shape),
                  _const_spec(gs.shape), _const_spec((MXU_DIM, MXU_DIM))],
        out_specs=out_specs,
        out_shape=out_shape,
        compiler_params=_cparams(1),
        name="proj_t",
    )(x3.reshape(m, d), norm.astype(F32)[None, :], wt, ws, rope, gains_arr, gs, _block_diag_mean())


def _silu(g):
    return g * jax.nn.sigmoid(g)


def _out0_kernel(x_ref, oa_ref, ga_ref, ob_ref, gb_ref, sub_ref, w_ref, y_ref, *, post_scale):
    ya = (oa_ref[...] * _silu(ga_ref[...])).astype(BF16)
    ob = ob_ref[...]
    cols = []
    for h in range(N_HEADS_B):
        oc = ob[:, h * LANES:(h + 1) * LANES]
        ms = jnp.mean(oc * oc, axis=-1, keepdims=True)
        cols.append(oc * lax.rsqrt(ms + EPS))
    obn = jnp.concatenate(cols, axis=-1) * sub_ref[...] * post_scale
    yb = (obn * _silu(gb_ref[...])).astype(BF16)
    y = jnp.dot(ya, w_ref[0:WIDTH_A, :], preferred_element_type=F32)
    y = y + jnp.dot(yb, w_ref[WIDTH_A:WIDTH_A + WIDTH_B, :], preferred_element_type=F32)
    y_ref[...] = x_ref[...] + y


def _out1_kernel(x_ref, o_ref, g_ref, w_ref, y_ref):
    yo = (o_ref[...] * _silu(g_ref[...])).astype(BF16)
    y_ref[...] = x_ref[...] + jnp.dot(yo, w_ref[...], preferred_element_type=F32)


def _out0(x2d, o_a, g_a, o_b, g_b, p, lam_init, tm):
    m, d = x2d.shape
    sub = _tile_gain(p["subln_b"], WIDTH_B)
    w = p["w_out"].astype(BF16)
    return pl.pallas_call(
        functools.partial(_out0_kernel, post_scale=1.0 - lam_init),
        grid=(m // tm,),
        in_specs=[_row_spec(tm, d), _row_spec(tm, WIDTH_A), _row_spec(tm, WIDTH_A), _row_spec(tm, WIDTH_B),
                  _row_spec(tm, WIDTH_B), _const_spec(sub.shape), _const_spec(w.shape)],
        out_specs=_row_spec(tm, d),
        out_shape=jax.ShapeDtypeStruct((m, d), F32),
        compiler_params=_cparams(1),
        name="out0",
    )(x2d, o_a, g_a, o_b, g_b, sub, w)


def _out1(x2d, o, g, p, tm):
    m, d = x2d.shape
    w = p["w_out"].astype(BF16)
    return pl.pallas_call(
        _out1_kernel,
        grid=(m // tm,),
        in_specs=[_row_spec(tm, d), _row_spec(tm, WIDTH_C), _row_spec(tm, WIDTH_C), _const_spec(w.shape)],
        out_specs=_row_spec(tm, d),
        out_shape=jax.ShapeDtypeStruct((m, d), F32),
        compiler_params=_cparams(1),
        name="out1",
    )(x2d, o, g, w)


def _select_topk(load_keys, n_chunks, groups, rows, nsel, idx_bits, idx_of):
    shape = (rows, LANES)
    rb = min(rows, COUNT_ROWS)
    n_blk = rows // rb

    def count(pred_of):
        accs = []
        for b in range(n_blk):
            pred = pred_of(slice(b * rb, (b + 1) * rb))

            def body(c, acc, b=b, pred=pred):
                blk = load_keys(c, b * rb, rb)
                for g in range(groups):
                    acc = jnp.where(pred(blk[:, g * LANES:(g + 1) * LANES], c, g), acc + 1.0, acc)
                return acc
            accs.append(lax.fori_loop(0, n_chunks, body, jnp.zeros((rb, LANES), F32),
                                      unroll=isinstance(n_chunks, int)))
        outs = [jnp.broadcast_to(jnp.sum(a, axis=-1, keepdims=True), (rb, LANES)) for a in accs]
        return outs[0] if n_blk == 1 else jnp.concatenate(outs, axis=0)

    kf = float(nsel)

    def bit_step(i, thr):
        bit = lax.shift_left(jnp.int32(1), jnp.int32(31) - i)
        cand = thr ^ bit

        def pred_of(sl):
            cb = cand[sl]
            return lambda k, c, g: k >= cb
        return jnp.where(count(pred_of) >= kf, cand, thr)

    thr = lax.fori_loop(0, 32, bit_step, jnp.full(shape, jnp.iinfo(jnp.int32).min, I32))
    thr = jnp.maximum(thr, KEY_NEG_INF + 1)

    def gt_of(sl):
        tb = thr[sl]
        return lambda k, c, g: k > tb

    def ge_of(sl):
        tb = thr[sl]
        return lambda k, c, g: k >= tb

    n_gt = count(gt_of)
    n_ge = count(ge_of)
    need = kf - n_gt
    excess = n_ge - kf

    def tie_cut():
        def idx_step(i, cut):
            bit = lax.shift_left(jnp.int32(1), jnp.int32(idx_bits - 1) - i)
            cand = cut | bit

            def pred_of(sl):
                tb, cb = thr[sl], cand[sl]
                return lambda k, c, g: (k == tb) & (idx_of(c, g, rb) < cb)
            return jnp.where(count(pred_of) < need, cand, cut)
        return lax.fori_loop(0, idx_bits, idx_step, jnp.zeros(shape, I32))

    big = jnp.full(shape, jnp.iinfo(jnp.int32).max, I32)
    any_excess = jnp.max(excess) > 0.0
    cut = lax.cond(any_excess, lambda: jnp.where(excess > 0.0, tie_cut(), big), lambda: big)
    return thr, cut


def _select_topk_t(load_keys, n_chunks, rpc, cols, nsel, idx_bits):
    shape = (SUBLANES, cols)
    reps = rpc // SUBLANES
    n_part = math.gcd(reps, COUNT_PARTS)

    def count(pred):
        def body(c, acc):
            x = jnp.where(pred(load_keys(c).reshape(reps, SUBLANES, cols), c), 1.0, 0.0)
            part = jnp.sum(x.reshape(reps // n_part, n_part, SUBLANES, cols), axis=0)
            return acc + jnp.sum(part, axis=0)
        acc = lax.fori_loop(0, n_chunks, body, jnp.zeros(shape, F32))
        return jnp.broadcast_to(jnp.sum(acc, axis=0, keepdims=True), shape)

    kf = float(nsel)

    def bit_step(state):
        i, thr, n_thr = state
        bit = lax.shift_left(jnp.int32(1), jnp.int32(31) - i)
        cand = thr ^ bit
        n_cand = count(lambda k, c: k >= cand[None])
        take = n_cand >= kf
        return i + 1, jnp.where(take, cand, thr), jnp.where(take, n_cand, n_thr)

    def unresolved(state):
        i, _, n_thr = state
        return (i < 32) & (jnp.max(jnp.abs(n_thr - kf)) > 0.0)

    _, thr, n_thr = lax.while_loop(
        unresolved, bit_step,
        (jnp.int32(0), jnp.full(shape, jnp.iinfo(jnp.int32).min, I32), jnp.full(shape, 2.0 ** 30, F32)))
    excess = jnp.where(thr > KEY_NEG_INF, n_thr - kf, -1.0)
    thr = jnp.maximum(thr, KEY_NEG_INF + 1)
    row = lax.broadcasted_iota(I32, (rpc, cols), 0).reshape(reps, SUBLANES, cols)

    def tie_cut():
        need = kf - count(lambda k, c: k > thr[None])

        def idx_step(i, cut):
            bit = lax.shift_left(jnp.int32(1), jnp.int32(idx_bits - 1) - i)
            cand = cut | bit
            n_lt = count(lambda k, c: (k == thr[None]) & (row + c * rpc < cand[None]))
            return jnp.where(n_lt < need, cand, cut)
        return lax.fori_loop(0, idx_bits, idx_step, jnp.zeros(shape, I32))

    big = jnp.full(shape, jnp.iinfo(jnp.int32).max, I32)
    any_excess = jnp.max(excess) > 0.0
    cut = lax.cond(any_excess, lambda: jnp.where(excess > 0.0, tie_cut(), big), lambda: big)
    return thr, cut


def _masked_pairs_t(qt_ref, qm_ref, n_pairs, tq):
    top = lax.broadcasted_iota(I32, (LANES, tq), 0) < HEAD_DIM
    for j in range(n_pairs):
        pair = qt_ref[0, j * LANES:(j + 1) * LANES, :]
        zero = jnp.zeros_like(pair)
        qm_ref[2 * j] = jnp.where(top, pair, zero)
        qm_ref[2 * j + 1] = jnp.where(top, zero, pair)


def _idx_prompt_t_kernel(qit_ref, wit_ref, ki_ref, bias_ref, qm_ref, s_ref, *, tq, tk, nsel, seq):
    qi = pl.program_id(1)
    q0 = qi * tq
    n_chunks = seq // tk
    nc = (q0 + tq + tk - 1) // tk
    _masked_pairs_t(qit_ref, qm_ref, N_IDX_HEADS // 2, tq)
    wt = wit_ref[0]
    col = q0 + lax.broadcasted_iota(I32, (tk, tq), 1)
    row0 = lax.broadcasted_iota(I32, (tk, tq), 0)

    def score_body(c, carry):
        start = pl.multiple_of(c * tk, tk)
        kblk = ki_ref[0, pl.ds(start, tk), :]
        acc = jnp.zeros((tk, tq), F32)
        for h in range(N_IDX_HEADS):
            s = jnp.dot(kblk, qm_ref[h], preferred_element_type=F32)
            acc = acc + jnp.maximum(s, 0.0) * wt[h:h + 1, :]
        acc = jnp.where(row0 + c * tk <= col, acc, -jnp.inf)
        s_ref[c] = _sort_key(acc)
        return carry

    lax.fori_loop(0, nc, score_body, 0)
    thr, cut = _select_topk_t(lambda c: s_ref[c], nc, tk, tq, nsel, int(math.log2(seq)) + 1)
    thr_t = jnp.tile(thr, (tk // SUBLANES, 1))
    cut_t = jnp.tile(cut, (tk // SUBLANES, 1))

    def write_body(c, carry):
        k = s_ref[c]
        sel = (k > thr_t) | ((k == thr_t) & (row0 + c * tk <= cut_t))
        start = pl.multiple_of(c * tk, tk)
        bias_ref[0, 0, pl.ds(start, tk), :] = jnp.where(sel, 0.0, NEG).astype(BF16)
        return carry

    lax.fori_loop(0, nc, write_body, 0)

    def fill_body(c, carry):
        start = pl.multiple_of(c * tk, tk)
        bias_ref[0, 0, pl.ds(start, tk), :] = jnp.full((tk, tq), NEG, BF16)
        return carry

    lax.fori_loop(nc, n_chunks, fill_body, 0)


def _idx_prompt_t(qit, wit, ki16, nsel, tq, tq_out, tk):
    b, seq, _ = ki16.shape
    nq = seq // tq
    per = tq_out // tq
    return pl.pallas_call(
        functools.partial(_idx_prompt_t_kernel, tq=tq, tk=tk, nsel=nsel, seq=seq),
        grid=(b, nq),
        in_specs=[pl.BlockSpec((1, WIDTH_I, tq), lambda bi, qi: (bi, 0, qi)),
                  pl.BlockSpec((1, N_IDX_HEADS, tq), lambda bi, qi: (bi, 0, qi)),
                  pl.BlockSpec((1, seq, LANES), lambda bi, qi: (bi, 0, 0))],
        out_specs=pl.BlockSpec((1, 1, seq, tq), lambda bi, qi: (bi, qi // per, 0, qi % per)),
        out_shape=jax.ShapeDtypeStruct((b, seq // tq_out, seq, tq_out), BF16),
        scratch_shapes=[pltpu.VMEM((N_IDX_HEADS, LANES, tq), BF16),
                        pltpu.VMEM((seq // tk, tk, tq), I32)],
        compiler_params=_cparams(2),
        name="idx_prompt",
    )(qit, wit, ki16)


def _flash_t_kernel(*refs, variant, n_pairs, tq, tk, sub):
    it = iter(refs)
    k_ref, qt_ref, vt_ref = next(it), next(it), next(it)
    bias_ref = next(it) if variant == "dsa" else None
    cqt_ref, ck_ref = (next(it), next(it)) if variant == "fox" else (None, None)
    lam_ref = next(it) if variant == "diff" else None
    o_ref = next(it)
    qm_ref, m_ref, l_ref, acc_ref = next(it), next(it), next(it), next(it)
    bias32_ref = next(it) if variant == "dsa" else None

    qi = pl.program_id(1)
    kc = pl.program_id(2)
    nk = pl.num_programs(2)
    last = ((qi + 1) * tq - 1) // tk
    n_units = 2 * n_pairs
    vr = acc_ref.shape[1]

    @pl.when(kc == 0)
    def _init():
        _masked_pairs_t(qt_ref, qm_ref, n_pairs, tq)
        m_ref[...] = jnp.full(m_ref.shape, NEG, F32)
        l_ref[...] = jnp.zeros(l_ref.shape, F32)
        acc_ref[...] = jnp.zeros(acc_ref.shape, F32)

    def compute(masked):
        if variant == "dsa":
            bias32_ref[...] = bias_ref[0, 0].astype(F32)
            masked = False
        if masked:
            col = qi * tq + lax.broadcasted_iota(I32, (sub, tq), 1)
            row0 = kc * tk + lax.broadcasted_iota(I32, (sub, tq), 0)
        ones_rows = jnp.ones((2 * SUBLANES, sub), BF16)
        if variant == "fox":
            ck_all = ck_ref[0] * LOG2E
        for g0 in range(0, n_units, FLASH_GROUP):
            units = list(range(g0, min(g0 + FLASH_GROUP, n_units)))
            state = [[m_ref[u], l_ref[u], acc_ref[u]] for u in units]
            for r0 in range(0, tk, sub):
                ss = [jnp.dot(k_ref[0, r0:r0 + sub, (u // 2) * LANES:(u // 2 + 1) * LANES], qm_ref[u],
                              preferred_element_type=F32) for u in units]
                ps = []
                for e, u in enumerate(units):
                    s = ss[e]
                    if variant == "dsa":
                        s = s + bias32_ref[r0:r0 + sub, :]
                    cq_row = 0.0
                    if variant == "fox":
                        s = s - jnp.broadcast_to(ck_all[r0:r0 + sub, u:u + 1], (sub, tq))
                        cq_row = cqt_ref[0, u:u + 1, :] * LOG2E
                    if masked:
                        s = jnp.where(row0 + r0 <= col, s, NEG)
                    m_prev, l_prev, acc = state[e]
                    m_new = jnp.maximum(m_prev, jnp.max(s, axis=0, keepdims=True) + cq_row)
                    alpha = jnp.exp2(m_prev - m_new)
                    p = jnp.exp2(s - jnp.tile(m_new - cq_row, (sub // SUBLANES, 1)))
                    state[e][0] = m_new
                    state[e][1] = alpha * l_prev
                    state[e][2] = jnp.tile(alpha, (vr // SUBLANES, 1)) * acc
                    ps.append(p.astype(BF16))
                for e, u in enumerate(units):
                    v0 = (u // 2) * vr if variant == "diff" else u * vr
                    v_aug = jnp.concatenate([vt_ref[0, v0:v0 + vr, r0:r0 + sub], ones_rows], axis=0)
                    pv = jnp.dot(v_aug, ps[e], preferred_element_type=F32)
                    state[e][1] = state[e][1] + pv[vr:vr + SUBLANES, :]
                    state[e][2] = state[e][2] + pv[0:vr, :]
            for e, u in enumerate(units):
                m_ref[u], l_ref[u], acc_ref[u] = state[e]

    needs_mask = (kc + 1) * tk - 1 > qi * tq

    @pl.when((kc <= last) & needs_mask)
    def _diag():
        compute(True)

    @pl.when((kc <= last) & jnp.logical_not(needs_mask))
    def _full():
        compute(False)

    @pl.when(kc == nk - 1)
    def _fin():
        def norm(u):
            return acc_ref[u] / jnp.tile(l_ref[u], (vr // SUBLANES, 1))
        for j in range(n_pairs):
            if variant == "diff":
                ot = norm(2 * j) - lam_ref[0, 0] * norm(2 * j + 1)
            else:
                ot = jnp.concatenate([norm(2 * j), norm(2 * j + 1)], axis=0)
            o_ref[0, :, j * LANES:(j + 1) * LANES] = ot.T


def _flash_t(variant, k16, qt, vt, tq, tk, bias=None, cqt=None, ck=None, lam=None):
    b, seq, w = k16.shape
    n_pairs = w // LANES
    n_units = 2 * n_pairs
    nq, nk = seq // tq, seq // tk
    vr = LANES if variant == "diff" else HEAD_DIM
    sub = min(tk, FLASH_SUB)

    def kc_of(qi, kc):
        return jnp.minimum(kc, ((qi + 1) * tq - 1) // tk)

    in_specs = [pl.BlockSpec((1, tk, w), lambda bi, qi, kc: (bi, kc_of(qi, kc), 0)),
                pl.BlockSpec((1, w, tq), lambda bi, qi, kc: (bi, 0, qi)),
                pl.BlockSpec((1, vt.shape[1], tk), lambda bi, qi, kc: (bi, 0, kc_of(qi, kc)))]
    args = [k16, qt, vt]
    scratch = [pltpu.VMEM((n_units, LANES, tq), BF16), pltpu.VMEM((n_units, SUBLANES, tq), F32),
               pltpu.VMEM((n_units, SUBLANES, tq), F32), pltpu.VMEM((n_units, vr, tq), F32)]
    if variant == "dsa":
        in_specs.append(pl.BlockSpec((1, 1, tk, tq), lambda bi, qi, kc: (bi, qi, kc_of(qi, kc), 0)))
        args.append(bias)
        scratch.append(pltpu.VMEM((tk, tq), F32))
    if variant == "fox":
        in_specs.append(pl.BlockSpec((1, n_units, tq), lambda bi, qi, kc: (bi, 0, qi)))
        in_specs.append(pl.BlockSpec((1, tk, n_units), lambda bi, qi, kc: (bi, kc_of(qi, kc), 0)))
        args += [cqt, ck]
    if variant == "diff":
        in_specs.append(pl.BlockSpec(memory_space=pltpu.SMEM))
        args.append(lam)
    return pl.pallas_call(
        functools.partial(_flash_t_kernel, variant=variant, n_pairs=n_pairs, tq=tq, tk=tk, sub=sub),
        grid=(b, nq, nk),
        in_specs=in_specs,
        out_specs=pl.BlockSpec((1, tq, w), lambda bi, qi, kc: (bi, qi, 0)),
        out_shape=jax.ShapeDtypeStruct((b, seq, w), F32),
        scratch_shapes=scratch,
        compiler_params=_cparams(3),
        name="flash_" + variant,
    )(*args)


PAGES_PER_STEP = 8


def _pages_per_step(n_pages):
    g = PAGES_PER_STEP
    while n_pages % g:
        g //= 2
    return g


def _upper_ones():
    r = lax.broadcasted_iota(I32, (LANES, LANES), 0)
    c = lax.broadcasted_iota(I32, (LANES, LANES), 1)
    return jnp.where(r <= c, 1.0, 0.0).astype(BF16)


def _cumsum_kernel(*refs, g):
    x_refs = refs[1:1 + g]
    xn_ref, c_ref, cn_ref, carry_ref = refs[1 + g:5 + g]
    p = pl.program_id(1)
    upper = _upper_ones()

    @pl.when(p == 0)
    def _():
        carry_ref[...] = jnp.zeros(carry_ref.shape, F32)

    def page_cumsum(x):
        return sum(jnp.dot(t, upper, preferred_element_type=F32) for t in _split3(x))

    local = [page_cumsum(x_refs[j][0]) for j in range(g)]
    total = carry_ref[...]
    for j in range(g):
        c_ref[0, :, j * LANES:(j + 1) * LANES] = local[j] + total
        total = total + jnp.broadcast_to(local[j][:, LANES - 1:LANES], total.shape)
    carry_ref[...] = total

    @pl.when(p == pl.num_programs(1) - 1)
    def _():
        cn_ref[0] = page_cumsum(xn_ref[0]) + total


def _cumsum_pages(page_table, pool_t, new_t):
    b, n_pages = page_table.shape
    h = pool_t.shape[1]
    g = _pages_per_step(n_pages)
    pool_specs = [pl.BlockSpec((1, h, LANES), lambda bi, p, pt, j=j: (pt[bi, p * g + j], 0, 0)) for j in range(g)]
    grid_spec = pltpu.PrefetchScalarGridSpec(
        num_scalar_prefetch=1,
        grid=(b, n_pages // g),
        in_specs=pool_specs + [pl.BlockSpec((1, h, LANES), lambda bi, p, pt: (bi, 0, 0))],
        out_specs=[pl.BlockSpec((1, h, g * LANES), lambda bi, p, pt: (bi, 0, p)),
                   pl.BlockSpec((1, h, LANES), lambda bi, p, pt: (bi, 0, 0))],
        scratch_shapes=[pltpu.VMEM((h, LANES), F32)],
    )
    return pl.pallas_call(
        functools.partial(_cumsum_kernel, g=g),
        grid_spec=grid_spec,
        out_shape=(jax.ShapeDtypeStruct((b, h, n_pages * LANES), F32), jax.ShapeDtypeStruct((b, h, LANES), F32)),
        compiler_params=_cparams(2),
        name="cumsum_pages",
    )(page_table, *([pool_t] * g), new_t)


def _idx_decode_kernel(*refs, g, n_pages, n_new, nsel, page):
    q_ref, w_ref = refs[1:3]
    k_refs = refs[3:3 + g]
    knew_ref, bias_ref, biasn_ref, s_ref = refs[3 + g:7 + g]
    p = pl.program_id(1)
    n_steps = n_pages // g
    rows = n_new

    def scores(kt):
        n = kt.shape[1] // LANES
        s = jnp.dot(q_ref[0], kt.astype(BF16), preferred_element_type=F32)
        t = jnp.maximum(s, 0.0) * jnp.tile(w_ref[0], (1, n))
        acc = t[0:rows]
        for h in range(1, N_IDX_HEADS):
            acc = acc + t[h * rows:(h + 1) * rows]
        return acc

    @pl.when(p < n_steps)
    def _past():
        keys = _sort_key(scores(jnp.concatenate([k_refs[j][0] for j in range(g)], axis=1)))
        for j in range(g):
            s_ref[p * g + j] = keys[:, j * LANES:(j + 1) * LANES]

    @pl.when(p == n_steps)
    def _new():
        sc = scores(knew_ref[0])
        i = lax.broadcasted_iota(I32, (rows, LANES), 0)
        lane = lax.broadcasted_iota(I32, (rows, LANES), 1)
        s_ref[n_pages] = _sort_key(jnp.where(lane <= i, sc, -jnp.inf))

        def idx_of(c, grp, nr):
            return lane + c * page

        n_keys = (n_pages + 1) * page
        thr, cut = _select_topk(lambda c, r0, nr: s_ref[c], n_pages + 1, 1, rows, nsel,
                                int(math.log2(n_keys)) + 1, idx_of)

        def selected(c):
            k = s_ref[c]
            sel = (k > thr) | ((k == thr) & (lane + c * page <= cut))
            return jnp.where(sel, 0.0, NEG)

        def write_body(c, carry):
            bias_ref[0, c] = selected(c)
            return carry

        lax.fori_loop(0, n_pages, write_body, 0)
        biasn_ref[0] = selected(n_pages)


def _idx_decode(page_table, q_st, w_st, kt_pool, kt_new, nsel):
    b, n_pages = page_table.shape
    page = kt_pool.shape[2]
    n_new = q_st.shape[1] // N_IDX_HEADS
    g = _pages_per_step(n_pages)
    n_steps = n_pages // g

    def req_map(bi, p, pt):
        return (bi, 0, 0)

    pool_specs = [pl.BlockSpec((1, IDX_DIM, page),
                               lambda bi, p, pt, j=j: (pt[bi, jnp.minimum(p, n_steps - 1) * g + j], 0, 0))
                  for j in range(g)]
    grid_spec = pltpu.PrefetchScalarGridSpec(
        num_scalar_prefetch=1,
        grid=(b, n_steps + 1),
        in_specs=[pl.BlockSpec((1,) + q_st.shape[1:], req_map), pl.BlockSpec((1,) + w_st.shape[1:], req_map)]
        + pool_specs + [pl.BlockSpec((1, IDX_DIM, page), req_map)],
        out_specs=[pl.BlockSpec((1, n_pages, n_new, LANES), lambda bi, p, pt: (bi, 0, 0, 0)),
                   pl.BlockSpec((1, n_new, LANES), req_map)],
        scratch_shapes=[pltpu.VMEM((n_pages + 1, n_new, LANES), I32)],
    )
    return pl.pallas_call(
        functools.partial(_idx_decode_kernel, g=g, n_pages=n_pages, n_new=n_new, nsel=nsel, page=page),
        grid_spec=grid_spec,
        out_shape=(jax.ShapeDtypeStruct((b, n_pages, n_new, LANES), F32),
                   jax.ShapeDtypeStruct((b, n_new, LANES), F32)),
        compiler_params=_cparams(2),
        name="idx_decode",
    )(page_table, q_st, w_st, *([kt_pool] * g), kt_new)


def _attn_decode_kernel(*refs, variant, g, n_units, n_new, n_pages):
    it = iter(refs)
    next(it)
    q_ref = next(it)
    k_refs = [next(it) for _ in range(g)]
    v_refs = [next(it) for _ in range(g)]
    knew_ref, vnew_ref = next(it), next(it)
    bias_ref, biasn_ref = (next(it), next(it)) if variant == "dsa" else (None, None)
    cq_ref, ck_ref, cn_ref = (next(it), next(it), next(it)) if variant == "fox" else (None, None, None)
    lam_ref = next(it) if variant == "diff" else None
    o_ref = next(it)
    qbd_ref, m_ref, l_ref, acc_ref = next(it), next(it), next(it), next(it)

    p = pl.program_id(1)
    n_steps = n_pages // g
    rows = n_units * n_new
    w = q_ref.shape[-1]
    wa = acc_ref.shape[-1]

    @pl.when(p == 0)
    def _init():
        q = q_ref[0].astype(F32)
        qt = jnp.concatenate([q] * n_units, axis=0)
        r = _div_pow2(lax.broadcasted_iota(I32, (rows, w), 0), n_new)
        c = _div_pow2(lax.broadcasted_iota(I32, (rows, w), 1), HEAD_DIM)
        qbd_ref[...] = jnp.where(r == c, qt, 0.0).astype(BF16)
        m_ref[...] = jnp.full(m_ref.shape, NEG, F32)
        l_ref[...] = jnp.zeros(l_ref.shape, F32)
        acc_ref[...] = jnp.zeros(acc_ref.shape, F32)

    def expand_rows(x8):
        return jnp.concatenate([x8] * n_units, axis=0)

    def expand_units(xu):
        return jnp.concatenate([jnp.broadcast_to(xu[u:u + 1, :], (n_new, xu.shape[1])) for u in range(n_units)],
                               axis=0)

    def v_head(v_ref, h, is_new):
        if is_new or len(v_ref.shape) == 3:
            return v_ref[0, :, h * LANES:(h + 1) * LANES]
        return v_ref[0, :, h, :]

    def step(kts, vs, bias, ck, is_new):
        ng = len(kts)
        kt_all = kts[0][0] if ng == 1 else jnp.concatenate([kts[j][0] for j in range(ng)], axis=1)
        s = jnp.dot(qbd_ref[...], kt_all.astype(BF16), preferred_element_type=F32)
        if variant == "dsa":
            s = s + bias
        if variant == "fox":
            s = s + (jnp.tile(cq_ref[0], (1, ng)) - expand_units(ck))
        if is_new:
            i = expand_rows(lax.broadcasted_iota(I32, (n_new, LANES), 0))
            lane = lax.broadcasted_iota(I32, (rows, LANES), 1)
            s = jnp.where(lane <= i, s, NEG)
        m_prev = m_ref[...]
        m_new = jnp.maximum(m_prev, jnp.max(s, axis=-1, keepdims=True))
        alpha = jnp.exp(m_prev - m_new)
        pr = jnp.exp(s - jnp.tile(m_new, (1, ng)))
        l_ref[...] = alpha * l_ref[...] + jnp.sum(pr, axis=-1, keepdims=True)
        m_ref[...] = m_new
        p16 = pr.astype(BF16)
        if variant == "diff":
            hr = 2 * n_new
            pv = jnp.concatenate(
                [jnp.dot(p16[h * hr:(h + 1) * hr],
                         jnp.concatenate([v_head(vs[j], h, is_new) for j in range(ng)], axis=0).astype(BF16),
                         preferred_element_type=F32) for h in range(n_units // 2)], axis=0)
        else:
            vt_all = vs[0][0] if ng == 1 else jnp.concatenate([vs[j][0] for j in range(ng)], axis=1)
            pv = lax.dot_general(p16, vt_all.astype(BF16), NT_DIMS, preferred_element_type=F32)
        acc_ref[...] = jnp.tile(alpha, (1, wa // LANES)) * acc_ref[...] + pv

    @pl.when(p < n_steps)
    def _past():
        bias = ck = None
        if variant == "dsa":
            bias = jnp.concatenate([expand_rows(bias_ref[0, j]) for j in range(g)], axis=1)
        if variant == "fox":
            ck = ck_ref[0]
        step(k_refs, v_refs, bias, ck, False)

    @pl.when(p == n_steps)
    def _new():
        bias = expand_rows(biasn_ref[0]) if variant == "dsa" else None
        ck = cn_ref[0] if variant == "fox" else None
        step([knew_ref], [vnew_ref], bias, ck, True)
        accn = acc_ref[...] / jnp.tile(l_ref[...], (1, wa // LANES))
        if variant == "diff":
            lam = lam_ref[0, 0]
            o_ref[0] = jnp.concatenate(
                [accn[(2 * h) * n_new:(2 * h + 1) * n_new] - lam * accn[(2 * h + 1) * n_new:(2 * h + 2) * n_new]
                 for h in range(n_units // 2)], axis=1)
        else:
            cu = _div_pow2(lax.broadcasted_iota(I32, (n_new, wa), 1), HEAD_DIM)
            out = jnp.zeros((n_new, wa), F32)
            for u in range(n_units):
                out = out + jnp.where(cu == u, accn[u * n_new:(u + 1) * n_new], 0.0)
            o_ref[0] = out


def _attn_decode(variant, page_table, q16, kt_pool, v_pool, kt_new, v_new,
                  bias=None, bias_new=None, cq=None, ck=None, cn=None, lam=None):
    b, n_pages = page_table.shape
    page = kt_pool.shape[2]
    n_new, w = q16.shape[1], q16.shape[2]
    n_units = w // HEAD_DIM
    rows = n_units * n_new
    g = _pages_per_step(n_pages)
    n_steps = n_pages // g
    wa = LANES if variant == "diff" else w

    def req_map(bi, p, pt):
        return (bi, 0, 0)

    def pool_spec(arr, j):
        nd = arr.ndim
        return pl.BlockSpec((1,) + arr.shape[1:],
                            lambda bi, p, pt: (pt[bi, jnp.minimum(p, n_steps - 1) * g + j],) + (0,) * (nd - 1))

    in_specs = ([pl.BlockSpec((1, n_new, w), req_map)]
                + [pool_spec(kt_pool, j) for j in range(g)] + [pool_spec(v_pool, j) for j in range(g)]
                + [pl.BlockSpec((1,) + kt_new.shape[1:], req_map), pl.BlockSpec((1,) + v_new.shape[1:], req_map)])
    args = [q16] + [kt_pool] * g + [v_pool] * g + [kt_new, v_new]
    if variant == "dsa":
        in_specs.append(pl.BlockSpec((1, g, n_new, LANES),
                                     lambda bi, p, pt: (bi, jnp.minimum(p, n_steps - 1), 0, 0)))
        in_specs.append(pl.BlockSpec((1, n_new, LANES), req_map))
        args += [bias, bias_new]
    if variant == "fox":
        in_specs.append(pl.BlockSpec((1, rows, LANES), req_map))
        in_specs.append(pl.BlockSpec((1, n_units, g * LANES), lambda bi, p, pt: (bi, 0, jnp.minimum(p, n_steps - 1))))
        in_specs.append(pl.BlockSpec((1, n_units, LANES), req_map))
        args += [cq, ck, cn]
    if variant == "diff":
        in_specs.append(pl.BlockSpec(memory_space=pltpu.SMEM))
        args.append(lam)
    wo = v_new.shape[2] if variant == "diff" else w
    grid_spec = pltpu.PrefetchScalarGridSpec(
        num_scalar_prefetch=1,
        grid=(b, n_steps + 1),
        in_specs=in_specs,
        out_specs=pl.BlockSpec((1, n_new, wo), req_map),
        scratch_shapes=[pltpu.VMEM((rows, w), BF16), pltpu.VMEM((rows, LANES), F32),
                        pltpu.VMEM((rows, LANES), F32), pltpu.VMEM((rows, wa), F32)],
    )
    return pl.pallas_call(
        functools.partial(_attn_decode_kernel, variant=variant, g=g, n_units=n_units, n_new=n_new,
                          n_pages=n_pages),
        grid_spec=grid_spec,
        out_shape=jax.ShapeDtypeStruct((b, n_new, wo), F32),
        compiler_params=_cparams(2),
        name="attn_decode_" + variant,
    )(page_table, *args)


def _feature_major_pages(cache):
    n_pool, page = cache.shape[0], cache.shape[1]
    perm = (0,) + tuple(range(2, cache.ndim)) + (1,)
    return jnp.transpose(cache, perm).reshape(n_pool, -1, page)


def _feature_major_new(x3, page):
    xt = jnp.swapaxes(x3, 1, 2)
    return jnp.pad(xt, ((0, 0), (0, 0), (0, page - xt.shape[2])))


def _diff_lambda(p, lam_init):
    def e(a, c):
        return jnp.exp(jnp.sum(a.astype(F32) * c.astype(F32)))
    return (e(p["lam_q1"], p["lam_k1"]) - e(p["lam_q2"], p["lam_k2"]) + lam_init).reshape(1, 1).astype(F32)


def _pad_rows(x, rows):
    return jnp.pad(x, ((0, 0), (0, rows - x.shape[1]), (0, 0)))


def _tile_for(m, pref):
    t = min(m, pref)
    while m % t:
        t //= 2
    return t


def _token_major(xt, *tail):
    b, _, seq = xt.shape
    return jnp.swapaxes(xt, 1, 2).reshape((b, seq) + tail)


def _even_prompt(x, layer, p, tiles):
    b, seq, d = x.shape
    tm, tq, tk = tiles
    pos = jnp.tile(jnp.arange(seq), b)
    t = _proj0(x.reshape(b * seq, d), pos, p, tm, ("ka16", "ki16", "kb16", "vb", "ga", "gb"))
    lam_init = 0.8 - 0.6 * math.exp(-0.3 * layer)
    lam = _diff_lambda(p, lam_init)
    nsel = min(TOPK_MAX, seq // 4)
    r3 = lambda a: a.reshape(b, seq, a.shape[-1])
    offs = _even_splits()
    col = lambda i: p["w_in"][:, offs[i]:offs[i + 1]]
    log2_scale = QK_SCALE * LOG2E
    qat, kat, vat16, vat, qit, qbt, kbt, vbt16, kit, wit = _proj_t(
        x, p["norm"],
        [(col(0), 0, True, log2_scale, (BF16,)), (col(1), 1, True, 1.0, (F32,)),
         (col(2), None, False, 1.0, (BF16, F32)), (col(4), None, True, 1.0, (BF16,)),
         (col(7), 2, True, log2_scale, (BF16,)), (col(8), 3, True, 1.0, (F32,)),
         (col(9), None, False, 1.0, (BF16,))],
        [p["qn_a"], p["kn_a"], p["qn_b"], p["kn_b"]], pos, ("idx", col(5), col(6), p["kn_idx"]), tm)
    bias = _idx_prompt_t(qit, wit, r3(t["ki16"]), nsel, min(tq, IDX_Q_TILE), tq, tk)
    o_a = _flash_t("dsa", r3(t["ka16"]), qat, vat16, tq, tk, bias=bias)
    o_b = _flash_t("diff", r3(t["kb16"]), qbt, vbt16, tq, tk, lam=lam)
    y = _out0(x.reshape(b * seq, d), o_a.reshape(b * seq, -1), t["ga"], o_b.reshape(b * seq, -1), t["gb"], p,
              lam_init, tm)
    new = (_token_major(kat, N_HEADS_A, HEAD_DIM), _token_major(vat, N_HEADS_A, HEAD_DIM), _token_major(kit, IDX_DIM),
           _token_major(kbt, N_HEADS_B, 2, HEAD_DIM), t["vb"].reshape(b, seq, N_HEADS_B, 2 * HEAD_DIM))
    return y.reshape(b, seq, d), new


def _even_sample(x, layer, caches, page_table, p):
    cache_k_a, cache_v_a, cache_k_i, cache_k_b, cache_v_b = caches
    b, n_new, d = x.shape
    n_pages = page_table.shape[1]
    page = cache_k_a.shape[1]
    past = n_pages * page
    pos = jnp.tile(past + jnp.arange(n_new), b)
    m = b * n_new
    t = _proj0(x.reshape(m, d), pos, p, _tile_for(m, ROW_TILE),
               ("qa", "ka", "va", "ga", "qi", "ki", "wi", "qb", "kb", "vb", "gb"))
    lam_init = 0.8 - 0.6 * math.exp(-0.3 * layer)
    lam = _diff_lambda(p, lam_init)
    nsel = min(TOPK_MAX, (past + n_new) // 4)
    r3 = lambda a: a.reshape(b, n_new, a.shape[-1])
    q_st = jnp.swapaxes(t["qi"].reshape(b, n_new, N_IDX_HEADS, IDX_DIM), 1, 2).reshape(
        b, N_IDX_HEADS * n_new, IDX_DIM)
    w_st = jnp.swapaxes(t["wi"].reshape(b, n_new, N_IDX_HEADS), 1, 2).reshape(b, N_IDX_HEADS * n_new, 1)
    w_st = jnp.broadcast_to(w_st, (b, N_IDX_HEADS * n_new, LANES))
    bias, bias_new = _idx_decode(page_table, q_st, w_st, _feature_major_pages(cache_k_i),
                                 _feature_major_new(r3(t["ki"]), page), nsel)
    o_a = _attn_decode("dsa", page_table, r3(t["qa"]), _feature_major_pages(cache_k_a),
                       _feature_major_pages(cache_v_a), _feature_major_new(r3(t["ka"]), page),
                       _feature_major_new(r3(t["va"]), page), bias=bias, bias_new=bias_new)
    o_b = _attn_decode("diff", page_table, r3(t["qb"]), _feature_major_pages(cache_k_b), cache_v_b,
                       _feature_major_new(r3(t["kb"]), page), _pad_rows(r3(t["vb"]), page), lam=lam)
    y = _out0(x.reshape(m, d), o_a.reshape(m, -1), t["ga"], o_b.reshape(m, -1), t["gb"], p, lam_init,
              _tile_for(m, ROW_TILE))
    new = (t["ka"].reshape(b, n_new, N_HEADS_A, HEAD_DIM), t["va"].reshape(b, n_new, N_HEADS_A, HEAD_DIM),
           t["ki"].reshape(b, n_new, IDX_DIM), t["kb"].reshape(b, n_new, N_HEADS_B, 2, HEAD_DIM),
           t["vb"].reshape(b, n_new, N_HEADS_B, 2 * HEAD_DIM))
    return y.reshape(b, n_new, d), new


def _odd_prompt(x, p, tiles):
    b, seq, d = x.shape
    tm, tq, tk = tiles
    t = _proj1(x.reshape(b * seq, d), p, tm, ("k16", "gate"))
    wc = WIDTH_C
    w_in = p["w_in"]
    qt, kt, vt16, vt, logf_t = _proj_t(
        x, p["norm"],
        [(w_in[:, 0:wc], 0, False, QK_SCALE * LOG2E, (BF16,)), (w_in[:, wc:2 * wc], 1, False, 1.0, (F32,)),
         (w_in[:, 2 * wc:3 * wc], None, False, 1.0, (BF16, F32))],
        [p["qn"], p["kn"]], None, ("logf", w_in[:, 4 * wc:], p["b_f"]), tm)
    n_blk = seq // LANES
    pages = jnp.swapaxes(logf_t.reshape(b, N_HEADS_C, n_blk, LANES), 1, 2).reshape(b * n_blk, N_HEADS_C, LANES)
    ident = jnp.arange(b * n_blk, dtype=I32).reshape(b, n_blk)
    c_t, _ = _cumsum_pages(ident, pages, jnp.zeros((b, N_HEADS_C, LANES), F32))
    o = _flash_t("fox", t["k16"].reshape(b, seq, wc), qt, vt16, tq, tk, cqt=c_t, ck=jnp.swapaxes(c_t, 1, 2))
    y = _out1(x.reshape(b * seq, d), o.reshape(b * seq, -1), t["gate"], p, tm)
    new = (_token_major(kt, N_HEADS_C, HEAD_DIM), _token_major(vt, N_HEADS_C, HEAD_DIM), _token_major(logf_t, N_HEADS_C))
    return y.reshape(b, seq, d), new


def _odd_sample(x, caches, page_table, p):
    cache_k, cache_v, cache_logf = caches
    b, n_new, d = x.shape
    page = cache_k.shape[1]
    m = b * n_new
    t = _proj1(x.reshape(m, d), p, _tile_for(m, ROW_TILE), ("q16", "k", "v", "gate", "logf"))
    r3 = lambda a: a.reshape(b, n_new, a.shape[-1])
    c_past_t, c_new_t = _cumsum_pages(page_table, _feature_major_pages(cache_logf.astype(F32)),
                                      _feature_major_new(r3(t["logf"]), page))
    cq = jnp.broadcast_to(c_new_t[:, :, :n_new].reshape(b, N_HEADS_C * n_new, 1), (b, N_HEADS_C * n_new, LANES))
    o = _attn_decode("fox", page_table, r3(t["q16"]), _feature_major_pages(cache_k), _feature_major_pages(cache_v),
                     _feature_major_new(r3(t["k"]), page), _feature_major_new(r3(t["v"]), page),
                     cq=cq, ck=c_past_t, cn=c_new_t)
    y = _out1(x.reshape(m, d), o.reshape(m, -1), t["gate"], p, _tile_for(m, ROW_TILE))
    new = (t["k"].reshape(b, n_new, N_HEADS_C, HEAD_DIM), t["v"].reshape(b, n_new, N_HEADS_C, HEAD_DIM),
           t["logf"].reshape(b, n_new, N_HEADS_C))
    return y.reshape(b, n_new, d), new


def kernel(x_prompt, x_sample, cache_l0_k_a, cache_l0_v_a, cache_l0_k_idx, cache_l0_k_b, cache_l0_v_b,
           cache_l1_k_c, cache_l1_v_c, cache_l1_logf_c, page_table,
           l0_norm, l0_w_in, l0_qn_a, l0_kn_a, l0_kn_idx, l0_qn_b, l0_kn_b,
           l0_lam_q1, l0_lam_k1, l0_lam_q2, l0_lam_k2, l0_subln_b, l0_w_out,
           l1_norm, l1_w_in, l1_b_f, l1_qn, l1_kn, l1_w_out):
    p0 = dict(norm=l0_norm, w_in=l0_w_in, qn_a=l0_qn_a, kn_a=l0_kn_a, kn_idx=l0_kn_idx, qn_b=l0_qn_b, kn_b=l0_kn_b,
              lam_q1=l0_lam_q1, lam_k1=l0_lam_k1, lam_q2=l0_lam_q2, lam_k2=l0_lam_k2, subln_b=l0_subln_b,
              w_out=l0_w_out)
    p1 = dict(norm=l1_norm, w_in=l1_w_in, b_f=l1_b_f, qn=l1_qn, kn=l1_kn, w_out=l1_w_out)
    b, seq, _ = x_prompt.shape
    tiles = (_tile_for(b * seq, ROW_TILE), _tile_for(seq, Q_TILE), _tile_for(seq, K_TILE))
    page_table = page_table.astype(I32)
    xp, sp0 = _even_prompt(x_prompt, 0, p0, tiles)
    xs, ss0 = _even_sample(x_sample, 0, (cache_l0_k_a, cache_l0_v_a, cache_l0_k_idx, cache_l0_k_b, cache_l0_v_b),
                           page_table, p0)
    xp, sp1 = _odd_prompt(xp, p1, tiles)
    xs, ss1 = _odd_sample(xs, (cache_l1_k_c, cache_l1_v_c, cache_l1_logf_c), page_table, p1)
    (p_k_a, p_v_a, p_k_idx, p_k_b, p_v_b), (p_k_c, p_v_c, p_logf_c) = sp0, sp1
    (s_k_a, s_v_a, s_k_idx, s_k_b, s_v_b), (s_k_c, s_v_c, s_logf_c) = ss0, ss1
    return (xp, xs, p_k_a, s_k_a, p_v_a, s_v_a, p_k_idx, s_k_idx, p_k_b, s_k_b, p_v_b, s_v_b,
            p_k_c, s_k_c, p_v_c, s_v_c, p_logf_c, s_logf_c)
```

```python
import functools
import math

import jax
import jax.numpy as jnp
from jax import lax
from jax.experimental import pallas as pl
from jax.experimental.pallas import tpu as pltpu

F32 = jnp.float32
BF16 = jnp.bfloat16
I32 = jnp.int32

HEAD_DIM = 64
ROPE_THETA = 500000.0
N_HEADS_A = 8
N_IDX_HEADS = 8
IDX_DIM = 64
TOPK_MAX = 256
N_HEADS_B = 4
N_HEADS_C = 16
EPS = 1e-6
WIDTH_A = N_HEADS_A * HEAD_DIM
WIDTH_B = N_HEADS_B * 2 * HEAD_DIM
WIDTH_C = N_HEADS_C * HEAD_DIM
WIDTH_I = N_IDX_HEADS * IDX_DIM
QK_SCALE = HEAD_DIM ** -0.5
LOG2E = 1.4426950408889634

LANES = 128
SUBLANES = 8
MXU_DIM = 256
VMEM_LIMIT = 56 * 1024 * 1024
ROW_TILE = 256
Q_TILE = 512
IDX_Q_TILE = 256
K_TILE = 512
COUNT_ROWS = 64
COUNT_PARTS = 8
FLASH_SUB = 512
FLASH_GROUP = 8

NEG = -1e30
NT_DIMS = (((1,), (1,)), ((), ()))

KEY_NEG_INF = -2139095041


def _cparams(n_axes):
    return pltpu.CompilerParams(
        dimension_semantics=("arbitrary",) * n_axes, vmem_limit_bytes=VMEM_LIMIT)


def _div_pow2(x, d):
    assert d & (d - 1) == 0
    return lax.shift_right_logical(x, jnp.int32(d.bit_length() - 1))


def _sort_key(x):
    bits = pltpu.bitcast(x, I32)
    return bits ^ ((bits >> 31) & jnp.int32(0x7FFFFFFF))


def _rms_rows(x, g):
    ms = jnp.mean(x * x, axis=-1, keepdims=True)
    return x * lax.rsqrt(ms + EPS) * g


def _split2(x):
    hi = x.astype(BF16)
    return hi, (x - hi.astype(F32)).astype(BF16)


def _split3(x):
    hi = x.astype(BF16)
    r = x - hi.astype(F32)
    mid = r.astype(BF16)
    return hi, mid, (r - mid.astype(F32)).astype(BF16)


def _head_norm(h, bd, gain):
    w = h.shape[-1]
    hi, lo = _split2(h * h)
    bd16 = bd.astype(BF16)
    if w >= MXU_DIM:
        cols = [jnp.dot(hi[:, c * MXU_DIM:(c + 1) * MXU_DIM], bd16, preferred_element_type=F32)
                + jnp.dot(lo[:, c * MXU_DIM:(c + 1) * MXU_DIM], bd16, preferred_element_type=F32)
                for c in range(w // MXU_DIM)]
        ms = cols[0] if len(cols) == 1 else jnp.concatenate(cols, axis=-1)
    else:
        ms = (jnp.dot(hi, bd16[:w, :w], preferred_element_type=F32)
              + jnp.dot(lo, bd16[:w, :w], preferred_element_type=F32))
    return h * lax.rsqrt(ms + EPS) * gain


def _rope(y, rope_ref):
    c = rope_ref[:, 0:LANES]
    s_lo = rope_ref[:, LANES:2 * LANES]
    s_hi = rope_ref[:, 2 * LANES:3 * LANES]
    outs = []
    for j in range(y.shape[-1] // LANES):
        yc = y[:, j * LANES:(j + 1) * LANES]
        outs.append(yc * c + pltpu.roll(yc, LANES - 8, 1) * s_lo + pltpu.roll(yc, 8, 1) * s_hi)
    return outs[0] if len(outs) == 1 else jnp.concatenate(outs, axis=-1)


def _proj0_kernel(*refs, want):
    x_ref, g_ref, w_ref, ws_ref, rope_ref, gains_ref, gki_ref, bd_ref = refs[:8]
    o = dict(zip(want, refs[8:]))
    xb = _rms_rows(x_ref[...], g_ref[...]).astype(BF16)
    bd = bd_ref[...]
    w512 = WIDTH_A

    def piece(j):
        return jnp.dot(xb, w_ref[:, j * w512:(j + 1) * w512], preferred_element_type=F32)

    def put(name, val):
        if name in o:
            o[name][...] = val.astype(o[name].dtype)

    def needs(*names):
        return any(n in o for n in names)

    if needs("qa"):
        put("qa", _rope(_head_norm(piece(0), bd, gains_ref[0:1, :]), rope_ref) * QK_SCALE)
    if needs("ka", "ka16"):
        k_a = _rope(_head_norm(piece(1), bd, gains_ref[1:2, :]), rope_ref)
        put("ka", k_a)
        put("ka16", k_a)
    if needs("va"):
        put("va", piece(2))
    if needs("ga"):
        put("ga", piece(3))
    if needs("qi"):
        put("qi", _rope(piece(4), rope_ref))
    if needs("qb"):
        put("qb", _rope(_head_norm(piece(5), bd, gains_ref[2:3, :]), rope_ref) * QK_SCALE)
    if needs("kb", "kb16"):
        k_b = _rope(_head_norm(piece(6), bd, gains_ref[3:4, :]), rope_ref)
        put("kb", k_b)
        put("kb16", k_b)
    if needs("vb"):
        put("vb", piece(7))
    if needs("gb"):
        put("gb", piece(8))
    if needs("ki", "ki16", "wi"):
        hs = jnp.dot(xb, ws_ref[...], preferred_element_type=F32)
        k_i = _rope(_head_norm(hs[:, 0:LANES], bd, gki_ref[...]), rope_ref)
        put("ki", k_i[:, 0:IDX_DIM])
        put("ki16", k_i)
        put("wi", hs[:, LANES:LANES + N_IDX_HEADS] * (WIDTH_I ** -0.5))


def _proj1_kernel(*refs, want):
    x_ref, g_ref, w_ref, wf_ref, bf_ref, gains_ref, bd_ref = refs[:7]
    o = dict(zip(want, refs[7:]))
    xb = _rms_rows(x_ref[...], g_ref[...]).astype(BF16)
    bd = bd_ref[...]
    wc = WIDTH_C

    def piece(j):
        return jnp.dot(xb, w_ref[:, j * wc:(j + 1) * wc], preferred_element_type=F32)

    def put(name, val):
        if name in o:
            o[name][...] = val.astype(o[name].dtype)

    if "q16" in o:
        put("q16", _head_norm(piece(0), bd, gains_ref[0:1, :]) * QK_SCALE)
    if "k" in o or "k16" in o:
        k = _head_norm(piece(1), bd, gains_ref[1:2, :])
        put("k", k)
        put("k16", k)
    if "v" in o:
        put("v", piece(2))
    if "gate" in o:
        put("gate", piece(3))
    if "logf" in o:
        f = jnp.dot(xb, wf_ref[...], preferred_element_type=F32)[:, 0:N_HEADS_C] + bf_ref[...]
        put("logf", _log_sigmoid(f))


def _row_spec(tm, w):
    return pl.BlockSpec((tm, w), lambda i: (i, 0))


def _const_spec(shape):
    return pl.BlockSpec(shape, lambda i: (0,) * len(shape))


def _block_diag_mean():
    r = lax.broadcasted_iota(I32, (MXU_DIM, MXU_DIM), 0) // HEAD_DIM
    c = lax.broadcasted_iota(I32, (MXU_DIM, MXU_DIM), 1) // HEAD_DIM
    return jnp.where(r == c, 1.0 / HEAD_DIM, 0.0).astype(F32)


def _rope_table(pos):
    rot = HEAD_DIM // 4
    half = rot // 2
    inv = jnp.power(F32(ROPE_THETA), -jnp.arange(half, dtype=F32) * (2.0 / rot))
    ang = pos.astype(F32)[:, None] * inv[None, :]
    cos, sin = jnp.cos(ang), jnp.sin(ang)
    m = pos.shape[0]
    ones = jnp.ones((m, HEAD_DIM - rot), F32)
    zeros = jnp.zeros((m, HEAD_DIM - rot), F32)
    zh = jnp.zeros((m, half), F32)
    c = jnp.concatenate([cos, cos, ones], axis=-1)
    s_lo = jnp.concatenate([-sin, zh, zeros], axis=-1)
    s_hi = jnp.concatenate([zh, sin, zeros], axis=-1)
    return jnp.concatenate([c, c, s_lo, s_lo, s_hi, s_hi], axis=-1)


def _tile_gain(g, w):
    return jnp.tile(g.astype(F32), w // g.shape[0])[None, :]


P0_OUTS = dict(qa=(WIDTH_A, BF16), ka=(WIDTH_A, F32), ka16=(WIDTH_A, BF16), va=(WIDTH_A, F32), ga=(WIDTH_A, F32),
               qi=(WIDTH_I, BF16), ki=(IDX_DIM, F32), ki16=(LANES, BF16), wi=(N_IDX_HEADS, F32),
               qb=(WIDTH_B, BF16), kb=(WIDTH_B, F32), kb16=(WIDTH_B, BF16), vb=(WIDTH_B, F32), gb=(WIDTH_B, F32))
P1_OUTS = dict(q16=(WIDTH_C, BF16), k=(WIDTH_C, F32), k16=(WIDTH_C, BF16), v=(WIDTH_C, F32), gate=(WIDTH_C, F32),
               logf=(N_HEADS_C, F32))


def _even_splits():
    sizes = (WIDTH_A,) * 4 + (WIDTH_I, IDX_DIM, N_IDX_HEADS) + (WIDTH_B,) * 4
    offs = [0]
    for s in sizes:
        offs.append(offs[-1] + s)
    return offs


def _proj0(x2d, pos, p, tm, want):
    m, d = x2d.shape
    offs = _even_splits()
    w_in = p["w_in"]
    cols = [w_in[:, offs[i]:offs[i + 1]] for i in range(len(offs) - 1)]
    w_big = jnp.concatenate([cols[0], cols[1], cols[2], cols[3], cols[4], cols[7], cols[8], cols[9], cols[10]],
                            axis=1).astype(BF16)
    w_small = jnp.concatenate(
        [cols[5], cols[5], cols[6], jnp.zeros((d, LANES - N_IDX_HEADS), w_in.dtype)], axis=1).astype(BF16)
    gains = jnp.concatenate([_tile_gain(p["qn_a"], WIDTH_A), _tile_gain(p["kn_a"], WIDTH_A),
                             _tile_gain(p["qn_b"], WIDTH_B), _tile_gain(p["kn_b"], WIDTH_B)], axis=0)
    gki = _tile_gain(p["kn_idx"], LANES)
    rope = _rope_table(pos)
    out_shape = tuple(jax.ShapeDtypeStruct((m, P0_OUTS[n][0]), P0_OUTS[n][1]) for n in want)
    out_specs = tuple(_row_spec(tm, s.shape[1]) for s in out_shape)
    outs = pl.pallas_call(
        functools.partial(_proj0_kernel, want=tuple(want)),
        grid=(m // tm,),
        in_specs=[_row_spec(tm, d), _const_spec((1, d)), _const_spec(w_big.shape), _const_spec(w_small.shape),
                  _row_spec(tm, 3 * LANES), _const_spec(gains.shape), _const_spec(gki.shape),
                  _const_spec((MXU_DIM, MXU_DIM))],
        out_specs=out_specs,
        out_shape=out_shape,
        compiler_params=_cparams(1),
        name="proj0",
    )(x2d, p["norm"].astype(F32)[None, :], w_big, w_small, rope, gains, gki, _block_diag_mean())
    return dict(zip(want, outs))


def _proj1(x2d, p, tm, want):
    m, d = x2d.shape
    wc = WIDTH_C
    w_in = p["w_in"]
    w_big = w_in[:, :4 * wc].astype(BF16)
    w_f = jnp.concatenate([w_in[:, 4 * wc:], jnp.zeros((d, LANES - N_HEADS_C), w_in.dtype)], axis=1).astype(BF16)
    gains = jnp.concatenate([_tile_gain(p["qn"], wc), _tile_gain(p["kn"], wc)], axis=0)
    out_shape = tuple(jax.ShapeDtypeStruct((m, P1_OUTS[n][0]), P1_OUTS[n][1]) for n in want)
    out_specs = tuple(_row_spec(tm, s.shape[1]) for s in out_shape)
    outs = pl.pallas_call(
        functools.partial(_proj1_kernel, want=tuple(want)),
        grid=(m // tm,),
        in_specs=[_row_spec(tm, d), _const_spec((1, d)), _const_spec(w_big.shape), _const_spec(w_f.shape),
                  _const_spec((1, N_HEADS_C)), _const_spec(gains.shape), _const_spec((MXU_DIM, MXU_DIM))],
        out_specs=out_specs,
        out_shape=out_shape,
        compiler_params=_cparams(1),
        name="proj1",
    )(x2d, p["norm"].astype(F32)[None, :], w_big, w_f, p["b_f"].astype(F32)[None, :], gains, _block_diag_mean())
    return dict(zip(want, outs))


def _log_sigmoid(f):
    return jnp.minimum(f, 0.0) - jnp.log1p(jnp.exp(-jnp.abs(f)))


def _head_norm_t(h, bd, gain):
    w, tm = h.shape
    hi, lo = _split2(h * h)
    bd16 = bd.astype(BF16)
    if w >= MXU_DIM:
        slabs = [jnp.dot(bd16, hi[c * MXU_DIM:(c + 1) * MXU_DIM, :], preferred_element_type=F32)
                 + jnp.dot(bd16, lo[c * MXU_DIM:(c + 1) * MXU_DIM, :], preferred_element_type=F32)
                 for c in range(w // MXU_DIM)]
        ms = slabs[0] if len(slabs) == 1 else jnp.concatenate(slabs, axis=0)
    else:
        ms = (jnp.dot(bd16[:w, :w], hi, preferred_element_type=F32)
              + jnp.dot(bd16[:w, :w], lo, preferred_element_type=F32))
    return h * lax.rsqrt(ms + EPS) * jnp.tile(gain, (1, tm // LANES))


def _rope_t(y, rope_ref):
    half = HEAD_DIM // 8
    cos = rope_ref[0:half, :]
    sin = rope_ref[half:2 * half, :]
    parts = []
    for h in range(y.shape[0] // HEAD_DIM):
        r0 = h * HEAD_DIM
        x1 = y[r0:r0 + half, :]
        x2 = y[r0 + half:r0 + 2 * half, :]
        parts += [x1 * cos - x2 * sin, x2 * cos + x1 * sin, y[r0 + 2 * half:r0 + HEAD_DIM, :]]
    return jnp.concatenate(parts, axis=0)


def _proj_t_kernel(*refs, specs, pw, small):
    x_ref, g_ref, wt_ref, ws_ref, rope_ref, gains_ref, gs_ref, bd_ref = refs[:8]
    outs = list(refs[8:])
    xb = _rms_rows(x_ref[...], g_ref[...]).astype(BF16)
    bd = bd_ref[...]
    tm = xb.shape[0]
    k = 0
    for j, (gain_row, rope, scale, kinds) in enumerate(specs):
        h = lax.dot_general(wt_ref[j * pw:(j + 1) * pw, :], xb, NT_DIMS, preferred_element_type=F32)
        if gain_row is not None:
            h = _head_norm_t(h, bd, gains_ref[gain_row])
        if rope:
            h = _rope_t(h, rope_ref)
        if scale != 1.0:
            h = h * scale
        for dt in kinds:
            outs[k][0] = h.astype(dt)
            k += 1
    if small == "idx":
        hs = lax.dot_general(ws_ref[...], xb, NT_DIMS, preferred_element_type=F32)
        outs[k][0] = _rope_t(_head_norm_t(hs[0:IDX_DIM, :], bd, gs_ref[...]), rope_ref)
        outs[k + 1][0] = hs[IDX_DIM:IDX_DIM + N_IDX_HEADS, :] * (WIDTH_I ** -0.5)
    if small == "logf":
        f = lax.dot_general(ws_ref[...], xb, NT_DIMS, preferred_element_type=F32)
        outs[k][0] = _log_sigmoid(f + jnp.tile(gs_ref[...], (1, tm // LANES)))


def _rope_table_t(pos):
    rot = HEAD_DIM // 4
    half = rot // 2
    inv = jnp.power(F32(ROPE_THETA), -jnp.arange(half, dtype=F32) * (2.0 / rot))
    ang = inv[:, None] * pos.astype(F32)[None, :]
    return jnp.concatenate([jnp.cos(ang), jnp.sin(ang)], axis=0)


def _lane_replicated(col):
    return jnp.broadcast_to(col.astype(F32)[:, None], (col.shape[0], LANES))


def _proj_t(x3, norm, blocks, gains, pos, small, tm):
    b, seq, d = x3.shape
    m = b * seq
    nt = seq // tm
    pw = blocks[0][0].shape[1]
    wt = jnp.concatenate([blk[0].T for blk in blocks], axis=0).astype(BF16)
    specs = tuple(tuple(blk[1:]) for blk in blocks)
    gains_arr = (jnp.zeros((1, pw, LANES), F32) if not gains else
                 jnp.stack([_lane_replicated(jnp.tile(g, pw // g.shape[0])) for g in gains]))
    rope = jnp.zeros((2 * SUBLANES, m), F32) if pos is None else _rope_table_t(pos)
    out_rows = [(pw, dt) for blk in blocks for dt in blk[4]]
    kind = None
    ws = jnp.zeros((SUBLANES, d), BF16)
    gs = jnp.zeros((SUBLANES, LANES), F32)
    if small is not None:
        kind = small[0]
        if kind == "idx":
            ws = jnp.concatenate([small[1].T, small[2].T], axis=0).astype(BF16)
            gs = _lane_replicated(small[3])
            out_rows += [(IDX_DIM, F32), (N_IDX_HEADS, F32)]
        else:
            ws = small[1].T.astype(BF16)
            gs = _lane_replicated(small[2])
            out_rows += [(N_HEADS_C, F32)]
    out_shape = [jax.ShapeDtypeStruct((b, r, seq), dt) for r, dt in out_rows]
    out_specs = [pl.BlockSpec((1, r, tm), lambda i: (i // nt, 0, i % nt)) for r, _ in out_rows]
    return pl.pallas_call(
        functools.partial(_proj_t_kernel, specs=specs, pw=pw, small=kind),
        grid=(m // tm,),
        in_specs=[_row_spec(tm, d), _const_spec((1, d)), _const_spec(wt.shape), _const_spec(ws.shape),
                  pl.BlockSpec((2 * SUBLANES, tm), lambda i: (0, i)), _const_spec(gains_arr.shape),
                  _const_spec(gs.shape), _const_spec((MXU_DIM, MXU_DIM))],
        out_specs=out_specs,
        out_shape=out_shape,
        compiler_params=_cparams(1),
        name="proj_t",
    )(x3.reshape(m, d), norm.astype(F32)[None, :], wt, ws, rope, gains_arr, gs, _block_diag_mean())


def _silu(g):
    return g * jax.nn.sigmoid(g)


def _out0_kernel(x_ref, oa_ref, ga_ref, ob_ref, gb_ref, sub_ref, w_ref, y_ref, *, post_scale):
    ya = (oa_ref[...] * _silu(ga_ref[...])).astype(BF16)
    ob = ob_ref[...]
    cols = []
    for h in range(N_HEADS_B):
        oc = ob[:, h * LANES:(h + 1) * LANES]
        ms = jnp.mean(oc * oc, axis=-1, keepdims=True)
        cols.append(oc * lax.rsqrt(ms + EPS))
    obn = jnp.concatenate(cols, axis=-1) * sub_ref[...] * post_scale
    yb = (obn * _silu(gb_ref[...])).astype(BF16)
    y = jnp.dot(ya, w_ref[0:WIDTH_A, :], preferred_element_type=F32)
    y = y + jnp.dot(yb, w_ref[WIDTH_A:WIDTH_A + WIDTH_B, :], preferred_element_type=F32)
    y_ref[...] = x_ref[...] + y


def _out1_kernel(x_ref, o_ref, g_ref, w_ref, y_ref):
    yo = (o_ref[...] * _silu(g_ref[...])).astype(BF16)
    y_ref[...] = x_ref[...] + jnp.dot(yo, w_ref[...], preferred_element_type=F32)


def _out0(x2d, o_a, g_a, o_b, g_b, p, lam_init, tm):
    m, d = x2d.shape
    sub = _tile_gain(p["subln_b"], WIDTH_B)
    w = p["w_out"].astype(BF16)
    return pl.pallas_call(
        functools.partial(_out0_kernel, post_scale=1.0 - lam_init),
        grid=(m // tm,),
        in_specs=[_row_spec(tm, d), _row_spec(tm, WIDTH_A), _row_spec(tm, WIDTH_A), _row_spec(tm, WIDTH_B),
                  _row_spec(tm, WIDTH_B), _const_spec(sub.shape), _const_spec(w.shape)],
        out_specs=_row_spec(tm, d),
        out_shape=jax.ShapeDtypeStruct((m, d), F32),
        compiler_params=_cparams(1),
        name="out0",
    )(x2d, o_a, g_a, o_b, g_b, sub, w)


def _out1(x2d, o, g, p, tm):
    m, d = x2d.shape
    w = p["w_out"].astype(BF16)
    return pl.pallas_call(
        _out1_kernel,
        grid=(m // tm,),
        in_specs=[_row_spec(tm, d), _row_spec(tm, WIDTH_C), _row_spec(tm, WIDTH_C), _const_spec(w.shape)],
        out_specs=_row_spec(tm, d),
        out_shape=jax.ShapeDtypeStruct((m, d), F32),
        compiler_params=_cparams(1),
        name="out1",
    )(x2d, o, g, w)


def _select_topk(load_keys, n_chunks, groups, rows, nsel, idx_bits, idx_of):
    shape = (rows, LANES)
    rb = min(rows, COUNT_ROWS)
    n_blk = rows // rb

    def count(pred_of):
        accs = []
        for b in range(n_blk):
            pred = pred_of(slice(b * rb, (b + 1) * rb))

            def body(c, acc, b=b, pred=pred):
                blk = load_keys(c, b * rb, rb)
                for g in range(groups):
                    acc = jnp.where(pred(blk[:, g * LANES:(g + 1) * LANES], c, g), acc + 1.0, acc)
                return acc
            accs.append(lax.fori_loop(0, n_chunks, body, jnp.zeros((rb, LANES), F32),
                                      unroll=isinstance(n_chunks, int)))
        outs = [jnp.broadcast_to(jnp.sum(a, axis=-1, keepdims=True), (rb, LANES)) for a in accs]
        return outs[0] if n_blk == 1 else jnp.concatenate(outs, axis=0)

    kf = float(nsel)

    def bit_step(i, thr):
        bit = lax.shift_left(jnp.int32(1), jnp.int32(31) - i)
        cand = thr ^ bit

        def pred_of(sl):
            cb = cand[sl]
            return lambda k, c, g: k >= cb
        return jnp.where(count(pred_of) >= kf, cand, thr)

    thr = lax.fori_loop(0, 32, bit_step, jnp.full(shape, jnp.iinfo(jnp.int32).min, I32))
    thr = jnp.maximum(thr, KEY_NEG_INF + 1)

    def gt_of(sl):
        tb = thr[sl]
        return lambda k, c, g: k > tb

    def ge_of(sl):
        tb = thr[sl]
        return lambda k, c, g: k >= tb

    n_gt = count(gt_of)
    n_ge = count(ge_of)
    need = kf - n_gt
    excess = n_ge - kf

    def tie_cut():
        def idx_step(i, cut):
            bit = lax.shift_left(jnp.int32(1), jnp.int32(idx_bits - 1) - i)
            cand = cut | bit

            def pred_of(sl):
                tb, cb = thr[sl], cand[sl]
                return lambda k, c, g: (k == tb) & (idx_of(c, g, rb) < cb)
            return jnp.where(count(pred_of) < need, cand, cut)
        return lax.fori_loop(0, idx_bits, idx_step, jnp.zeros(shape, I32))

    big = jnp.full(shape, jnp.iinfo(jnp.int32).max, I32)
    any_excess = jnp.max(excess) > 0.0
    cut = lax.cond(any_excess, lambda: jnp.where(excess > 0.0, tie_cut(), big), lambda: big)
    return thr, cut


def _select_topk_t(load_keys, n_chunks, rpc, cols, nsel, idx_bits):
    shape = (SUBLANES, cols)
    reps = rpc // SUBLANES
    n_part = math.gcd(reps, COUNT_PARTS)

    def count(pred):
        def body(c, acc):
            x = jnp.where(pred(load_keys(c).reshape(reps, SUBLANES, cols), c), 1.0, 0.0)
            part = jnp.sum(x.reshape(reps // n_part, n_part, SUBLANES, cols), axis=0)
            return acc + jnp.sum(part, axis=0)
        acc = lax.fori_loop(0, n_chunks, body, jnp.zeros(shape, F32))
        return jnp.broadcast_to(jnp.sum(acc, axis=0, keepdims=True), shape)

    kf = float(nsel)

    def bit_step(i, state):
        thr, n_thr = state
        bit = lax.shift_left(jnp.int32(1), jnp.int32(31) - i)
        cand = thr ^ bit
        n_cand = count(lambda k, c: k >= cand[None])
        take = n_cand >= kf
        return jnp.where(take, cand, thr), jnp.where(take, n_cand, n_thr)

    thr, n_thr = lax.fori_loop(
        0, 32, bit_step,
        (jnp.full(shape, jnp.iinfo(jnp.int32).min, I32), jnp.full(shape, 2.0 ** 30, F32)))
    excess = jnp.where(thr > KEY_NEG_INF, n_thr - kf, -1.0)
    thr = jnp.maximum(thr, KEY_NEG_INF + 1)
    row = lax.broadcasted_iota(I32, (rpc, cols), 0).reshape(reps, SUBLANES, cols)

    def tie_cut():
        need = kf - count(lambda k, c: k > thr[None])

        def idx_step(i, cut):
            bit = lax.shift_left(jnp.int32(1), jnp.int32(idx_bits - 1) - i)
            cand = cut | bit
            n_lt = count(lambda k, c: (k == thr[None]) & (row + c * rpc < cand[None]))
            return jnp.where(n_lt < need, cand, cut)
        return lax.fori_loop(0, idx_bits, idx_step, jnp.zeros(shape, I32))

    big = jnp.full(shape, jnp.iinfo(jnp.int32).max, I32)
    any_excess = jnp.max(excess) > 0.0
    cut = lax.cond(any_excess, lambda: jnp.where(excess > 0.0, tie_cut(), big), lambda: big)
    return thr, cut


def _masked_pairs_t(qt_ref, qm_ref, n_pairs, tq):
    top = lax.broadcasted_iota(I32, (LANES, tq), 0) < HEAD_DIM
    for j in range(n_pairs):
        pair = qt_ref[0, j * LANES:(j + 1) * LANES, :]
        zero = jnp.zeros_like(pair)
        qm_ref[2 * j] = jnp.where(top, pair, zero)
        qm_ref[2 * j + 1] = jnp.where(top, zero, pair)


def _idx_prompt_t_kernel(qit_ref, wit_ref, ki_ref, bias_ref, qm_ref, s_ref, *, tq, tk, nsel, seq):
    qi = pl.program_id(1)
    q0 = qi * tq
    n_chunks = seq // tk
    nc = (q0 + tq + tk - 1) // tk
    _masked_pairs_t(qit_ref, qm_ref, N_IDX_HEADS // 2, tq)
    wt = wit_ref[0]
    col = q0 + lax.broadcasted_iota(I32, (tk, tq), 1)
    row0 = lax.broadcasted_iota(I32, (tk, tq), 0)

    def score_body(c, carry):
        start = pl.multiple_of(c * tk, tk)
        kblk = ki_ref[0, pl.ds(start, tk), :]
        acc = jnp.zeros((tk, tq), F32)
        for h in range(N_IDX_HEADS):
            s = jnp.dot(kblk, qm_ref[h], preferred_element_type=F32)
            acc = acc + jnp.maximum(s, 0.0) * wt[h:h + 1, :]
        acc = jnp.where(row0 + c * tk <= col, acc, -jnp.inf)
        s_ref[c] = _sort_key(acc)
        return carry

    lax.fori_loop(0, nc, score_body, 0)
    thr, cut = _select_topk_t(lambda c: s_ref[c], nc, tk, tq, nsel, int(math.log2(seq)) + 1)
    thr_t = jnp.tile(thr, (tk // SUBLANES, 1))
    cut_t = jnp.tile(cut, (tk // SUBLANES, 1))

    def write_body(c, carry):
        k = s_ref[c]
        sel = (k > thr_t) | ((k == thr_t) & (row0 + c * tk <= cut_t))
        start = pl.multiple_of(c * tk, tk)
        bias_ref[0, 0, pl.ds(start, tk), :] = jnp.where(sel, 0.0, NEG).astype(BF16)
        return carry

    lax.fori_loop(0, nc, write_body, 0)

    def fill_body(c, carry):
        start = pl.multiple_of(c * tk, tk)
        bias_ref[0, 0, pl.ds(start, tk), :] = jnp.full((tk, tq), NEG, BF16)
        return carry

    lax.fori_loop(nc, n_chunks, fill_body, 0)


def _idx_prompt_t(qit, wit, ki16, nsel, tq, tq_out, tk):
    b, seq, _ = ki16.shape
    nq = seq // tq
    per = tq_out // tq
    return pl.pallas_call(
        functools.partial(_idx_prompt_t_kernel, tq=tq, tk=tk, nsel=nsel, seq=seq),
        grid=(b, nq),
        in_specs=[pl.BlockSpec((1, WIDTH_I, tq), lambda bi, qi: (bi, 0, qi)),
                  pl.BlockSpec((1, N_IDX_HEADS, tq), lambda bi, qi: (bi, 0, qi)),
                  pl.BlockSpec((1, seq, LANES), lambda bi, qi: (bi, 0, 0))],
        out_specs=pl.BlockSpec((1, 1, seq, tq), lambda bi, qi: (bi, qi // per, 0, qi % per)),
        out_shape=jax.ShapeDtypeStruct((b, seq // tq_out, seq, tq_out), BF16),
        scratch_shapes=[pltpu.VMEM((N_IDX_HEADS, LANES, tq), BF16),
                        pltpu.VMEM((seq // tk, tk, tq), I32)],
        compiler_params=_cparams(2),
        name="idx_prompt",
    )(qit, wit, ki16)


def _flash_t_kernel(*refs, variant, n_pairs, tq, tk, sub):
    it = iter(refs)
    k_ref, qt_ref, vt_ref = next(it), next(it), next(it)
    bias_ref = next(it) if variant == "dsa" else None
    cqt_ref, ck_ref = (next(it), next(it)) if variant == "fox" else (None, None)
    lam_ref = next(it) if variant == "diff" else None
    o_ref = next(it)
    qm_ref, m_ref, l_ref, acc_ref = next(it), next(it), next(it), next(it)
    bias32_ref = next(it) if variant == "dsa" else None

    qi = pl.program_id(1)
    kc = pl.program_id(2)
    nk = pl.num_programs(2)
    last = ((qi + 1) * tq - 1) // tk
    n_units = 2 * n_pairs
    vr = acc_ref.shape[1]

    @pl.when(kc == 0)
    def _init():
        _masked_pairs_t(qt_ref, qm_ref, n_pairs, tq)
        m_ref[...] = jnp.full(m_ref.shape, NEG, F32)
        l_ref[...] = jnp.zeros(l_ref.shape, F32)
        acc_ref[...] = jnp.zeros(acc_ref.shape, F32)

    def compute(masked):
        if variant == "dsa":
            bias32_ref[...] = bias_ref[0, 0].astype(F32)
            masked = False
        if masked:
            col = qi * tq + lax.broadcasted_iota(I32, (sub, tq), 1)
            row0 = kc * tk + lax.broadcasted_iota(I32, (sub, tq), 0)
        ones_rows = jnp.ones((2 * SUBLANES, sub), BF16)
        if variant == "fox":
            ck_all = ck_ref[0] * LOG2E
        for g0 in range(0, n_units, FLASH_GROUP):
            units = list(range(g0, min(g0 + FLASH_GROUP, n_units)))
            state = [[m_ref[u], l_ref[u], acc_ref[u]] for u in units]
            for r0 in range(0, tk, sub):
                ss = [jnp.dot(k_ref[0, r0:r0 + sub, (u // 2) * LANES:(u // 2 + 1) * LANES], qm_ref[u],
                              preferred_element_type=F32) for u in units]
                ps = []
                for e, u in enumerate(units):
                    s = ss[e]
                    if variant == "dsa":
                        s = s + bias32_ref[r0:r0 + sub, :]
                    cq_row = 0.0
                    if variant == "fox":
                        s = s - jnp.broadcast_to(ck_all[r0:r0 + sub, u:u + 1], (sub, tq))
                        cq_row = cqt_ref[0, u:u + 1, :] * LOG2E
                    if masked:
                        s = jnp.where(row0 + r0 <= col, s, NEG)
                    m_prev, l_prev, acc = state[e]
                    m_new = jnp.maximum(m_prev, jnp.max(s, axis=0, keepdims=True) + cq_row)
                    alpha = jnp.exp2(m_prev - m_new)
                    p = jnp.exp2(s - jnp.tile(m_new - cq_row, (sub // SUBLANES, 1)))
                    state[e][0] = m_new
                    state[e][1] = alpha * l_prev
                    state[e][2] = jnp.tile(alpha, (vr // SUBLANES, 1)) * acc
                    ps.append(p.astype(BF16))
                for e, u in enumerate(units):
                    v0 = (u // 2) * vr if variant == "diff" else u * vr
                    v_aug = jnp.concatenate([vt_ref[0, v0:v0 + vr, r0:r0 + sub], ones_rows], axis=0)
                    pv = jnp.dot(v_aug, ps[e], preferred_element_type=F32)
                    state[e][1] = state[e][1] + pv[vr:vr + SUBLANES, :]
                    state[e][2] = state[e][2] + pv[0:vr, :]
            for e, u in enumerate(units):
                m_ref[u], l_ref[u], acc_ref[u] = state[e]

    needs_mask = (kc + 1) * tk - 1 > qi * tq

    @pl.when((kc <= last) & needs_mask)
    def _diag():
        compute(True)

    @pl.when((kc <= last) & jnp.logical_not(needs_mask))
    def _full():
        compute(False)

    @pl.when(kc == nk - 1)
    def _fin():
        def norm(u):
            return acc_ref[u] / jnp.tile(l_ref[u], (vr // SUBLANES, 1))
        for j in range(n_pairs):
            if variant == "diff":
                ot = norm(2 * j) - lam_ref[0, 0] * norm(2 * j + 1)
            else:
                ot = jnp.concatenate([norm(2 * j), norm(2 * j + 1)], axis=0)
            o_ref[0, :, j * LANES:(j + 1) * LANES] = ot.T


def _flash_t(variant, k16, qt, vt, tq, tk, bias=None, cqt=None, ck=None, lam=None):
    b, seq, w = k16.shape
    n_pairs = w // LANES
    n_units = 2 * n_pairs
    nq, nk = seq // tq, seq // tk
    vr = LANES if variant == "diff" else HEAD_DIM
    sub = min(tk, FLASH_SUB)

    def kc_of(qi, kc):
        return jnp.minimum(kc, ((qi + 1) * tq - 1) // tk)

    in_specs = [pl.BlockSpec((1, tk, w), lambda bi, qi, kc: (bi, kc_of(qi, kc), 0)),
                pl.BlockSpec((1, w, tq), lambda bi, qi, kc: (bi, 0, qi)),
                pl.BlockSpec((1, vt.shape[1], tk), lambda bi, qi, kc: (bi, 0, kc_of(qi, kc)))]
    args = [k16, qt, vt]
    scratch = [pltpu.VMEM((n_units, LANES, tq), BF16), pltpu.VMEM((n_units, SUBLANES, tq), F32),
               pltpu.VMEM((n_units, SUBLANES, tq), F32), pltpu.VMEM((n_units, vr, tq), F32)]
    if variant == "dsa":
        in_specs.append(pl.BlockSpec((1, 1, tk, tq), lambda bi, qi, kc: (bi, qi, kc_of(qi, kc), 0)))
        args.append(bias)
        scratch.append(pltpu.VMEM((tk, tq), F32))
    if variant == "fox":
        in_specs.append(pl.BlockSpec((1, n_units, tq), lambda bi, qi, kc: (bi, 0, qi)))
        in_specs.append(pl.BlockSpec((1, tk, n_units), lambda bi, qi, kc: (bi, kc_of(qi, kc), 0)))
        args += [cqt, ck]
    if variant == "diff":
        in_specs.append(pl.BlockSpec(memory_space=pltpu.SMEM))
        args.append(lam)
    return pl.pallas_call(
        functools.partial(_flash_t_kernel, variant=variant, n_pairs=n_pairs, tq=tq, tk=tk, sub=sub),
        grid=(b, nq, nk),
        in_specs=in_specs,
        out_specs=pl.BlockSpec((1, tq, w), lambda bi, qi, kc: (bi, qi, 0)),
        out_shape=jax.ShapeDtypeStruct((b, seq, w), F32),
        scratch_shapes=scratch,
        compiler_params=_cparams(3),
        name="flash_" + variant,
    )(*args)


PAGES_PER_STEP = 8
SMALL_PAGES_PER_STEP = 64


def _pages_per_step(n_pages, pref=PAGES_PER_STEP):
    g = pref
    while n_pages % g:
        g //= 2
    return g


def _upper_ones():
    r = lax.broadcasted_iota(I32, (LANES, LANES), 0)
    c = lax.broadcasted_iota(I32, (LANES, LANES), 1)
    return jnp.where(r <= c, 1.0, 0.0).astype(BF16)


def _cumsum_kernel(*refs, g):
    x_refs = refs[1:1 + g]
    xn_ref, c_ref, cn_ref, carry_ref = refs[1 + g:5 + g]
    p = pl.program_id(1)
    upper = _upper_ones()

    @pl.when(p == 0)
    def _():
        carry_ref[...] = jnp.zeros(carry_ref.shape, F32)

    def page_cumsum(x):
        return sum(jnp.dot(t, upper, preferred_element_type=F32) for t in _split3(x))

    local = [page_cumsum(x_refs[j][0]) for j in range(g)]
    total = carry_ref[...]
    for j in range(g):
        c_ref[0, :, j * LANES:(j + 1) * LANES] = local[j] + total
        total = total + jnp.broadcast_to(local[j][:, LANES - 1:LANES], total.shape)
    carry_ref[...] = total

    @pl.when(p == pl.num_programs(1) - 1)
    def _():
        cn_ref[0] = page_cumsum(xn_ref[0]) + total


def _cumsum_pages(page_table, pool_t, new_t):
    b, n_pages = page_table.shape
    h = pool_t.shape[1]
    g = _pages_per_step(n_pages, SMALL_PAGES_PER_STEP)
    pool_specs = [pl.BlockSpec((1, h, LANES), lambda bi, p, pt, j=j: (pt[bi, p * g + j], 0, 0)) for j in range(g)]
    grid_spec = pltpu.PrefetchScalarGridSpec(
        num_scalar_prefetch=1,
        grid=(b, n_pages // g),
        in_specs=pool_specs + [pl.BlockSpec((1, h, LANES), lambda bi, p, pt: (bi, 0, 0))],
        out_specs=[pl.BlockSpec((1, h, g * LANES), lambda bi, p, pt: (bi, 0, p)),
                   pl.BlockSpec((1, h, LANES), lambda bi, p, pt: (bi, 0, 0))],
        scratch_shapes=[pltpu.VMEM((h, LANES), F32)],
    )
    return pl.pallas_call(
        functools.partial(_cumsum_kernel, g=g),
        grid_spec=grid_spec,
        out_shape=(jax.ShapeDtypeStruct((b, h, n_pages * LANES), F32), jax.ShapeDtypeStruct((b, h, LANES), F32)),
        compiler_params=_cparams(2),
        name="cumsum_pages",
    )(page_table, *([pool_t] * g), new_t)


def _idx_decode_kernel(*refs, g, n_pages, n_new, nsel, page):
    q_ref, w_ref = refs[1:3]
    k_refs = refs[3:3 + g]
    knew_ref, bias_ref, biasn_ref, s_ref = refs[3 + g:7 + g]
    p = pl.program_id(1)
    n_steps = n_pages // g
    rows = n_new

    def scores(kt):
        n = kt.shape[1] // LANES
        s = jnp.dot(q_ref[0], kt.astype(BF16), preferred_element_type=F32)
        t = jnp.maximum(s, 0.0) * jnp.tile(w_ref[0], (1, n))
        acc = t[0:rows]
        for h in range(1, N_IDX_HEADS):
            acc = acc + t[h * rows:(h + 1) * rows]
        return acc

    @pl.when(p < n_steps)
    def _past():
        keys = _sort_key(scores(jnp.concatenate([k_refs[j][0] for j in range(g)], axis=1)))
        for j in range(g):
            s_ref[p * g + j] = keys[:, j * LANES:(j + 1) * LANES]

    @pl.when(p == n_steps)
    def _new():
        sc = scores(knew_ref[0])
        i = lax.broadcasted_iota(I32, (rows, LANES), 0)
        lane = lax.broadcasted_iota(I32, (rows, LANES), 1)
        s_ref[n_pages] = _sort_key(jnp.where(lane <= i, sc, -jnp.inf))

        def idx_of(c, grp, nr):
            return lane + c * page

        n_keys = (n_pages + 1) * page
        thr, cut = _select_topk(lambda c, r0, nr: s_ref[c], n_pages + 1, 1, rows, nsel,
                                int(math.log2(n_keys)) + 1, idx_of)

        def selected(c):
            k = s_ref[c]
            sel = (k > thr) | ((k == thr) & (lane + c * page <= cut))
            return jnp.where(sel, 0.0, NEG)

        def write_body(c, carry):
            bias_ref[0, c] = selected(c)
            return carry

        lax.fori_loop(0, n_pages, write_body, 0)
        biasn_ref[0] = selected(n_pages)


def _idx_decode(page_table, q_st, w_st, kt_pool, kt_new, nsel):
    b, n_pages = page_table.shape
    page = kt_pool.shape[2]
    n_new = q_st.shape[1] // N_IDX_HEADS
    g = _pages_per_step(n_pages, SMALL_PAGES_PER_STEP)
    n_steps = n_pages // g

    def req_map(bi, p, pt):
        return (bi, 0, 0)

    pool_specs = [pl.BlockSpec((1, IDX_DIM, page),
                               lambda bi, p, pt, j=j: (pt[bi, jnp.minimum(p, n_steps - 1) * g + j], 0, 0))
                  for j in range(g)]
    grid_spec = pltpu.PrefetchScalarGridSpec(
        num_scalar_prefetch=1,
        grid=(b, n_steps + 1),
        in_specs=[pl.BlockSpec((1,) + q_st.shape[1:], req_map), pl.BlockSpec((1,) + w_st.shape[1:], req_map)]
        + pool_specs + [pl.BlockSpec((1, IDX_DIM, page), req_map)],
        out_specs=[pl.BlockSpec((1, n_pages, n_new, LANES), lambda bi, p, pt: (bi, 0, 0, 0)),
                   pl.BlockSpec((1, n_new, LANES), req_map)],
        scratch_shapes=[pltpu.VMEM((n_pages + 1, n_new, LANES), I32)],
    )
    return pl.pallas_call(
        functools.partial(_idx_decode_kernel, g=g, n_pages=n_pages, n_new=n_new, nsel=nsel, page=page),
        grid_spec=grid_spec,
        out_shape=(jax.ShapeDtypeStruct((b, n_pages, n_new, LANES), F32),
                   jax.ShapeDtypeStruct((b, n_new, LANES), F32)),
        compiler_params=_cparams(2),
        name="idx_decode",
    )(page_table, q_st, w_st, *([kt_pool] * g), kt_new)


def _attn_decode_kernel(*refs, variant, g, n_units, n_new, n_pages):
    it = iter(refs)
    next(it)
    q_ref = next(it)
    k_refs = [next(it) for _ in range(g)]
    v_refs = [next(it) for _ in range(g)]
    knew_ref, vnew_ref = next(it), next(it)
    bias_ref, biasn_ref = (next(it), next(it)) if variant == "dsa" else (None, None)
    cq_ref, ck_ref, cn_ref = (next(it), next(it), next(it)) if variant == "fox" else (None, None, None)
    lam_ref = next(it) if variant == "diff" else None
    o_ref = next(it)
    qbd_ref, m_ref, l_ref, acc_ref = next(it), next(it), next(it), next(it)

    p = pl.program_id(1)
    n_steps = n_pages // g
    rows = n_units * n_new
    w = q_ref.shape[-1]
    wa = acc_ref.shape[-1]

    @pl.when(p == 0)
    def _init():
        q = q_ref[0].astype(F32)
        qt = jnp.concatenate([q] * n_units, axis=0)
        r = _div_pow2(lax.broadcasted_iota(I32, (rows, w), 0), n_new)
        c = _div_pow2(lax.broadcasted_iota(I32, (rows, w), 1), HEAD_DIM)
        qbd_ref[...] = jnp.where(r == c, qt, 0.0).astype(BF16)
        m_ref[...] = jnp.full(m_ref.shape, NEG, F32)
        l_ref[...] = jnp.zeros(l_ref.shape, F32)
        acc_ref[...] = jnp.zeros(acc_ref.shape, F32)

    def expand_rows(x8):
        return jnp.concatenate([x8] * n_units, axis=0)

    def expand_units(xu):
        return jnp.concatenate([jnp.broadcast_to(xu[u:u + 1, :], (n_new, xu.shape[1])) for u in range(n_units)],
                               axis=0)

    def v_head(v_ref, h, is_new):
        if is_new or len(v_ref.shape) == 3:
            return v_ref[0, :, h * LANES:(h + 1) * LANES]
        return v_ref[0, :, h, :]

    def step(kts, vs, bias, ck, is_new):
        ng = len(kts)
        kt_all = kts[0][0] if ng == 1 else jnp.concatenate([kts[j][0] for j in range(ng)], axis=1)
        s = jnp.dot(qbd_ref[...], kt_all.astype(BF16), preferred_element_type=F32)
        if variant == "dsa":
            s = s + bias
        if variant == "fox":
            s = s + (jnp.tile(cq_ref[0], (1, ng)) - expand_units(ck))
        if is_new:
            i = expand_rows(lax.broadcasted_iota(I32, (n_new, LANES), 0))
            lane = lax.broadcasted_iota(I32, (rows, LANES), 1)
            s = jnp.where(lane <= i, s, NEG)
        m_prev = m_ref[...]
        m_new = jnp.maximum(m_prev, jnp.max(s, axis=-1, keepdims=True))
        alpha = jnp.exp(m_prev - m_new)
        pr = jnp.exp(s - jnp.tile(m_new, (1, ng)))
        l_ref[...] = alpha * l_ref[...] + jnp.sum(pr, axis=-1, keepdims=True)
        m_ref[...] = m_new
        p16 = pr.astype(BF16)
        if variant == "diff":
            hr = 2 * n_new
            pv = jnp.concatenate(
                [jnp.dot(p16[h * hr:(h + 1) * hr],
                         jnp.concatenate([v_head(vs[j], h, is_new) for j in range(ng)], axis=0).astype(BF16),
                         preferred_element_type=F32) for h in range(n_units // 2)], axis=0)
        else:
            vt_all = vs[0][0] if ng == 1 else jnp.concatenate([vs[j][0] for j in range(ng)], axis=1)
            pv = lax.dot_general(p16, vt_all.astype(BF16), NT_DIMS, preferred_element_type=F32)
        acc_ref[...] = jnp.tile(alpha, (1, wa // LANES)) * acc_ref[...] + pv

    @pl.when(p < n_steps)
    def _past():
        bias = ck = None
        if variant == "dsa":
            bias = jnp.concatenate([expand_rows(bias_ref[0, j]) for j in range(g)], axis=1)
        if variant == "fox":
            ck = ck_ref[0]
        step(k_refs, v_refs, bias, ck, False)

    @pl.when(p == n_steps)
    def _new():
        bias = expand_rows(biasn_ref[0]) if variant == "dsa" else None
        ck = cn_ref[0] if variant == "fox" else None
        step([knew_ref], [vnew_ref], bias, ck, True)
        accn = acc_ref[...] / jnp.tile(l_ref[...], (1, wa // LANES))
        if variant == "diff":
            lam = lam_ref[0, 0]
            o_ref[0] = jnp.concatenate(
                [accn[(2 * h) * n_new:(2 * h + 1) * n_new] - lam * accn[(2 * h + 1) * n_new:(2 * h + 2) * n_new]
                 for h in range(n_units // 2)], axis=1)
        else:
            cu = _div_pow2(lax.broadcasted_iota(I32, (n_new, wa), 1), HEAD_DIM)
            out = jnp.zeros((n_new, wa), F32)
            for u in range(n_units):
                out = out + jnp.where(cu == u, accn[u * n_new:(u + 1) * n_new], 0.0)
            o_ref[0] = out


def _attn_decode(variant, page_table, q16, kt_pool, v_pool, kt_new, v_new,
                  bias=None, bias_new=None, cq=None, ck=None, cn=None, lam=None):
    b, n_pages = page_table.shape
    page = kt_pool.shape[2]
    n_new, w = q16.shape[1], q16.shape[2]
    n_units = w // HEAD_DIM
    rows = n_units * n_new
    g = _pages_per_step(n_pages)
    n_steps = n_pages // g
    wa = LANES if variant == "diff" else w

    def req_map(bi, p, pt):
        return (bi, 0, 0)

    def pool_spec(arr, j):
        nd = arr.ndim
        return pl.BlockSpec((1,) + arr.shape[1:],
                            lambda bi, p, pt: (pt[bi, jnp.minimum(p, n_steps - 1) * g + j],) + (0,) * (nd - 1))

    in_specs = ([pl.BlockSpec((1, n_new, w), req_map)]
                + [pool_spec(kt_pool, j) for j in range(g)] + [pool_spec(v_pool, j) for j in range(g)]
                + [pl.BlockSpec((1,) + kt_new.shape[1:], req_map), pl.BlockSpec((1,) + v_new.shape[1:], req_map)])
    args = [q16] + [kt_pool] * g + [v_pool] * g + [kt_new, v_new]
    if variant == "dsa":
        in_specs.append(pl.BlockSpec((1, g, n_new, LANES),
                                     lambda bi, p, pt: (bi, jnp.minimum(p, n_steps - 1), 0, 0)))
        in_specs.append(pl.BlockSpec((1, n_new, LANES), req_map))
        args += [bias, bias_new]
    if variant == "fox":
        in_specs.append(pl.BlockSpec((1, rows, LANES), req_map))
        in_specs.append(pl.BlockSpec((1, n_units, g * LANES), lambda bi, p, pt: (bi, 0, jnp.minimum(p, n_steps - 1))))
        in_specs.append(pl.BlockSpec((1, n_units, LANES), req_map))
        args += [cq, ck, cn]
    if variant == "diff":
        in_specs.append(pl.BlockSpec(memory_space=pltpu.SMEM))
        args.append(lam)
    wo = v_new.shape[2] if variant == "diff" else w
    grid_spec = pltpu.PrefetchScalarGridSpec(
        num_scalar_prefetch=1,
        grid=(b, n_steps + 1),
        in_specs=in_specs,
        out_specs=pl.BlockSpec((1, n_new, wo), req_map),
        scratch_shapes=[pltpu.VMEM((rows, w), BF16), pltpu.VMEM((rows, LANES), F32),
                        pltpu.VMEM((rows, LANES), F32), pltpu.VMEM((rows, wa), F32)],
    )
    return pl.pallas_call(
        functools.partial(_attn_decode_kernel, variant=variant, g=g, n_units=n_units, n_new=n_new,
                          n_pages=n_pages),
        grid_spec=grid_spec,
        out_shape=jax.ShapeDtypeStruct((b, n_new, wo), F32),
        compiler_params=_cparams(2),
        name="attn_decode_" + variant,
    )(page_table, *args)


def _feature_major_pages(cache):
    n_pool, page = cache.shape[0], cache.shape[1]
    perm = (0,) + tuple(range(2, cache.ndim)) + (1,)
    return jnp.transpose(cache, perm).reshape(n_pool, -1, page)


def _feature_major_new(x3, page):
    xt = jnp.swapaxes(x3, 1, 2)
    return jnp.pad(xt, ((0, 0), (0, 0), (0, page - xt.shape[2])))


def _diff_lambda(p, lam_init):
    def e(a, c):
        return jnp.exp(jnp.sum(a.astype(F32) * c.astype(F32)))
    return (e(p["lam_q1"], p["lam_k1"]) - e(p["lam_q2"], p["lam_k2"]) + lam_init).reshape(1, 1).astype(F32)


def _pad_rows(x, rows):
    return jnp.pad(x, ((0, 0), (0, rows - x.shape[1]), (0, 0)))


def _tile_for(m, pref):
    t = min(m, pref)
    while m % t:
        t //= 2
    return t


def _token_major(xt, *tail):
    b, _, seq = xt.shape
    return jnp.swapaxes(xt, 1, 2).reshape((b, seq) + tail)


def _even_prompt(x, layer, p, tiles):
    b, seq, d = x.shape
    tm, tq, tk = tiles
    pos = jnp.tile(jnp.arange(seq), b)
    t = _proj0(x.reshape(b * seq, d), pos, p, tm, ("ka16", "ki16", "kb16", "vb", "ga", "gb"))
    lam_init = 0.8 - 0.6 * math.exp(-0.3 * layer)
    lam = _diff_lambda(p, lam_init)
    nsel = min(TOPK_MAX, seq // 4)
    r3 = lambda a: a.reshape(b, seq, a.shape[-1])
    offs = _even_splits()
    col = lambda i: p["w_in"][:, offs[i]:offs[i + 1]]
    log2_scale = QK_SCALE * LOG2E
    qat, kat, vat16, vat, qit, qbt, kbt, vbt16, kit, wit = _proj_t(
        x, p["norm"],
        [(col(0), 0, True, log2_scale, (BF16,)), (col(1), 1, True, 1.0, (F32,)),
         (col(2), None, False, 1.0, (BF16, F32)), (col(4), None, True, 1.0, (BF16,)),
         (col(7), 2, True, log2_scale, (BF16,)), (col(8), 3, True, 1.0, (F32,)),
         (col(9), None, False, 1.0, (BF16,))],
        [p["qn_a"], p["kn_a"], p["qn_b"], p["kn_b"]], pos, ("idx", col(5), col(6), p["kn_idx"]), tm)
    bias = _idx_prompt_t(qit, wit, r3(t["ki16"]), nsel, min(tq, IDX_Q_TILE), tq, tk)
    o_a = _flash_t("dsa", r3(t["ka16"]), qat, vat16, tq, tk, bias=bias)
    o_b = _flash_t("diff", r3(t["kb16"]), qbt, vbt16, tq, tk, lam=lam)
    y = _out0(x.reshape(b * seq, d), o_a.reshape(b * seq, -1), t["ga"], o_b.reshape(b * seq, -1), t["gb"], p,
              lam_init, tm)
    new = (_token_major(kat, N_HEADS_A, HEAD_DIM), _token_major(vat, N_HEADS_A, HEAD_DIM), _token_major(kit, IDX_DIM),
           _token_major(kbt, N_HEADS_B, 2, HEAD_DIM), t["vb"].reshape(b, seq, N_HEADS_B, 2 * HEAD_DIM))
    return y.reshape(b, seq, d), new


def _even_sample(x, layer, caches, page_table, p):
    cache_k_a, cache_v_a, cache_k_i, cache_k_b, cache_v_b = caches
    b, n_new, d = x.shape
    n_pages = page_table.shape[1]
    page = cache_k_a.shape[1]
    past = n_pages * page
    pos = jnp.tile(past + jnp.arange(n_new), b)
    m = b * n_new
    t = _proj0(x.reshape(m, d), pos, p, _tile_for(m, ROW_TILE),
               ("qa", "ka", "va", "ga", "qi", "ki", "wi", "qb", "kb", "vb", "gb"))
    lam_init = 0.8 - 0.6 * math.exp(-0.3 * layer)
    lam = _diff_lambda(p, lam_init)
    nsel = min(TOPK_MAX, (past + n_new) // 4)
    r3 = lambda a: a.reshape(b, n_new, a.shape[-1])
    q_st = jnp.swapaxes(t["qi"].reshape(b, n_new, N_IDX_HEADS, IDX_DIM), 1, 2).reshape(
        b, N_IDX_HEADS * n_new, IDX_DIM)
    w_st = jnp.swapaxes(t["wi"].reshape(b, n_new, N_IDX_HEADS), 1, 2).reshape(b, N_IDX_HEADS * n_new, 1)
    w_st = jnp.broadcast_to(w_st, (b, N_IDX_HEADS * n_new, LANES))
    bias, bias_new = _idx_decode(page_table, q_st, w_st, _feature_major_pages(cache_k_i),
                                 _feature_major_new(r3(t["ki"]), page), nsel)
    o_a = _attn_decode("dsa", page_table, r3(t["qa"]), _feature_major_pages(cache_k_a),
                       _feature_major_pages(cache_v_a), _feature_major_new(r3(t["ka"]), page),
                       _feature_major_new(r3(t["va"]), page), bias=bias, bias_new=bias_new)
    o_b = _attn_decode("diff", page_table, r3(t["qb"]), _feature_major_pages(cache_k_b), cache_v_b,
                       _feature_major_new(r3(t["kb"]), page), _pad_rows(r3(t["vb"]), page), lam=lam)
    y = _out0(x.reshape(m, d), o_a.reshape(m, -1), t["ga"], o_b.reshape(m, -1), t["gb"], p, lam_init,
              _tile_for(m, ROW_TILE))
    new = (t["ka"].reshape(b, n_new, N_HEADS_A, HEAD_DIM), t["va"].reshape(b, n_new, N_HEADS_A, HEAD_DIM),
           t["ki"].reshape(b, n_new, IDX_DIM), t["kb"].reshape(b, n_new, N_HEADS_B, 2, HEAD_DIM),
           t["vb"].reshape(b, n_new, N_HEADS_B, 2 * HEAD_DIM))
    return y.reshape(b, n_new, d), new


def _odd_prompt(x, p, tiles):
    b, seq, d = x.shape
    tm, tq, tk = tiles
    t = _proj1(x.reshape(b * seq, d), p, tm, ("k16", "gate"))
    wc = WIDTH_C
    w_in = p["w_in"]
    qt, kt, vt16, vt, logf_t = _proj_t(
        x, p["norm"],
        [(w_in[:, 0:wc], 0, False, QK_SCALE * LOG2E, (BF16,)), (w_in[:, wc:2 * wc], 1, False, 1.0, (F32,)),
         (w_in[:, 2 * wc:3 * wc], None, False, 1.0, (BF16, F32))],
        [p["qn"], p["kn"]], None, ("logf", w_in[:, 4 * wc:], p["b_f"]), tm)
    n_blk = seq // LANES
    pages = jnp.swapaxes(logf_t.reshape(b, N_HEADS_C, n_blk, LANES), 1, 2).reshape(b * n_blk, N_HEADS_C, LANES)
    ident = jnp.arange(b * n_blk, dtype=I32).reshape(b, n_blk)
    c_t, _ = _cumsum_pages(ident, pages, jnp.zeros((b, N_HEADS_C, LANES), F32))
    o = _flash_t("fox", t["k16"].reshape(b, seq, wc), qt, vt16, tq, tk, cqt=c_t, ck=jnp.swapaxes(c_t, 1, 2))
    y = _out1(x.reshape(b * seq, d), o.reshape(b * seq, -1), t["gate"], p, tm)
    new = (_token_major(kt, N_HEADS_C, HEAD_DIM), _token_major(vt, N_HEADS_C, HEAD_DIM), _token_major(logf_t, N_HEADS_C))
    return y.reshape(b, seq, d), new


def _odd_sample(x, caches, page_table, p):
    cache_k, cache_v, cache_logf = caches
    b, n_new, d = x.shape
    page = cache_k.shape[1]
    m = b * n_new
    t = _proj1(x.reshape(m, d), p, _tile_for(m, ROW_TILE), ("q16", "k", "v", "gate", "logf"))
    r3 = lambda a: a.reshape(b, n_new, a.shape[-1])
    c_past_t, c_new_t = _cumsum_pages(page_table, _feature_major_pages(cache_logf.astype(F32)),
                                      _feature_major_new(r3(t["logf"]), page))
    cq = jnp.broadcast_to(c_new_t[:, :, :n_new].reshape(b, N_HEADS_C * n_new, 1), (b, N_HEADS_C * n_new, LANES))
    o = _attn_decode("fox", page_table, r3(t["q16"]), _feature_major_pages(cache_k), _feature_major_pages(cache_v),
                     _feature_major_new(r3(t["k"]), page), _feature_major_new(r3(t["v"]), page),
                     cq=cq, ck=c_past_t, cn=c_new_t)
    y = _out1(x.reshape(m, d), o.reshape(m, -1), t["gate"], p, _tile_for(m, ROW_TILE))
    new = (t["k"].reshape(b, n_new, N_HEADS_C, HEAD_DIM), t["v"].reshape(b, n_new, N_HEADS_C, HEAD_DIM),
           t["logf"].reshape(b, n_new, N_HEADS_C))
    return y.reshape(b, n_new, d), new


def kernel(x_prompt, x_sample, cache_l0_k_a, cache_l0_v_a, cache_l0_k_idx, cache_l0_k_b, cache_l0_v_b,
           cache_l1_k_c, cache_l1_v_c, cache_l1_logf_c, page_table,
           l0_norm, l0_w_in, l0_qn_a, l0_kn_a, l0_kn_idx, l0_qn_b, l0_kn_b,
           l0_lam_q1, l0_lam_k1, l0_lam_q2, l0_lam_k2, l0_subln_b, l0_w_out,
           l1_norm, l1_w_in, l1_b_f, l1_qn, l1_kn, l1_w_out):
    p0 = dict(norm=l0_norm, w_in=l0_w_in, qn_a=l0_qn_a, kn_a=l0_kn_a, kn_idx=l0_kn_idx, qn_b=l0_qn_b, kn_b=l0_kn_b,
              lam_q1=l0_lam_q1, lam_k1=l0_lam_k1, lam_q2=l0_lam_q2, lam_k2=l0_lam_k2, subln_b=l0_subln_b,
              w_out=l0_w_out)
    p1 = dict(norm=l1_norm, w_in=l1_w_in, b_f=l1_b_f, qn=l1_qn, kn=l1_kn, w_out=l1_w_out)
    b, seq, _ = x_prompt.shape
    tiles = (_tile_for(b * seq, ROW_TILE), _tile_for(seq, Q_TILE), _tile_for(seq, K_TILE))
    page_table = page_table.astype(I32)
    xp, sp0 = _even_prompt(x_prompt, 0, p0, tiles)
    xs, ss0 = _even_sample(x_sample, 0, (cache_l0_k_a, cache_l0_v_a, cache_l0_k_idx, cache_l0_k_b, cache_l0_v_b),
                           page_table, p0)
    xp, sp1 = _odd_prompt(xp, p1, tiles)
    xs, ss1 = _odd_sample(xs, (cache_l1_k_c, cache_l1_v_c, cache_l1_logf_c), page_table, p1)
    (p_k_a, p_v_a, p_k_idx, p_k_b, p_v_b), (p_k_c, p_v_c, p_logf_c) = sp0, sp1
    (s_k_a, s_v_a, s_k_idx, s_k_b, s_v_b), (s_k_c, s_v_c, s_logf_c) = ss0, ss1
    return (xp, xs, p_k_a, s_k_a, p_v_a, s_v_a, p_k_idx, s_k_idx, p_k_b, s_k_b, p_v_b, s_v_b,
            p_k_c, s_k_c, p_v_c, s_v_c, p_logf_c, s_logf_c)
```

```python
import functools
import math

import jax
import jax.numpy as jnp
from jax import lax
from jax.experimental import pallas as pl
from jax.experimental.pallas import tpu as pltpu

F32 = jnp.float32
BF16 = jnp.bfloat16
I32 = jnp.int32

HEAD_DIM = 64
ROPE_THETA = 500000.0
N_HEADS_A = 8
N_IDX_HEADS = 8
IDX_DIM = 64
TOPK_MAX = 256
N_HEADS_B = 4
N_HEADS_C = 16
EPS = 1e-6
WIDTH_A = N_HEADS_A * HEAD_DIM
WIDTH_B = N_HEADS_B * 2 * HEAD_DIM
WIDTH_C = N_HEADS_C * HEAD_DIM
WIDTH_I = N_IDX_HEADS * IDX_DIM
QK_SCALE = HEAD_DIM ** -0.5
LOG2E = 1.4426950408889634

LANES = 128
SUBLANES = 8
MXU_DIM = 256
VMEM_LIMIT = 56 * 1024 * 1024
ROW_TILE = 256
Q_TILE = 512
IDX_Q_TILE = 256
K_TILE = 512
COUNT_ROWS = 64
COUNT_PARTS = 8
FLASH_SUB = 512
FLASH_GROUP = 8

NEG = -1e30
NT_DIMS = (((1,), (1,)), ((), ()))

KEY_NEG_INF = -2139095041


def _cparams(n_axes):
    return pltpu.CompilerParams(
        dimension_semantics=("arbitrary",) * n_axes, vmem_limit_bytes=VMEM_LIMIT)


def _div_pow2(x, d):
    assert d & (d - 1) == 0
    return lax.shift_right_logical(x, jnp.int32(d.bit_length() - 1))


def _sort_key(x):
    bits = pltpu.bitcast(x, I32)
    return bits ^ ((bits >> 31) & jnp.int32(0x7FFFFFFF))


def _rms_rows(x, g):
    ms = jnp.mean(x * x, axis=-1, keepdims=True)
    return x * lax.rsqrt(ms + EPS) * g


def _split2(x):
    hi = x.astype(BF16)
    return hi, (x - hi.astype(F32)).astype(BF16)


def _split3(x):
    hi = x.astype(BF16)
    r = x - hi.astype(F32)
    mid = r.astype(BF16)
    return hi, mid, (r - mid.astype(F32)).astype(BF16)


def _head_norm(h, bd, gain):
    w = h.shape[-1]
    hi, lo = _split2(h * h)
    bd16 = bd.astype(BF16)
    if w >= MXU_DIM:
        cols = [jnp.dot(hi[:, c * MXU_DIM:(c + 1) * MXU_DIM], bd16, preferred_element_type=F32)
                + jnp.dot(lo[:, c * MXU_DIM:(c + 1) * MXU_DIM], bd16, preferred_element_type=F32)
                for c in range(w // MXU_DIM)]
        ms = cols[0] if len(cols) == 1 else jnp.concatenate(cols, axis=-1)
    else:
        ms = (jnp.dot(hi, bd16[:w, :w], preferred_element_type=F32)
              + jnp.dot(lo, bd16[:w, :w], preferred_element_type=F32))
    return h * lax.rsqrt(ms + EPS) * gain


def _rope(y, rope_ref):
    c = rope_ref[:, 0:LANES]
    s_lo = rope_ref[:, LANES:2 * LANES]
    s_hi = rope_ref[:, 2 * LANES:3 * LANES]
    outs = []
    for j in range(y.shape[-1] // LANES):
        yc = y[:, j * LANES:(j + 1) * LANES]
        outs.append(yc * c + pltpu.roll(yc, LANES - 8, 1) * s_lo + pltpu.roll(yc, 8, 1) * s_hi)
    return outs[0] if len(outs) == 1 else jnp.concatenate(outs, axis=-1)


def _proj0_kernel(*refs, want):
    x_ref, g_ref, w_ref, ws_ref, rope_ref, gains_ref, gki_ref, bd_ref = refs[:8]
    o = dict(zip(want, refs[8:]))
    xb = _rms_rows(x_ref[...], g_ref[...]).astype(BF16)
    bd = bd_ref[...]
    rope_ref = _rope_coeffs(rope_ref[...])
    w512 = WIDTH_A

    def piece(j):
        return jnp.dot(xb, w_ref[:, j * w512:(j + 1) * w512], preferred_element_type=F32)

    def put(name, val):
        if name in o:
            o[name][...] = val.astype(o[name].dtype)

    def needs(*names):
        return any(n in o for n in names)

    if needs("qa"):
        put("qa", _rope(_head_norm(piece(0), bd, gains_ref[0:1, :]), rope_ref) * QK_SCALE)
    if needs("ka", "ka16"):
        k_a = _rope(_head_norm(piece(1), bd, gains_ref[1:2, :]), rope_ref)
        put("ka", k_a)
        put("ka16", k_a)
    if needs("va"):
        put("va", piece(2))
    if needs("ga"):
        put("ga", piece(3))
    if needs("qi"):
        put("qi", _rope(piece(4), rope_ref))
    if needs("qb"):
        put("qb", _rope(_head_norm(piece(5), bd, gains_ref[2:3, :]), rope_ref) * QK_SCALE)
    if needs("kb", "kb16"):
        k_b = _rope(_head_norm(piece(6), bd, gains_ref[3:4, :]), rope_ref)
        put("kb", k_b)
        put("kb16", k_b)
    if needs("vb"):
        put("vb", piece(7))
    if needs("gb"):
        put("gb", piece(8))
    if needs("ki", "ki16", "wi"):
        hs = jnp.dot(xb, ws_ref[...], preferred_element_type=F32)
        k_i = _rope(_head_norm(hs[:, 0:LANES], bd, gki_ref[...]), rope_ref)
        put("ki", k_i[:, 0:IDX_DIM])
        put("ki16", k_i)
        put("wi", hs[:, LANES:LANES + N_IDX_HEADS] * (WIDTH_I ** -0.5))


def _proj1_kernel(*refs, want):
    x_ref, g_ref, w_ref, wf_ref, bf_ref, gains_ref, bd_ref = refs[:7]
    o = dict(zip(want, refs[7:]))
    xb = _rms_rows(x_ref[...], g_ref[...]).astype(BF16)
    bd = bd_ref[...]
    wc = WIDTH_C

    def piece(j):
        return jnp.dot(xb, w_ref[:, j * wc:(j + 1) * wc], preferred_element_type=F32)

    def put(name, val):
        if name in o:
            o[name][...] = val.astype(o[name].dtype)

    if "q16" in o:
        put("q16", _head_norm(piece(0), bd, gains_ref[0:1, :]) * QK_SCALE)
    if "k" in o or "k16" in o:
        k = _head_norm(piece(1), bd, gains_ref[1:2, :])
        put("k", k)
        put("k16", k)
    if "v" in o:
        put("v", piece(2))
    if "gate" in o:
        put("gate", piece(3))
    if "logf" in o:
        f = jnp.dot(xb, wf_ref[...], preferred_element_type=F32)[:, 0:N_HEADS_C] + bf_ref[...]
        put("logf", _log_sigmoid(f))


def _row_spec(tm, w):
    return pl.BlockSpec((tm, w), lambda i: (i, 0))


def _const_spec(shape):
    return pl.BlockSpec(shape, lambda i: (0,) * len(shape))


def _block_diag_mean():
    r = lax.broadcasted_iota(I32, (MXU_DIM, MXU_DIM), 0) // HEAD_DIM
    c = lax.broadcasted_iota(I32, (MXU_DIM, MXU_DIM), 1) // HEAD_DIM
    return jnp.where(r == c, 1.0 / HEAD_DIM, 0.0).astype(F32)


def _rope_table(pos):
    rot = HEAD_DIM // 4
    half = rot // 2
    inv = jnp.power(F32(ROPE_THETA), -jnp.arange(half, dtype=F32) * (2.0 / rot))
    ang = pos.astype(F32)[:, None] * inv[None, :]
    return jnp.concatenate([jnp.cos(ang), jnp.sin(ang)], axis=-1)


def _rope_coeffs(cs):
    half = HEAD_DIM // 8
    r = lax.broadcasted_iota(I32, (2 * half, 3 * LANES), 0)
    lane = lax.broadcasted_iota(I32, (2 * half, 3 * LANES), 1)
    d = lane & (HEAD_DIM - 1)
    grp = _div_pow2(lane, LANES)
    place = jnp.where((grp == 0) & (d < 2 * half) & (r == (d & (half - 1))), 1.0, 0.0)
    place = place + jnp.where((grp == 1) & (d < half) & (r == d + half), -1.0, 0.0)
    place = place + jnp.where((grp == 2) & (d >= half) & (d < 2 * half) & (r == d), 1.0, 0.0)
    place = place.astype(BF16)
    one = jnp.where((grp[0:1] == 0) & (d[0:1] >= 2 * half), 1.0, 0.0)
    return sum(jnp.dot(t, place, preferred_element_type=F32) for t in _split3(cs)) + one


def _tile_gain(g, w):
    return jnp.tile(g.astype(F32), w // g.shape[0])[None, :]


P0_OUTS = dict(qa=(WIDTH_A, BF16), ka=(WIDTH_A, F32), ka16=(WIDTH_A, BF16), va=(WIDTH_A, F32), ga=(WIDTH_A, F32),
               qi=(WIDTH_I, BF16), ki=(IDX_DIM, F32), ki16=(LANES, BF16), wi=(N_IDX_HEADS, F32),
               qb=(WIDTH_B, BF16), kb=(WIDTH_B, F32), kb16=(WIDTH_B, BF16), vb=(WIDTH_B, F32), gb=(WIDTH_B, F32))
P1_OUTS = dict(q16=(WIDTH_C, BF16), k=(WIDTH_C, F32), k16=(WIDTH_C, BF16), v=(WIDTH_C, F32), gate=(WIDTH_C, F32),
               logf=(N_HEADS_C, F32))


def _even_splits():
    sizes = (WIDTH_A,) * 4 + (WIDTH_I, IDX_DIM, N_IDX_HEADS) + (WIDTH_B,) * 4
    offs = [0]
    for s in sizes:
        offs.append(offs[-1] + s)
    return offs


def _proj0(x2d, pos, p, tm, want):
    m, d = x2d.shape
    offs = _even_splits()
    w_in = p["w_in"]
    cols = [w_in[:, offs[i]:offs[i + 1]] for i in range(len(offs) - 1)]
    w_big = jnp.concatenate([cols[0], cols[1], cols[2], cols[3], cols[4], cols[7], cols[8], cols[9], cols[10]],
                            axis=1).astype(BF16)
    w_small = jnp.concatenate(
        [cols[5], cols[5], cols[6], jnp.zeros((d, LANES - N_IDX_HEADS), w_in.dtype)], axis=1).astype(BF16)
    gains = jnp.concatenate([_tile_gain(p["qn_a"], WIDTH_A), _tile_gain(p["kn_a"], WIDTH_A),
                             _tile_gain(p["qn_b"], WIDTH_B), _tile_gain(p["kn_b"], WIDTH_B)], axis=0)
    gki = _tile_gain(p["kn_idx"], LANES)
    rope = _rope_table(pos)
    out_shape = tuple(jax.ShapeDtypeStruct((m, P0_OUTS[n][0]), P0_OUTS[n][1]) for n in want)
    out_specs = tuple(_row_spec(tm, s.shape[1]) for s in out_shape)
    outs = pl.pallas_call(
        functools.partial(_proj0_kernel, want=tuple(want)),
        grid=(m // tm,),
        in_specs=[_row_spec(tm, d), _const_spec((1, d)), _const_spec(w_big.shape), _const_spec(w_small.shape),
                  _row_spec(tm, rope.shape[1]), _const_spec(gains.shape), _const_spec(gki.shape),
                  _const_spec((MXU_DIM, MXU_DIM))],
        out_specs=out_specs,
        out_shape=out_shape,
        compiler_params=_cparams(1),
        name="proj0",
    )(x2d, p["norm"].astype(F32)[None, :], w_big, w_small, rope, gains, gki, _block_diag_mean())
    return dict(zip(want, outs))


def _proj1(x2d, p, tm, want):
    m, d = x2d.shape
    wc = WIDTH_C
    w_in = p["w_in"]
    w_big = w_in[:, :4 * wc].astype(BF16)
    w_f = jnp.concatenate([w_in[:, 4 * wc:], jnp.zeros((d, LANES - N_HEADS_C), w_in.dtype)], axis=1).astype(BF16)
    gains = jnp.concatenate([_tile_gain(p["qn"], wc), _tile_gain(p["kn"], wc)], axis=0)
    out_shape = tuple(jax.ShapeDtypeStruct((m, P1_OUTS[n][0]), P1_OUTS[n][1]) for n in want)
    out_specs = tuple(_row_spec(tm, s.shape[1]) for s in out_shape)
    outs = pl.pallas_call(
        functools.partial(_proj1_kernel, want=tuple(want)),
        grid=(m // tm,),
        in_specs=[_row_spec(tm, d), _const_spec((1, d)), _const_spec(w_big.shape), _const_spec(w_f.shape),
                  _const_spec((1, N_HEADS_C)), _const_spec(gains.shape), _const_spec((MXU_DIM, MXU_DIM))],
        out_specs=out_specs,
        out_shape=out_shape,
        compiler_params=_cparams(1),
        name="proj1",
    )(x2d, p["norm"].astype(F32)[None, :], w_big, w_f, p["b_f"].astype(F32)[None, :], gains, _block_diag_mean())
    return dict(zip(want, outs))


def _log_sigmoid(f):
    return jnp.minimum(f, 0.0) - jnp.log1p(jnp.exp(-jnp.abs(f)))


def _head_norm_t(h, bd, gain):
    w, tm = h.shape
    hi, lo = _split2(h * h)
    bd16 = bd.astype(BF16)
    if w >= MXU_DIM:
        slabs = [jnp.dot(bd16, hi[c * MXU_DIM:(c + 1) * MXU_DIM, :], preferred_element_type=F32)
                 + jnp.dot(bd16, lo[c * MXU_DIM:(c + 1) * MXU_DIM, :], preferred_element_type=F32)
                 for c in range(w // MXU_DIM)]
        ms = slabs[0] if len(slabs) == 1 else jnp.concatenate(slabs, axis=0)
    else:
        ms = (jnp.dot(bd16[:w, :w], hi, preferred_element_type=F32)
              + jnp.dot(bd16[:w, :w], lo, preferred_element_type=F32))
    return h * lax.rsqrt(ms + EPS) * jnp.tile(gain, (1, tm // LANES))


def _rope_t(y, rope_ref):
    half = HEAD_DIM // 8
    cos = rope_ref[0:half, :]
    sin = rope_ref[half:2 * half, :]
    parts = []
    for h in range(y.shape[0] // HEAD_DIM):
        r0 = h * HEAD_DIM
        x1 = y[r0:r0 + half, :]
        x2 = y[r0 + half:r0 + 2 * half, :]
        parts += [x1 * cos - x2 * sin, x2 * cos + x1 * sin, y[r0 + 2 * half:r0 + HEAD_DIM, :]]
    return jnp.concatenate(parts, axis=0)


def _proj_t_kernel(*refs, specs, pw, small):
    x_ref, g_ref, wt_ref, ws_ref, rope_ref, gains_ref, gs_ref, bd_ref = refs[:8]
    outs = list(refs[8:])
    xb = _rms_rows(x_ref[...], g_ref[...]).astype(BF16)
    bd = bd_ref[...]
    tm = xb.shape[0]
    k = 0
    for j, (gain_row, rope, scale, kinds) in enumerate(specs):
        h = lax.dot_general(wt_ref[j * pw:(j + 1) * pw, :], xb, NT_DIMS, preferred_element_type=F32)
        if gain_row is not None:
            h = _head_norm_t(h, bd, gains_ref[gain_row])
        if rope:
            h = _rope_t(h, rope_ref)
        if scale != 1.0:
            h = h * scale
        for dt in kinds:
            outs[k][0] = h.astype(dt)
            k += 1
    if small == "idx":
        hs = lax.dot_general(ws_ref[...], xb, NT_DIMS, preferred_element_type=F32)
        outs[k][0] = _rope_t(_head_norm_t(hs[0:IDX_DIM, :], bd, gs_ref[...]), rope_ref)
        outs[k + 1][0] = hs[IDX_DIM:IDX_DIM + N_IDX_HEADS, :] * (WIDTH_I ** -0.5)
    if small == "logf":
        f = lax.dot_general(ws_ref[...], xb, NT_DIMS, preferred_element_type=F32)
        outs[k][0] = _log_sigmoid(f + jnp.tile(gs_ref[...], (1, tm // LANES)))


def _rope_table_t(pos):
    rot = HEAD_DIM // 4
    half = rot // 2
    inv = jnp.power(F32(ROPE_THETA), -jnp.arange(half, dtype=F32) * (2.0 / rot))
    ang = inv[:, None] * pos.astype(F32)[None, :]
    return jnp.concatenate([jnp.cos(ang), jnp.sin(ang)], axis=0)


def _lane_replicated(col):
    return jnp.broadcast_to(col.astype(F32)[:, None], (col.shape[0], LANES))


def _proj_t(x3, norm, blocks, gains, pos, small, tm):
    b, seq, d = x3.shape
    m = b * seq
    nt = seq // tm
    pw = blocks[0][0].shape[1]
    wt = jnp.concatenate([blk[0].T for blk in blocks], axis=0).astype(BF16)
    specs = tuple(tuple(blk[1:]) for blk in blocks)
    gains_arr = (jnp.zeros((1, pw, LANES), F32) if not gains else
                 jnp.stack([_lane_replicated(jnp.tile(g, pw // g.shape[0])) for g in gains]))
    rope = jnp.zeros((2 * SUBLANES, m), F32) if pos is None else _rope_table_t(pos)
    out_rows = [(pw, dt) for blk in blocks for dt in blk[4]]
    kind = None
    ws = jnp.zeros((SUBLANES, d), BF16)
    gs = jnp.zeros((SUBLANES, LANES), F32)
    if small is not None:
        kind = small[0]
        if kind == "idx":
            ws = jnp.concatenate([small[1].T, small[2].T], axis=0).astype(BF16)
            gs = _lane_replicated(small[3])
            out_rows += [(IDX_DIM, F32), (N_IDX_HEADS, F32)]
        else:
            ws = small[1].T.astype(BF16)
            gs = _lane_replicated(small[2])
            out_rows += [(N_HEADS_C, F32)]
    out_shape = [jax.ShapeDtypeStruct((b, r, seq), dt) for r, dt in out_rows]
    out_specs = [pl.BlockSpec((1, r, tm), lambda i: (i // nt, 0, i % nt)) for r, _ in out_rows]
    return pl.pallas_call(
        functools.partial(_proj_t_kernel, specs=specs, pw=pw, small=kind),
        grid=(m // tm,),
        in_specs=[_row_spec(tm, d), _const_spec((1, d)), _const_spec(wt.shape), _const_spec(ws.shape),
                  pl.BlockSpec((2 * SUBLANES, tm), lambda i: (0, i)), _const_spec(gains_arr.shape),
                  _const_spec(gs.shape), _const_spec((MXU_DIM, MXU_DIM))],
        out_specs=out_specs,
        out_shape=out_shape,
        compiler_params=_cparams(1),
        name="proj_t",
    )(x3.reshape(m, d), norm.astype(F32)[None, :], wt, ws, rope, gains_arr, gs, _block_diag_mean())


def _silu(g):
    return g * jax.nn.sigmoid(g)


def _out0_kernel(x_ref, oa_ref, ga_ref, ob_ref, gb_ref, sub_ref, w_ref, y_ref, *, post_scale):
    ya = (oa_ref[...] * _silu(ga_ref[...])).astype(BF16)
    ob = ob_ref[...]
    cols = []
    for h in range(N_HEADS_B):
        oc = ob[:, h * LANES:(h + 1) * LANES]
        ms = jnp.mean(oc * oc, axis=-1, keepdims=True)
        cols.append(oc * lax.rsqrt(ms + EPS))
    obn = jnp.concatenate(cols, axis=-1) * sub_ref[...] * post_scale
    yb = (obn * _silu(gb_ref[...])).astype(BF16)
    y = jnp.dot(ya, w_ref[0:WIDTH_A, :], preferred_element_type=F32)
    y = y + jnp.dot(yb, w_ref[WIDTH_A:WIDTH_A + WIDTH_B, :], preferred_element_type=F32)
    y_ref[...] = x_ref[...] + y


def _out1_kernel(x_ref, o_ref, g_ref, w_ref, y_ref):
    yo = (o_ref[...] * _silu(g_ref[...])).astype(BF16)
    y_ref[...] = x_ref[...] + jnp.dot(yo, w_ref[...], preferred_element_type=F32)


def _out0(x2d, o_a, g_a, o_b, g_b, p, lam_init, tm):
    m, d = x2d.shape
    sub = _tile_gain(p["subln_b"], WIDTH_B)
    w = p["w_out"].astype(BF16)
    return pl.pallas_call(
        functools.partial(_out0_kernel, post_scale=1.0 - lam_init),
        grid=(m // tm,),
        in_specs=[_row_spec(tm, d), _row_spec(tm, WIDTH_A), _row_spec(tm, WIDTH_A), _row_spec(tm, WIDTH_B),
                  _row_spec(tm, WIDTH_B), _const_spec(sub.shape), _const_spec(w.shape)],
        out_specs=_row_spec(tm, d),
        out_shape=jax.ShapeDtypeStruct((m, d), F32),
        compiler_params=_cparams(1),
        name="out0",
    )(x2d, o_a, g_a, o_b, g_b, sub, w)


def _out1(x2d, o, g, p, tm):
    m, d = x2d.shape
    w = p["w_out"].astype(BF16)
    return pl.pallas_call(
        _out1_kernel,
        grid=(m // tm,),
        in_specs=[_row_spec(tm, d), _row_spec(tm, WIDTH_C), _row_spec(tm, WIDTH_C), _const_spec(w.shape)],
        out_specs=_row_spec(tm, d),
        out_shape=jax.ShapeDtypeStruct((m, d), F32),
        compiler_params=_cparams(1),
        name="out1",
    )(x2d, o, g, w)


def _select_topk(load_keys, n_chunks, groups, rows, nsel, idx_bits, idx_of):
    shape = (rows, LANES)
    rb = min(rows, COUNT_ROWS)
    n_blk = rows // rb

    def count(pred_of):
        accs = []
        for b in range(n_blk):
            pred = pred_of(slice(b * rb, (b + 1) * rb))

            def body(c, acc, b=b, pred=pred):
                blk = load_keys(c, b * rb, rb)
                for g in range(groups):
                    acc = jnp.where(pred(blk[:, g * LANES:(g + 1) * LANES], c, g), acc + 1.0, acc)
                return acc
            accs.append(lax.fori_loop(0, n_chunks, body, jnp.zeros((rb, LANES), F32),
                                      unroll=isinstance(n_chunks, int)))
        outs = [jnp.broadcast_to(jnp.sum(a, axis=-1, keepdims=True), (rb, LANES)) for a in accs]
        return outs[0] if n_blk == 1 else jnp.concatenate(outs, axis=0)

    kf = float(nsel)

    def bit_step(i, thr):
        bit = lax.shift_left(jnp.int32(1), jnp.int32(31) - i)
        cand = thr ^ bit

        def pred_of(sl):
            cb = cand[sl]
            return lambda k, c, g: k >= cb
        return jnp.where(count(pred_of) >= kf, cand, thr)

    thr = lax.fori_loop(0, 32, bit_step, jnp.full(shape, jnp.iinfo(jnp.int32).min, I32))
    thr = jnp.maximum(thr, KEY_NEG_INF + 1)

    def gt_of(sl):
        tb = thr[sl]
        return lambda k, c, g: k > tb

    def ge_of(sl):
        tb = thr[sl]
        return lambda k, c, g: k >= tb

    n_gt = count(gt_of)
    n_ge = count(ge_of)
    need = kf - n_gt
    excess = n_ge - kf

    def tie_cut():
        def idx_step(i, cut):
            bit = lax.shift_left(jnp.int32(1), jnp.int32(idx_bits - 1) - i)
            cand = cut | bit

            def pred_of(sl):
                tb, cb = thr[sl], cand[sl]
                return lambda k, c, g: (k == tb) & (idx_of(c, g, rb) < cb)
            return jnp.where(count(pred_of) < need, cand, cut)
        return lax.fori_loop(0, idx_bits, idx_step, jnp.zeros(shape, I32))

    big = jnp.full(shape, jnp.iinfo(jnp.int32).max, I32)
    any_excess = jnp.max(excess) > 0.0
    cut = lax.cond(any_excess, lambda: jnp.where(excess > 0.0, tie_cut(), big), lambda: big)
    return thr, cut


def _select_topk_t(load_keys, n_chunks, rpc, cols, nsel, idx_bits):
    shape = (SUBLANES, cols)
    reps = rpc // SUBLANES
    n_part = math.gcd(reps, COUNT_PARTS)

    def count(pred):
        def body(c, acc):
            x = jnp.where(pred(load_keys(c).reshape(reps, SUBLANES, cols), c), 1.0, 0.0)
            part = jnp.sum(x.reshape(reps // n_part, n_part, SUBLANES, cols), axis=0)
            return acc + jnp.sum(part, axis=0)
        acc = lax.fori_loop(0, n_chunks, body, jnp.zeros(shape, F32))
        return jnp.broadcast_to(jnp.sum(acc, axis=0, keepdims=True), shape)

    kf = float(nsel)

    def bit_step(i, state):
        thr, n_thr = state
        bit = lax.shift_left(jnp.int32(1), jnp.int32(31) - i)
        cand = thr ^ bit
        n_cand = count(lambda k, c: k >= cand[None])
        take = n_cand >= kf
        return jnp.where(take, cand, thr), jnp.where(take, n_cand, n_thr)

    thr, n_thr = lax.fori_loop(
        0, 32, bit_step,
        (jnp.full(shape, jnp.iinfo(jnp.int32).min, I32), jnp.full(shape, 2.0 ** 30, F32)))
    excess = jnp.where(thr > KEY_NEG_INF, n_thr - kf, -1.0)
    thr = jnp.maximum(thr, KEY_NEG_INF + 1)
    row = lax.broadcasted_iota(I32, (rpc, cols), 0).reshape(reps, SUBLANES, cols)

    def tie_cut():
        need = kf - count(lambda k, c: k > thr[None])

        def idx_step(i, cut):
            bit = lax.shift_left(jnp.int32(1), jnp.int32(idx_bits - 1) - i)
            cand = cut | bit
            n_lt = count(lambda k, c: (k == thr[None]) & (row + c * rpc < cand[None]))
            return jnp.where(n_lt < need, cand, cut)
        return lax.fori_loop(0, idx_bits, idx_step, jnp.zeros(shape, I32))

    big = jnp.full(shape, jnp.iinfo(jnp.int32).max, I32)
    any_excess = jnp.max(excess) > 0.0
    cut = lax.cond(any_excess, lambda: jnp.where(excess > 0.0, tie_cut(), big), lambda: big)
    return thr, cut


def _masked_pairs_t(qt_ref, qm_ref, n_pairs, tq):
    top = lax.broadcasted_iota(I32, (LANES, tq), 0) < HEAD_DIM
    for j in range(n_pairs):
        pair = qt_ref[0, j * LANES:(j + 1) * LANES, :]
        zero = jnp.zeros_like(pair)
        qm_ref[2 * j] = jnp.where(top, pair, zero)
        qm_ref[2 * j + 1] = jnp.where(top, zero, pair)


def _idx_prompt_t_kernel(qit_ref, wit_ref, ki_ref, bias_ref, qm_ref, s_ref, *, tq, tk, nsel, seq):
    qi = pl.program_id(1)
    q0 = qi * tq
    n_chunks = seq // tk
    nc = (q0 + tq + tk - 1) // tk
    _masked_pairs_t(qit_ref, qm_ref, N_IDX_HEADS // 2, tq)
    wt = wit_ref[0]
    col = q0 + lax.broadcasted_iota(I32, (tk, tq), 1)
    row0 = lax.broadcasted_iota(I32, (tk, tq), 0)

    def score_body(c, carry):
        start = pl.multiple_of(c * tk, tk)
        kblk = ki_ref[0, pl.ds(start, tk), :]
        acc = jnp.zeros((tk, tq), F32)
        for h in range(N_IDX_HEADS):
            s = jnp.dot(kblk, qm_ref[h], preferred_element_type=F32)
            acc = acc + jnp.maximum(s, 0.0) * wt[h:h + 1, :]
        acc = jnp.where(row0 + c * tk <= col, acc, -jnp.inf)
        s_ref[c] = _sort_key(acc)
        return carry

    lax.fori_loop(0, nc, score_body, 0)
    thr, cut = _select_topk_t(lambda c: s_ref[c], nc, tk, tq, nsel, int(math.log2(seq)) + 1)
    thr_t = jnp.tile(thr, (tk // SUBLANES, 1))
    cut_t = jnp.tile(cut, (tk // SUBLANES, 1))

    def write_body(c, carry):
        k = s_ref[c]
        sel = (k > thr_t) | ((k == thr_t) & (row0 + c * tk <= cut_t))
        start = pl.multiple_of(c * tk, tk)
        bias_ref[0, 0, pl.ds(start, tk), :] = jnp.where(sel, 0.0, NEG).astype(BF16)
        return carry

    lax.fori_loop(0, nc, write_body, 0)

    def fill_body(c, carry):
        start = pl.multiple_of(c * tk, tk)
        bias_ref[0, 0, pl.ds(start, tk), :] = jnp.full((tk, tq), NEG, BF16)
        return carry

    lax.fori_loop(nc, n_chunks, fill_body, 0)


def _idx_prompt_t(qit, wit, ki16, nsel, tq, tq_out, tk):
    b, seq, _ = ki16.shape
    nq = seq // tq
    per = tq_out // tq
    return pl.pallas_call(
        functools.partial(_idx_prompt_t_kernel, tq=tq, tk=tk, nsel=nsel, seq=seq),
        grid=(b, nq),
        in_specs=[pl.BlockSpec((1, WIDTH_I, tq), lambda bi, qi: (bi, 0, qi)),
                  pl.BlockSpec((1, N_IDX_HEADS, tq), lambda bi, qi: (bi, 0, qi)),
                  pl.BlockSpec((1, seq, LANES), lambda bi, qi: (bi, 0, 0))],
        out_specs=pl.BlockSpec((1, 1, seq, tq), lambda bi, qi: (bi, qi // per, 0, qi % per)),
        out_shape=jax.ShapeDtypeStruct((b, seq // tq_out, seq, tq_out), BF16),
        scratch_shapes=[pltpu.VMEM((N_IDX_HEADS, LANES, tq), BF16),
                        pltpu.VMEM((seq // tk, tk, tq), I32)],
        compiler_params=_cparams(2),
        name="idx_prompt",
    )(qit, wit, ki16)


def _flash_t_kernel(*refs, variant, n_pairs, tq, tk, sub):
    it = iter(refs)
    k_ref, qt_ref, vt_ref = next(it), next(it), next(it)
    bias_ref = next(it) if variant == "dsa" else None
    cqt_ref, ck_ref = (next(it), next(it)) if variant == "fox" else (None, None)
    lam_ref = next(it) if variant == "diff" else None
    o_ref = next(it)
    qm_ref, m_ref, l_ref, acc_ref = next(it), next(it), next(it), next(it)
    bias32_ref = next(it) if variant == "dsa" else None

    qi = pl.program_id(1)
    kc = pl.program_id(2)
    nk = pl.num_programs(2)
    last = ((qi + 1) * tq - 1) // tk
    n_units = 2 * n_pairs
    vr = acc_ref.shape[1]

    @pl.when(kc == 0)
    def _init():
        _masked_pairs_t(qt_ref, qm_ref, n_pairs, tq)
        m_ref[...] = jnp.full(m_ref.shape, NEG, F32)
        l_ref[...] = jnp.zeros(l_ref.shape, F32)
        acc_ref[...] = jnp.zeros(acc_ref.shape, F32)

    def compute(masked):
        if variant == "dsa":
            bias32_ref[...] = bias_ref[0, 0].astype(F32)
            masked = False
        if masked:
            col = qi * tq + lax.broadcasted_iota(I32, (sub, tq), 1)
            row0 = kc * tk + lax.broadcasted_iota(I32, (sub, tq), 0)
        ones_rows = jnp.ones((2 * SUBLANES, sub), BF16)
        if variant == "fox":
            ck_all = ck_ref[0] * LOG2E
        for g0 in range(0, n_units, FLASH_GROUP):
            units = list(range(g0, min(g0 + FLASH_GROUP, n_units)))
            state = [[m_ref[u], l_ref[u], acc_ref[u]] for u in units]
            for r0 in range(0, tk, sub):
                ss = [jnp.dot(k_ref[0, r0:r0 + sub, (u // 2) * LANES:(u // 2 + 1) * LANES], qm_ref[u],
                              preferred_element_type=F32) for u in units]
                ps = []
                for e, u in enumerate(units):
                    s = ss[e]
                    if variant == "dsa":
                        s = s + bias32_ref[r0:r0 + sub, :]
                    cq_row = 0.0
                    if variant == "fox":
                        s = s - jnp.broadcast_to(ck_all[r0:r0 + sub, u:u + 1], (sub, tq))
                        cq_row = cqt_ref[0, u:u + 1, :] * LOG2E
                    if masked:
                        s = jnp.where(row0 + r0 <= col, s, NEG)
                    m_prev, l_prev, acc = state[e]
                    m_new = jnp.maximum(m_prev, jnp.max(s, axis=0, keepdims=True) + cq_row)
                    alpha = jnp.exp2(m_prev - m_new)
                    p = jnp.exp2(s - jnp.tile(m_new - cq_row, (sub // SUBLANES, 1)))
                    state[e][0] = m_new
                    state[e][1] = alpha * l_prev
                    state[e][2] = jnp.tile(alpha, (vr // SUBLANES, 1)) * acc
                    ps.append(p.astype(BF16))
                for e, u in enumerate(units):
                    v0 = (u // 2) * vr if variant == "diff" else u * vr
                    v_aug = jnp.concatenate([vt_ref[0, v0:v0 + vr, r0:r0 + sub], ones_rows], axis=0)
                    pv = jnp.dot(v_aug, ps[e], preferred_element_type=F32)
                    state[e][1] = state[e][1] + pv[vr:vr + SUBLANES, :]
                    state[e][2] = state[e][2] + pv[0:vr, :]
            for e, u in enumerate(units):
                m_ref[u], l_ref[u], acc_ref[u] = state[e]

    needs_mask = (kc + 1) * tk - 1 > qi * tq

    @pl.when((kc <= last) & needs_mask)
    def _diag():
        compute(True)

    @pl.when((kc <= last) & jnp.logical_not(needs_mask))
    def _full():
        compute(False)

    @pl.when(kc == nk - 1)
    def _fin():
        def norm(u):
            return acc_ref[u] / jnp.tile(l_ref[u], (vr // SUBLANES, 1))
        for j in range(n_pairs):
            if variant == "diff":
                ot = norm(2 * j) - lam_ref[0, 0] * norm(2 * j + 1)
            else:
                ot = jnp.concatenate([norm(2 * j), norm(2 * j + 1)], axis=0)
            o_ref[0, :, j * LANES:(j + 1) * LANES] = ot.T


def _flash_t(variant, k16, qt, vt, tq, tk, bias=None, cqt=None, ck=None, lam=None):
    b, seq, w = k16.shape
    n_pairs = w // LANES
    n_units = 2 * n_pairs
    nq, nk = seq // tq, seq // tk
    vr = LANES if variant == "diff" else HEAD_DIM
    sub = min(tk, FLASH_SUB)

    def kc_of(qi, kc):
        return jnp.minimum(kc, ((qi + 1) * tq - 1) // tk)

    in_specs = [pl.BlockSpec((1, tk, w), lambda bi, qi, kc: (bi, kc_of(qi, kc), 0)),
                pl.BlockSpec((1, w, tq), lambda bi, qi, kc: (bi, 0, qi)),
                pl.BlockSpec((1, vt.shape[1], tk), lambda bi, qi, kc: (bi, 0, kc_of(qi, kc)))]
    args = [k16, qt, vt]
    scratch = [pltpu.VMEM((n_units, LANES, tq), BF16), pltpu.VMEM((n_units, SUBLANES, tq), F32),
               pltpu.VMEM((n_units, SUBLANES, tq), F32), pltpu.VMEM((n_units, vr, tq), F32)]
    if variant == "dsa":
        in_specs.append(pl.BlockSpec((1, 1, tk, tq), lambda bi, qi, kc: (bi, qi, kc_of(qi, kc), 0)))
        args.append(bias)
        scratch.append(pltpu.VMEM((tk, tq), F32))
    if variant == "fox":
        in_specs.append(pl.BlockSpec((1, n_units, tq), lambda bi, qi, kc: (bi, 0, qi)))
        in_specs.append(pl.BlockSpec((1, tk, n_units), lambda bi, qi, kc: (bi, kc_of(qi, kc), 0)))
        args += [cqt, ck]
    if variant == "diff":
        in_specs.append(pl.BlockSpec(memory_space=pltpu.SMEM))
        args.append(lam)
    return pl.pallas_call(
        functools.partial(_flash_t_kernel, variant=variant, n_pairs=n_pairs, tq=tq, tk=tk, sub=sub),
        grid=(b, nq, nk),
        in_specs=in_specs,
        out_specs=pl.BlockSpec((1, tq, w), lambda bi, qi, kc: (bi, qi, 0)),
        out_shape=jax.ShapeDtypeStruct((b, seq, w), F32),
        scratch_shapes=scratch,
        compiler_params=_cparams(3),
        name="flash_" + variant,
    )(*args)


PAGES_PER_STEP = 8
SMALL_PAGES_PER_STEP = 64


def _pages_per_step(n_pages, pref=PAGES_PER_STEP):
    g = pref
    while n_pages % g:
        g //= 2
    return g


def _upper_ones():
    r = lax.broadcasted_iota(I32, (LANES, LANES), 0)
    c = lax.broadcasted_iota(I32, (LANES, LANES), 1)
    return jnp.where(r <= c, 1.0, 0.0).astype(BF16)


def _cumsum_kernel(*refs, g):
    x_refs = refs[1:1 + g]
    xn_ref, c_ref, cn_ref, carry_ref = refs[1 + g:5 + g]
    p = pl.program_id(1)
    upper = _upper_ones()

    @pl.when(p == 0)
    def _():
        carry_ref[...] = jnp.zeros(carry_ref.shape, F32)

    def page_cumsum(x):
        return sum(jnp.dot(t, upper, preferred_element_type=F32) for t in _split3(x))

    local = [page_cumsum(x_refs[j][0]) for j in range(g)]
    total = carry_ref[...]
    for j in range(g):
        c_ref[0, :, j * LANES:(j + 1) * LANES] = local[j] + total
        total = total + jnp.broadcast_to(local[j][:, LANES - 1:LANES], total.shape)
    carry_ref[...] = total

    @pl.when(p == pl.num_programs(1) - 1)
    def _():
        cn_ref[0] = page_cumsum(xn_ref[0]) + total


def _cumsum_pages(page_table, pool_t, new_t):
    b, n_pages = page_table.shape
    h = pool_t.shape[1]
    g = _pages_per_step(n_pages, SMALL_PAGES_PER_STEP)
    pool_specs = [pl.BlockSpec((1, h, LANES), lambda bi, p, pt, j=j: (pt[bi, p * g + j], 0, 0)) for j in range(g)]
    grid_spec = pltpu.PrefetchScalarGridSpec(
        num_scalar_prefetch=1,
        grid=(b, n_pages // g),
        in_specs=pool_specs + [pl.BlockSpec((1, h, LANES), lambda bi, p, pt: (bi, 0, 0))],
        out_specs=[pl.BlockSpec((1, h, g * LANES), lambda bi, p, pt: (bi, 0, p)),
                   pl.BlockSpec((1, h, LANES), lambda bi, p, pt: (bi, 0, 0))],
        scratch_shapes=[pltpu.VMEM((h, LANES), F32)],
    )
    return pl.pallas_call(
        functools.partial(_cumsum_kernel, g=g),
        grid_spec=grid_spec,
        out_shape=(jax.ShapeDtypeStruct((b, h, n_pages * LANES), F32), jax.ShapeDtypeStruct((b, h, LANES), F32)),
        compiler_params=_cparams(2),
        name="cumsum_pages",
    )(page_table, *([pool_t] * g), new_t)


def _idx_decode_kernel(*refs, g, n_pages, n_new, nsel, page):
    q_ref, w_ref = refs[1:3]
    k_refs = refs[3:3 + g]
    knew_ref, bias_ref, biasn_ref, s_ref = refs[3 + g:7 + g]
    p = pl.program_id(1)
    n_steps = n_pages // g
    rows = n_new

    def scores(kt):
        n = kt.shape[1] // LANES
        s = jnp.dot(q_ref[0], kt.astype(BF16), preferred_element_type=F32)
        t = jnp.maximum(s, 0.0) * jnp.tile(w_ref[0], (1, n))
        acc = t[0:rows]
        for h in range(1, N_IDX_HEADS):
            acc = acc + t[h * rows:(h + 1) * rows]
        return acc

    @pl.when(p < n_steps)
    def _past():
        keys = _sort_key(scores(jnp.concatenate([k_refs[j][0] for j in range(g)], axis=1)))
        for j in range(g):
            s_ref[p * g + j] = keys[:, j * LANES:(j + 1) * LANES]

    @pl.when(p == n_steps)
    def _new():
        sc = scores(knew_ref[0])
        i = lax.broadcasted_iota(I32, (rows, LANES), 0)
        lane = lax.broadcasted_iota(I32, (rows, LANES), 1)
        s_ref[n_pages] = _sort_key(jnp.where(lane <= i, sc, -jnp.inf))

        def idx_of(c, grp, nr):
            return lane + c * page

        n_keys = (n_pages + 1) * page
        thr, cut = _select_topk(lambda c, r0, nr: s_ref[c], n_pages + 1, 1, rows, nsel,
                                int(math.log2(n_keys)) + 1, idx_of)

        def selected(c):
            k = s_ref[c]
            sel = (k > thr) | ((k == thr) & (lane + c * page <= cut))
            return jnp.where(sel, 0.0, NEG)

        def write_body(c, carry):
            bias_ref[0, c] = selected(c)
            return carry

        lax.fori_loop(0, n_pages, write_body, 0)
        biasn_ref[0] = selected(n_pages)


def _idx_decode(page_table, q_st, w_st, kt_pool, kt_new, nsel):
    b, n_pages = page_table.shape
    page = kt_pool.shape[2]
    n_new = q_st.shape[1] // N_IDX_HEADS
    g = _pages_per_step(n_pages, SMALL_PAGES_PER_STEP)
    n_steps = n_pages // g

    def req_map(bi, p, pt):
        return (bi, 0, 0)

    pool_specs = [pl.BlockSpec((1, IDX_DIM, page),
                               lambda bi, p, pt, j=j: (pt[bi, jnp.minimum(p, n_steps - 1) * g + j], 0, 0))
                  for j in range(g)]
    grid_spec = pltpu.PrefetchScalarGridSpec(
        num_scalar_prefetch=1,
        grid=(b, n_steps + 1),
        in_specs=[pl.BlockSpec((1,) + q_st.shape[1:], req_map), pl.BlockSpec((1,) + w_st.shape[1:], req_map)]
        + pool_specs + [pl.BlockSpec((1, IDX_DIM, page), req_map)],
        out_specs=[pl.BlockSpec((1, n_pages, n_new, LANES), lambda bi, p, pt: (bi, 0, 0, 0)),
                   pl.BlockSpec((1, n_new, LANES), req_map)],
        scratch_shapes=[pltpu.VMEM((n_pages + 1, n_new, LANES), I32)],
    )
    return pl.pallas_call(
        functools.partial(_idx_decode_kernel, g=g, n_pages=n_pages, n_new=n_new, nsel=nsel, page=page),
        grid_spec=grid_spec,
        out_shape=(jax.ShapeDtypeStruct((b, n_pages, n_new, LANES), F32),
                   jax.ShapeDtypeStruct((b, n_new, LANES), F32)),
        compiler_params=_cparams(2),
        name="idx_decode",
    )(page_table, q_st, w_st, *([kt_pool] * g), kt_new)


def _attn_decode_kernel(*refs, variant, g, n_units, n_new, n_pages):
    it = iter(refs)
    next(it)
    q_ref = next(it)
    k_refs = [next(it) for _ in range(g)]
    v_refs = [next(it) for _ in range(g)]
    knew_ref, vnew_ref = next(it), next(it)
    bias_ref, biasn_ref = (next(it), next(it)) if variant == "dsa" else (None, None)
    cq_ref, ck_ref, cn_ref = (next(it), next(it), next(it)) if variant == "fox" else (None, None, None)
    lam_ref = next(it) if variant == "diff" else None
    o_ref = next(it)
    qbd_ref, m_ref, l_ref, acc_ref = next(it), next(it), next(it), next(it)

    p = pl.program_id(1)
    n_steps = n_pages // g
    rows = n_units * n_new
    w = q_ref.shape[-1]
    wa = acc_ref.shape[-1]

    @pl.when(p == 0)
    def _init():
        q = q_ref[0].astype(F32)
        qt = jnp.concatenate([q] * n_units, axis=0)
        r = _div_pow2(lax.broadcasted_iota(I32, (rows, w), 0), n_new)
        c = _div_pow2(lax.broadcasted_iota(I32, (rows, w), 1), HEAD_DIM)
        qbd_ref[...] = jnp.where(r == c, qt, 0.0).astype(BF16)
        m_ref[...] = jnp.full(m_ref.shape, NEG, F32)
        l_ref[...] = jnp.zeros(l_ref.shape, F32)
        acc_ref[...] = jnp.zeros(acc_ref.shape, F32)

    def expand_rows(x8):
        return jnp.concatenate([x8] * n_units, axis=0)

    def expand_units(xu):
        return jnp.concatenate([jnp.broadcast_to(xu[u:u + 1, :], (n_new, xu.shape[1])) for u in range(n_units)],
                               axis=0)

    def v_head(v_ref, h, is_new):
        if is_new or len(v_ref.shape) == 3:
            return v_ref[0, :, h * LANES:(h + 1) * LANES]
        return v_ref[0, :, h, :]

    def step(kts, vs, bias, ck, is_new):
        ng = len(kts)
        kt_all = kts[0][0] if ng == 1 else jnp.concatenate([kts[j][0] for j in range(ng)], axis=1)
        s = jnp.dot(qbd_ref[...], kt_all.astype(BF16), preferred_element_type=F32)
        if variant == "dsa":
            s = s + bias
        if variant == "fox":
            s = s + (jnp.tile(cq_ref[0], (1, ng)) - expand_units(ck))
        if is_new:
            i = expand_rows(lax.broadcasted_iota(I32, (n_new, LANES), 0))
            lane = lax.broadcasted_iota(I32, (rows, LANES), 1)
            s = jnp.where(lane <= i, s, NEG)
        m_prev = m_ref[...]
        m_new = jnp.maximum(m_prev, jnp.max(s, axis=-1, keepdims=True))
        alpha = jnp.exp(m_prev - m_new)
        pr = jnp.exp(s - jnp.tile(m_new, (1, ng)))
        l_ref[...] = alpha * l_ref[...] + jnp.sum(pr, axis=-1, keepdims=True)
        m_ref[...] = m_new
        p16 = pr.astype(BF16)
        if variant == "diff":
            hr = 2 * n_new
            pv = jnp.concatenate(
                [jnp.dot(p16[h * hr:(h + 1) * hr],
                         jnp.concatenate([v_head(vs[j], h, is_new) for j in range(ng)], axis=0).astype(BF16),
                         preferred_element_type=F32) for h in range(n_units // 2)], axis=0)
        else:
            vt_all = vs[0][0] if ng == 1 else jnp.concatenate([vs[j][0] for j in range(ng)], axis=1)
            pv = lax.dot_general(p16, vt_all.astype(BF16), NT_DIMS, preferred_element_type=F32)
        acc_ref[...] = jnp.tile(alpha, (1, wa // LANES)) * acc_ref[...] + pv

    @pl.when(p < n_steps)
    def _past():
        bias = ck = None
        if variant == "dsa":
            bias = jnp.concatenate([expand_rows(bias_ref[0, j]) for j in range(g)], axis=1)
        if variant == "fox":
            ck = ck_ref[0]
        step(k_refs, v_refs, bias, ck, False)

    @pl.when(p == n_steps)
    def _new():
        bias = expand_rows(biasn_ref[0]) if variant == "dsa" else None
        ck = cn_ref[0] if variant == "fox" else None
        step([knew_ref], [vnew_ref], bias, ck, True)
        accn = acc_ref[...] / jnp.tile(l_ref[...], (1, wa // LANES))
        if variant == "diff":
            lam = lam_ref[0, 0]
            o_ref[0] = jnp.concatenate(
                [accn[(2 * h) * n_new:(2 * h + 1) * n_new] - lam * accn[(2 * h + 1) * n_new:(2 * h + 2) * n_new]
                 for h in range(n_units // 2)], axis=1)
        else:
            cu = _div_pow2(lax.broadcasted_iota(I32, (n_new, wa), 1), HEAD_DIM)
            out = jnp.zeros((n_new, wa), F32)
            for u in range(n_units):
                out = out + jnp.where(cu == u, accn[u * n_new:(u + 1) * n_new], 0.0)
            o_ref[0] = out


def _attn_decode(variant, page_table, q16, kt_pool, v_pool, kt_new, v_new,
                  bias=None, bias_new=None, cq=None, ck=None, cn=None, lam=None):
    b, n_pages = page_table.shape
    page = kt_pool.shape[2]
    n_new, w = q16.shape[1], q16.shape[2]
    n_units = w // HEAD_DIM
    rows = n_units * n_new
    g = _pages_per_step(n_pages)
    n_steps = n_pages // g
    wa = LANES if variant == "diff" else w

    def req_map(bi, p, pt):
        return (bi, 0, 0)

    def pool_spec(arr, j):
        nd = arr.ndim
        return pl.BlockSpec((1,) + arr.shape[1:],
                            lambda bi, p, pt: (pt[bi, jnp.minimum(p, n_steps - 1) * g + j],) + (0,) * (nd - 1))

    in_specs = ([pl.BlockSpec((1, n_new, w), req_map)]
                + [pool_spec(kt_pool, j) for j in range(g)] + [pool_spec(v_pool, j) for j in range(g)]
                + [pl.BlockSpec((1,) + kt_new.shape[1:], req_map), pl.BlockSpec((1,) + v_new.shape[1:], req_map)])
    args = [q16] + [kt_pool] * g + [v_pool] * g + [kt_new, v_new]
    if variant == "dsa":
        in_specs.append(pl.BlockSpec((1, g, n_new, LANES),
                                     lambda bi, p, pt: (bi, jnp.minimum(p, n_steps - 1), 0, 0)))
        in_specs.append(pl.BlockSpec((1, n_new, LANES), req_map))
        args += [bias, bias_new]
    if variant == "fox":
        in_specs.append(pl.BlockSpec((1, rows, LANES), req_map))
        in_specs.append(pl.BlockSpec((1, n_units, g * LANES), lambda bi, p, pt: (bi, 0, jnp.minimum(p, n_steps - 1))))
        in_specs.append(pl.BlockSpec((1, n_units, LANES), req_map))
        args += [cq, ck, cn]
    if variant == "diff":
        in_specs.append(pl.BlockSpec(memory_space=pltpu.SMEM))
        args.append(lam)
    wo = v_new.shape[2] if variant == "diff" else w
    grid_spec = pltpu.PrefetchScalarGridSpec(
        num_scalar_prefetch=1,
        grid=(b, n_steps + 1),
        in_specs=in_specs,
        out_specs=pl.BlockSpec((1, n_new, wo), req_map),
        scratch_shapes=[pltpu.VMEM((rows, w), BF16), pltpu.VMEM((rows, LANES), F32),
                        pltpu.VMEM((rows, LANES), F32), pltpu.VMEM((rows, wa), F32)],
    )
    return pl.pallas_call(
        functools.partial(_attn_decode_kernel, variant=variant, g=g, n_units=n_units, n_new=n_new,
                          n_pages=n_pages),
        grid_spec=grid_spec,
        out_shape=jax.ShapeDtypeStruct((b, n_new, wo), F32),
        compiler_params=_cparams(2),
        name="attn_decode_" + variant,
    )(page_table, *args)


def _feature_major_pages(cache):
    n_pool, page = cache.shape[0], cache.shape[1]
    perm = (0,) + tuple(range(2, cache.ndim)) + (1,)
    return jnp.transpose(cache, perm).reshape(n_pool, -1, page)


def _feature_major_new(x3, page):
    xt = jnp.swapaxes(x3, 1, 2)
    return jnp.pad(xt, ((0, 0), (0, 0), (0, page - xt.shape[2])))


def _diff_lambda(p, lam_init):
    def e(a, c):
        return jnp.exp(jnp.sum(a.astype(F32) * c.astype(F32)))
    return (e(p["lam_q1"], p["lam_k1"]) - e(p["lam_q2"], p["lam_k2"]) + lam_init).reshape(1, 1).astype(F32)


def _pad_rows(x, rows):
    return jnp.pad(x, ((0, 0), (0, rows - x.shape[1]), (0, 0)))


def _tile_for(m, pref):
    t = min(m, pref)
    while m % t:
        t //= 2
    return t


def _token_major(xt, *tail):
    b, _, seq = xt.shape
    return jnp.swapaxes(xt, 1, 2).reshape((b, seq) + tail)


def _even_prompt(x, layer, p, tiles):
    b, seq, d = x.shape
    tm, tq, tk = tiles
    pos = jnp.tile(jnp.arange(seq), b)
    t = _proj0(x.reshape(b * seq, d), pos, p, tm, ("ka16", "ki16", "kb16", "vb", "ga", "gb"))
    lam_init = 0.8 - 0.6 * math.exp(-0.3 * layer)
    lam = _diff_lambda(p, lam_init)
    nsel = min(TOPK_MAX, seq // 4)
    r3 = lambda a: a.reshape(b, seq, a.shape[-1])
    offs = _even_splits()
    col = lambda i: p["w_in"][:, offs[i]:offs[i + 1]]
    log2_scale = QK_SCALE * LOG2E
    qat, kat, vat16, vat, qit, qbt, kbt, vbt16, kit, wit = _proj_t(
        x, p["norm"],
        [(col(0), 0, True, log2_scale, (BF16,)), (col(1), 1, True, 1.0, (F32,)),
         (col(2), None, False, 1.0, (BF16, F32)), (col(4), None, True, 1.0, (BF16,)),
         (col(7), 2, True, log2_scale, (BF16,)), (col(8), 3, True, 1.0, (F32,)),
         (col(9), None, False, 1.0, (BF16,))],
        [p["qn_a"], p["kn_a"], p["qn_b"], p["kn_b"]], pos, ("idx", col(5), col(6), p["kn_idx"]), tm)
    bias = _idx_prompt_t(qit, wit, r3(t["ki16"]), nsel, min(tq, IDX_Q_TILE), tq, tk)
    o_a = _flash_t("dsa", r3(t["ka16"]), qat, vat16, tq, tk, bias=bias)
    o_b = _flash_t("diff", r3(t["kb16"]), qbt, vbt16, tq, tk, lam=lam)
    y = _out0(x.reshape(b * seq, d), o_a.reshape(b * seq, -1), t["ga"], o_b.reshape(b * seq, -1), t["gb"], p,
              lam_init, tm)
    new = (_token_major(kat, N_HEADS_A, HEAD_DIM), _token_major(vat, N_HEADS_A, HEAD_DIM), _token_major(kit, IDX_DIM),
           _token_major(kbt, N_HEADS_B, 2, HEAD_DIM), t["vb"].reshape(b, seq, N_HEADS_B, 2 * HEAD_DIM))
    return y.reshape(b, seq, d), new


def _even_sample(x, layer, caches, page_table, p):
    cache_k_a, cache_v_a, cache_k_i, cache_k_b, cache_v_b = caches
    b, n_new, d = x.shape
    n_pages = page_table.shape[1]
    page = cache_k_a.shape[1]
    past = n_pages * page
    pos = jnp.tile(past + jnp.arange(n_new), b)
    m = b * n_new
    t = _proj0(x.reshape(m, d), pos, p, _tile_for(m, ROW_TILE),
               ("qa", "ka", "va", "ga", "qi", "ki", "wi", "qb", "kb", "vb", "gb"))
    lam_init = 0.8 - 0.6 * math.exp(-0.3 * layer)
    lam = _diff_lambda(p, lam_init)
    nsel = min(TOPK_MAX, (past + n_new) // 4)
    r3 = lambda a: a.reshape(b, n_new, a.shape[-1])
    q_st = jnp.swapaxes(t["qi"].reshape(b, n_new, N_IDX_HEADS, IDX_DIM), 1, 2).reshape(
        b, N_IDX_HEADS * n_new, IDX_DIM)
    w_st = jnp.swapaxes(t["wi"].reshape(b, n_new, N_IDX_HEADS), 1, 2).reshape(b, N_IDX_HEADS * n_new, 1)
    w_st = jnp.broadcast_to(w_st, (b, N_IDX_HEADS * n_new, LANES))
    bias, bias_new = _idx_decode(page_table, q_st, w_st, _feature_major_pages(cache_k_i),
                                 _feature_major_new(r3(t["ki"]), page), nsel)
    o_a = _attn_decode("dsa", page_table, r3(t["qa"]), _feature_major_pages(cache_k_a),
                       _feature_major_pages(cache_v_a), _feature_major_new(r3(t["ka"]), page),
                       _feature_major_new(r3(t["va"]), page), bias=bias, bias_new=bias_new)
    o_b = _attn_decode("diff", page_table, r3(t["qb"]), _feature_major_pages(cache_k_b), cache_v_b,
                       _feature_major_new(r3(t["kb"]), page), _pad_rows(r3(t["vb"]), page), lam=lam)
    y = _out0(x.reshape(m, d), o_a.reshape(m, -1), t["ga"], o_b.reshape(m, -1), t["gb"], p, lam_init,
              _tile_for(m, ROW_TILE))
    new = (t["ka"].reshape(b, n_new, N_HEADS_A, HEAD_DIM), t["va"].reshape(b, n_new, N_HEADS_A, HEAD_DIM),
           t["ki"].reshape(b, n_new, IDX_DIM), t["kb"].reshape(b, n_new, N_HEADS_B, 2, HEAD_DIM),
           t["vb"].reshape(b, n_new, N_HEADS_B, 2 * HEAD_DIM))
    return y.reshape(b, n_new, d), new


def _odd_prompt(x, p, tiles):
    b, seq, d = x.shape
    tm, tq, tk = tiles
    t = _proj1(x.reshape(b * seq, d), p, tm, ("k16", "gate"))
    wc = WIDTH_C
    w_in = p["w_in"]
    qt, kt, vt16, vt, logf_t = _proj_t(
        x, p["norm"],
        [(w_in[:, 0:wc], 0, False, QK_SCALE * LOG2E, (BF16,)), (w_in[:, wc:2 * wc], 1, False, 1.0, (F32,)),
         (w_in[:, 2 * wc:3 * wc], None, False, 1.0, (BF16, F32))],
        [p["qn"], p["kn"]], None, ("logf", w_in[:, 4 * wc:], p["b_f"]), tm)
    n_blk = seq // LANES
    pages = jnp.swapaxes(logf_t.reshape(b, N_HEADS_C, n_blk, LANES), 1, 2).reshape(b * n_blk, N_HEADS_C, LANES)
    ident = jnp.arange(b * n_blk, dtype=I32).reshape(b, n_blk)
    c_t, _ = _cumsum_pages(ident, pages, jnp.zeros((b, N_HEADS_C, LANES), F32))
    o = _flash_t("fox", t["k16"].reshape(b, seq, wc), qt, vt16, tq, tk, cqt=c_t, ck=jnp.swapaxes(c_t, 1, 2))
    y = _out1(x.reshape(b * seq, d), o.reshape(b * seq, -1), t["gate"], p, tm)
    new = (_token_major(kt, N_HEADS_C, HEAD_DIM), _token_major(vt, N_HEADS_C, HEAD_DIM), _token_major(logf_t, N_HEADS_C))
    return y.reshape(b, seq, d), new


def _odd_sample(x, caches, page_table, p):
    cache_k, cache_v, cache_logf = caches
    b, n_new, d = x.shape
    page = cache_k.shape[1]
    m = b * n_new
    t = _proj1(x.reshape(m, d), p, _tile_for(m, ROW_TILE), ("q16", "k", "v", "gate", "logf"))
    r3 = lambda a: a.reshape(b, n_new, a.shape[-1])
    c_past_t, c_new_t = _cumsum_pages(page_table, _feature_major_pages(cache_logf.astype(F32)),
                                      _feature_major_new(r3(t["logf"]), page))
    cq = jnp.broadcast_to(c_new_t[:, :, :n_new].reshape(b, N_HEADS_C * n_new, 1), (b, N_HEADS_C * n_new, LANES))
    o = _attn_decode("fox", page_table, r3(t["q16"]), _feature_major_pages(cache_k), _feature_major_pages(cache_v),
                     _feature_major_new(r3(t["k"]), page), _feature_major_new(r3(t["v"]), page),
                     cq=cq, ck=c_past_t, cn=c_new_t)
    y = _out1(x.reshape(m, d), o.reshape(m, -1), t["gate"], p, _tile_for(m, ROW_TILE))
    new = (t["k"].reshape(b, n_new, N_HEADS_C, HEAD_DIM), t["v"].reshape(b, n_new, N_HEADS_C, HEAD_DIM),
           t["logf"].reshape(b, n_new, N_HEADS_C))
    return y.reshape(b, n_new, d), new


def kernel(x_prompt, x_sample, cache_l0_k_a, cache_l0_v_a, cache_l0_k_idx, cache_l0_k_b, cache_l0_v_b,
           cache_l1_k_c, cache_l1_v_c, cache_l1_logf_c, page_table,
           l0_norm, l0_w_in, l0_qn_a, l0_kn_a, l0_kn_idx, l0_qn_b, l0_kn_b,
           l0_lam_q1, l0_lam_k1, l0_lam_q2, l0_lam_k2, l0_subln_b, l0_w_out,
           l1_norm, l1_w_in, l1_b_f, l1_qn, l1_kn, l1_w_out):
    p0 = dict(norm=l0_norm, w_in=l0_w_in, qn_a=l0_qn_a, kn_a=l0_kn_a, kn_idx=l0_kn_idx, qn_b=l0_qn_b, kn_b=l0_kn_b,
              lam_q1=l0_lam_q1, lam_k1=l0_lam_k1, lam_q2=l0_lam_q2, lam_k2=l0_lam_k2, subln_b=l0_subln_b,
              w_out=l0_w_out)
    p1 = dict(norm=l1_norm, w_in=l1_w_in, b_f=l1_b_f, qn=l1_qn, kn=l1_kn, w_out=l1_w_out)
    b, seq, _ = x_prompt.shape
    tiles = (_tile_for(b * seq, ROW_TILE), _tile_for(seq, Q_TILE), _tile_for(seq, K_TILE))
    page_table = page_table.astype(I32)
    xp, sp0 = _even_prompt(x_prompt, 0, p0, tiles)
    xs, ss0 = _even_sample(x_sample, 0, (cache_l0_k_a, cache_l0_v_a, cache_l0_k_idx, cache_l0_k_b, cache_l0_v_b),
                           page_table, p0)
    xp, sp1 = _odd_prompt(xp, p1, tiles)
    xs, ss1 = _odd_sample(xs, (cache_l1_k_c, cache_l1_v_c, cache_l1_logf_c), page_table, p1)
    (p_k_a, p_v_a, p_k_idx, p_k_b, p_v_b), (p_k_c, p_v_c, p_logf_c) = sp0, sp1
    (s_k_a, s_v_a, s_k_idx, s_k_b, s_v_b), (s_k_c, s_v_c, s_logf_c) = ss0, ss1
    return (xp, xs, p_k_a, s_k_a, p_v_a, s_v_a, p_k_idx, s_k_idx, p_k_b, s_k_b, p_v_b, s_v_b,
            p_k_c, s_k_c, p_v_c, s_v_c, p_logf_c, s_logf_c)
```
